```python
import math
import jax, jax.numpy as jnp
from jax import lax
import numpy as np

D_MODEL = 1024
BATCH = 8
SEQ = 4096
DEPTH = 1
DEC_BATCH = 128
DEC_SEQ = 4
PAST_LEN = 8192
PAGE_SIZE = 128

NSA_HEADS = 8
NSA_KV_HEADS = 2
NSA_GROUP = NSA_HEADS // NSA_KV_HEADS
HEAD_DIM = 64
NSA_WIDTH = NSA_HEADS * HEAD_DIM
KV_WIDTH = NSA_KV_HEADS * HEAD_DIM
CMP_BLOCK = 32
CMP_STRIDE = 16
CMP_RATIO = CMP_BLOCK // CMP_STRIDE
CMP_HIDDEN = 2 * HEAD_DIM
SEL_BLOCK = 64
SEL_TOP = 16
SEL_LOCAL = 2
WINDOW = 512
Q_BLOCK = 128
HG_WIDTH = D_MODEL // 2
HG_EXPAND = 128
HG_HEADS = HG_WIDTH // HG_EXPAND
HG_DK = HG_EXPAND
HG_DV = HG_WIDTH // HG_HEADS
HG_CHUNK = 64
D_FF = 4 * D_MODEL
PLE_DIM = 256
RMS_EPS = 1e-6
NEG_INF = -1e30
IN_COLS = NSA_WIDTH + 6 * KV_WIDTH + 3 * NSA_HEADS + 4 * HG_WIDTH + 2 * D_MODEL

kernel_name = 'nsa_hgrn2_gated_parallel_decoder_step'


def rms_norm(x, g):
    x32 = x.astype(jnp.float32)
    y = x32 * lax.rsqrt(jnp.mean(jnp.square(x32), axis=-1, keepdims=True) + RMS_EPS)
    return (y * g.astype(jnp.float32)).astype(x.dtype)


def masked_softmax(s, mask):
    return jax.nn.softmax(jnp.where(mask, s, NEG_INF), axis=-1) * mask


def alibi_slopes():
    h = jnp.arange(1, NSA_HEADS + 1, dtype=jnp.float32)
    return jnp.exp2(-8.0 * h / NSA_HEADS).reshape(NSA_KV_HEADS, NSA_GROUP)


def split_in(z):
    sizes = (NSA_WIDTH, 2 * KV_WIDTH, 2 * KV_WIDTH, 2 * KV_WIDTH, 3 * NSA_HEADS,
             HG_WIDTH, HG_WIDTH, HG_WIDTH, HG_WIDTH)
    offsets = [int(v) for v in np.cumsum(sizes)]
    return jnp.split(z, offsets, axis=-1)


def project_in(xn, lp):
    B, T = xn.shape[:2]
    q, kv_c, kv_s, kv_w, g_nsa, hq, hf, hi, hg, g_merge = split_in(xn @ lp['w_in'])
    kv_shape = (B, T, 2, NSA_KV_HEADS, HEAD_DIM)
    return (q.reshape(B, T, NSA_HEADS, HEAD_DIM), kv_c.reshape(kv_shape), kv_s.reshape(kv_shape),
            kv_w.reshape(kv_shape), g_nsa, hq, hf, hi, hg, g_merge)


def compress(rows, w1, w2, pos):
    B, L = rows.shape[:2]
    n_seg = L // CMP_STRIDE
    n_cmp = n_seg - CMP_RATIO + 1
    seg = rows[:, : n_seg * CMP_STRIDE].reshape(B, n_seg, CMP_STRIDE, NSA_KV_HEADS, HEAD_DIM)
    hidden = jnp.einsum('rsd,rsdh->h', pos, w1)
    for r in range(CMP_RATIO):
        hidden = hidden + jnp.einsum('bnskd,sdh->bnkh', seg, w1[r])[:, r: r + n_cmp]
    return jnp.einsum('bnkh,hd->bnkd', jax.nn.gelu(hidden), w2)


def compress_kv(kv, lp):
    return (compress(kv[:, :, 0], lp['cmp_k_w1'], lp['cmp_k_w2'], lp['cmp_pos']),
            compress(kv[:, :, 1], lp['cmp_v_w1'], lp['cmp_v_w2'], lp['cmp_pos']))


def to_sel_blocks(rows):
    B, L = rows.shape[:2]
    n_sel = -(-L // SEL_BLOCK)
    rows = jnp.pad(rows, ((0, 0), (0, n_sel * SEL_BLOCK - L), (0, 0), (0, 0)))
    return rows.reshape(B, n_sel, SEL_BLOCK, NSA_KV_HEADS, HEAD_DIM).transpose(0, 3, 1, 2, 4)


def nsa_core(q, q_pos, g_nsa, kc, vc, ksb, vsb, kw, vw, kw_pos):
    B, Tq = q.shape[:2]
    f32 = jnp.float32
    scale = HEAD_DIM ** -0.5
    slopes = alibi_slopes()[None, :, :, None, None]
    qg = q.reshape(B, Tq, NSA_KV_HEADS, NSA_GROUP, HEAD_DIM)
    qpf = q_pos.astype(f32)

    n_cmp = kc.shape[1]
    cmp_start = jnp.arange(n_cmp) * CMP_STRIDE
    cmp_end = cmp_start + (CMP_BLOCK - 1)
    d_cmp = qpf[:, None] - cmp_end[None].astype(f32)
    s_cmp = jnp.einsum('btkgd,bnkd->bkgtn', qg, kc).astype(f32) * scale - slopes * d_cmp
    p_cmp = masked_softmax(s_cmp, d_cmp >= 0)
    o_cmp = jnp.einsum('bkgtn,bnkd->btkgd', p_cmp.astype(vc.dtype), vc)

    n_sel = ksb.shape[2]
    sel_start = jnp.arange(n_sel) * SEL_BLOCK
    cover = ((cmp_start[:, None] < sel_start[None] + SEL_BLOCK)
             & (cmp_start[:, None] + CMP_BLOCK > sel_start[None])).astype(f32)
    importance = jnp.einsum('bkgtn,nj->bktj', p_cmp, cover)
    back = (q_pos // SEL_BLOCK)[:, None] - jnp.arange(n_sel)[None]
    forced = (jnp.arange(n_sel)[None] == 0) | ((back >= 0) & (back < SEL_LOCAL))
    score = jnp.where(forced, 1e9, jnp.where(back >= 0, importance, -1e9))
    _, top_idx = lax.top_k(score, min(SEL_TOP, n_sel))
    b_idx = jnp.arange(B)[:, None, None]
    k_idx = jnp.arange(NSA_KV_HEADS)[None, :, None]
    offs = jnp.arange(SEL_BLOCK)

    def sel_step(carry, blk):
        m, l, acc = carry
        kb = ksb[b_idx, k_idx, blk]
        vb = vsb[b_idx, k_idx, blk]
        dist = qpf[None, None, :, None] - (blk[..., None] * SEL_BLOCK + offs).astype(f32)
        ok = (dist >= 0)[:, :, None]
        s = jnp.einsum('btkgd,bktsd->bkgts', qg, kb).astype(f32) * scale - slopes * dist[:, :, None]
        s = jnp.where(ok, s, NEG_INF)
        m_new = jnp.maximum(m, s.max(-1))
        alpha = jnp.exp(m - m_new)
        p = jnp.exp(s - m_new[..., None]) * ok
        l = l * alpha + p.sum(-1)
        acc = acc * alpha[..., None] + jnp.einsum('bkgts,bktsd->bkgtd', p, vb.astype(f32))
        return (m_new, l, acc), None

    stat_shape = (B, NSA_KV_HEADS, NSA_GROUP, Tq)
    init = (jnp.full(stat_shape, NEG_INF, f32), jnp.zeros(stat_shape, f32),
            jnp.zeros(stat_shape + (HEAD_DIM,), f32))
    (_, l_sel, acc), _ = lax.scan(sel_step, init, jnp.moveaxis(top_idx, -1, 0))
    o_sel = (acc / l_sel[..., None]).transpose(0, 3, 1, 2, 4)

    d_win = qpf[:, None] - kw_pos[None].astype(f32)
    win_ok = (d_win >= 0) & (d_win < WINDOW) & (kw_pos >= 0)[None]
    s_win = jnp.einsum('btkgd,bskd->bkgts', qg, kw).astype(f32) * scale - slopes * d_win
    p_win = masked_softmax(s_win, win_ok)
    o_win = jnp.einsum('bkgts,bskd->btkgd', p_win.astype(vw.dtype), vw)

    g = jax.nn.sigmoid(g_nsa.astype(f32)).reshape(B, Tq, NSA_KV_HEADS, NSA_GROUP, 3)
    o = g[..., 0:1] * o_cmp.astype(f32) + g[..., 1:2] * o_sel + g[..., 2:3] * o_win.astype(f32)
    return o.reshape(B, Tq, NSA_WIDTH).astype(q.dtype)


def nsa_prompt(q, g_nsa, kc, vc, ksb, vsb, kw, vw):
    B, T = q.shape[:2]
    qb = min(Q_BLOCK, T)
    pad = ((0, 0), (WINDOW, 0), (0, 0), (0, 0))
    kw_pad, vw_pad = jnp.pad(kw, pad), jnp.pad(vw, pad)

    def one_block(i):
        t0 = i * qb
        return nsa_core(lax.dynamic_slice_in_dim(q, t0, qb, axis=1), t0 + jnp.arange(qb),
                        lax.dynamic_slice_in_dim(g_nsa, t0, qb, axis=1), kc, vc, ksb, vsb,
                        lax.dynamic_slice_in_dim(kw_pad, t0, WINDOW + qb, axis=1),
                        lax.dynamic_slice_in_dim(vw_pad, t0, WINDOW + qb, axis=1),
                        t0 - WINDOW + jnp.arange(WINDOW + qb))

    o = lax.map(one_block, jnp.arange(T // qb))
    return jnp.swapaxes(o, 0, 1).reshape(B, T, NSA_WIDTH)


def hgrn2_chunked(q, k, v, log_f, s0):
    B, T, H, DK = q.shape
    DV = v.shape[-1]
    C = math.gcd(T, HG_CHUNK)
    n = T // C

    def chunks(a):
        return jnp.swapaxes(a.reshape(B, n, C, *a.shape[2:]), 0, 1)

    tri = jnp.tril(jnp.ones((C, C), dtype=bool))[None, :, :, None, None]

    def step(s, xs):
        qc, kc, vc, lc = xs
        b = jnp.cumsum(lc, axis=1)
        decay = jnp.exp(jnp.where(tri, b[:, :, None] - b[:, None], -jnp.inf))
        scores = jnp.einsum('bthk,bshk,btshk->bhts', qc, kc, decay)
        o = (jnp.einsum('bthk,bhkv->bthv', qc * jnp.exp(b), s)
             + jnp.einsum('bhts,bshv->bthv', scores, vc))
        b_last = b[:, -1:]
        s_new = (jnp.exp(b_last[:, 0])[..., None] * s
                 + jnp.einsum('bshk,bshv->bhkv', kc * jnp.exp(b_last - b), vc))
        return s_new, o

    s_fin, o = lax.scan(step, s0, (chunks(q), chunks(k), chunks(v), chunks(log_f)))
    return jnp.swapaxes(o, 0, 1).reshape(B, T, H, DV), s_fin


def hgrn2(hq, hf, hi, hg, lb, norm_g, s0):
    B, T = hq.shape[:2]
    f32 = jnp.float32
    shp = (B, T, HG_HEADS, HG_DK)
    pre = hf.astype(f32)
    log_f = jnp.log(lb + (1.0 - lb) * jax.nn.sigmoid(pre)).reshape(shp)
    k = ((1.0 - lb) * jax.nn.sigmoid(-pre)).reshape(shp)
    q = hq.astype(f32).reshape(shp)
    v = hi.astype(f32).reshape(B, T, HG_HEADS, HG_DV)
    o, s_fin = hgrn2_chunked(q, k, v, log_f, s0.astype(f32))
    o = rms_norm(o, norm_g) * jax.nn.silu(hg.astype(f32).reshape(B, T, HG_HEADS, HG_DV))
    return o.reshape(B, T, HG_WIDTH).astype(hq.dtype), s_fin


def merge_branches(o_nsa, o_hg, g_merge, lp):
    g_a, g_b = jnp.split(g_merge, 2, axis=-1)
    y = (jax.nn.sigmoid(g_a) * (o_nsa @ lp['w_branch_nsa'])
         + jax.nn.sigmoid(g_b) * (o_hg @ lp['w_branch_hgrn']))
    return y @ lp['w_out']


def mixer_prompt(xn, lp):
    B, T = xn.shape[:2]
    q, kv_c, kv_s, kv_w, g_nsa, hq, hf, hi, hg, g_merge = project_in(xn, lp)
    kc, vc = compress_kv(kv_c, lp)
    o_nsa = nsa_prompt(q, g_nsa, kc, vc, to_sel_blocks(kv_s[:, :, 0]), to_sel_blocks(kv_s[:, :, 1]),
                       kv_w[:, :, 0], kv_w[:, :, 1])
    s0 = jnp.zeros((B, HG_HEADS, HG_DK, HG_DV), jnp.float32)
    o_hg, s_fin = hgrn2(hq, hf, hi, hg, lp['hg_lb'], lp['hg_norm'], s0)
    out = merge_branches(o_nsa, o_hg, g_merge, lp)
    return out, kv_c, kv_s, kv_w[:, T - min(WINDOW, T):], s_fin


def mixer_sample(xn, cache_cmp_kv, cache_sel_kv, cache_win_kv, state_hgrn, page_table, layer, lp):
    B, T = xn.shape[:2]
    past_len = page_table.shape[1] * cache_cmp_kv.shape[2]
    wb = cache_win_kv.shape[2]
    q, kv_c, kv_s, kv_w, g_nsa, hq, hf, hi, hg, g_merge = project_in(xn, lp)

    def gather_past(cache, new):
        past = cache[layer, page_table].reshape(B, past_len, 2, NSA_KV_HEADS, HEAD_DIM).astype(new.dtype)
        return jnp.concatenate([past, new], axis=1)

    full_c = gather_past(cache_cmp_kv, kv_c)
    full_s = gather_past(cache_sel_kv, kv_s)
    full_w = jnp.concatenate([cache_win_kv[layer].astype(kv_w.dtype), kv_w], axis=1)
    kc, vc = compress_kv(full_c, lp)
    o_nsa = nsa_core(q, past_len + jnp.arange(T), g_nsa, kc, vc,
                     to_sel_blocks(full_s[:, :, 0]), to_sel_blocks(full_s[:, :, 1]),
                     full_w[:, :, 0], full_w[:, :, 1], past_len - wb + jnp.arange(wb + T))
    o_hg, s_new = hgrn2(hq, hf, hi, hg, lp['hg_lb'], lp['hg_norm'], state_hgrn[layer])
    out = merge_branches(o_nsa, o_hg, g_merge, lp)
    return out, kv_c, kv_s, full_w[:, T:], s_new


def finish_layer(h, mix, p, lp):
    h = h + rms_norm(mix, lp['norm_post_mix'])
    xn = rms_norm(h, lp['norm_pre_mlp'])
    ffn = jnp.square(jax.nn.relu(xn @ lp['w_mlp_up'])) @ lp['w_mlp_down']
    h = h + rms_norm(ffn, lp['norm_post_mlp'])
    gate = jax.nn.sigmoid(rms_norm(h, lp['norm_ple']) @ lp['w_ple_gate'])
    return h + gate * (p.astype(h.dtype) @ lp['w_ple_proj'])


def setup_inputs(seed: int = 0) -> dict:
    key = jax.random.key(seed)
    ks = jax.random.split(key, 32)
    f32 = jnp.float32
    n_pages = PAST_LEN // PAGE_SIZE
    n_used = DEC_BATCH * n_pages
    n_phys = n_used + n_used // 4
    win_buf = min(WINDOW, PAST_LEN)

    def nrm(k, shape, scale=1.0):
        return jax.random.normal(k, shape, f32) * scale

    def gain(k, width=D_MODEL):
        return 1.0 + nrm(k, (DEPTH, width), 0.01)

    page_table = jax.random.permutation(ks[0], n_phys)[:n_used].reshape(DEC_BATCH, n_pages).astype(jnp.int32)
    kv_row = (2, NSA_KV_HEADS, HEAD_DIM)
    return {
        'x_prompt': nrm(ks[1], (BATCH, SEQ, D_MODEL)),
        'x_sample': nrm(ks[2], (DEC_BATCH, DEC_SEQ, D_MODEL)),
        'cache_cmp_kv': nrm(ks[3], (DEPTH, n_phys, PAGE_SIZE) + kv_row),
        'cache_sel_kv': nrm(ks[4], (DEPTH, n_phys, PAGE_SIZE) + kv_row),
        'cache_win_kv': nrm(ks[5], (DEPTH, DEC_BATCH, win_buf) + kv_row),
        'state_hgrn': nrm(ks[6], (DEPTH, DEC_BATCH, HG_HEADS, HG_DK, HG_DV), 0.5),
        'page_table': page_table,
        'p_prompt': nrm(ks[7], (DEPTH, BATCH, SEQ, PLE_DIM)),
        'p_sample': nrm(ks[8], (DEPTH, DEC_BATCH, DEC_SEQ, PLE_DIM)),
        'w_in': nrm(ks[9], (DEPTH, D_MODEL, IN_COLS), D_MODEL ** -0.5),
        'cmp_k_w1': nrm(ks[10], (DEPTH, CMP_RATIO, CMP_STRIDE, HEAD_DIM, CMP_HIDDEN), (CMP_BLOCK * HEAD_DIM) ** -0.5),
        'cmp_k_w2': nrm(ks[11], (DEPTH, CMP_HIDDEN, HEAD_DIM), CMP_HIDDEN ** -0.5),
        'cmp_v_w1': nrm(ks[12], (DEPTH, CMP_RATIO, CMP_STRIDE, HEAD_DIM, CMP_HIDDEN), (CMP_BLOCK * HEAD_DIM) ** -0.5),
        'cmp_v_w2': nrm(ks[13], (DEPTH, CMP_HIDDEN, HEAD_DIM), CMP_HIDDEN ** -0.5),
        'cmp_pos': nrm(ks[14], (DEPTH, CMP_RATIO, CMP_STRIDE, HEAD_DIM), 0.1),
        'hg_lb_logits': nrm(ks[15], (DEPTH + 1, HG_WIDTH), 0.1),
        'hg_norm': gain(ks[16], HG_DV),
        'w_branch_nsa': nrm(ks[17], (DEPTH, NSA_WIDTH, D_MODEL), NSA_WIDTH ** -0.5),
        'w_branch_hgrn': nrm(ks[18], (DEPTH, HG_WIDTH, D_MODEL), HG_WIDTH ** -0.5),
        'w_out': nrm(ks[19], (DEPTH, D_MODEL, D_MODEL), D_MODEL ** -0.5),
        'norm_pre_mix': gain(ks[20]),
        'norm_post_mix': gain(ks[21]),
        'norm_pre_mlp': gain(ks[22]),
        'norm_post_mlp': gain(ks[23]),
        'w_mlp_up': nrm(ks[24], (DEPTH, D_MODEL, D_FF), D_MODEL ** -0.5),
        'w_mlp_down': nrm(ks[25], (DEPTH, D_FF, D_MODEL), D_FF ** -0.5),
        'norm_ple': gain(ks[26]),
        'w_ple_gate': nrm(ks[27], (DEPTH, D_MODEL, D_MODEL), D_MODEL ** -0.5),
        'w_ple_proj': nrm(ks[28], (DEPTH, PLE_DIM, D_MODEL), PLE_DIM ** -0.5),
    }


def reference(x_prompt, x_sample, cache_cmp_kv, cache_sel_kv, cache_win_kv, state_hgrn, page_table,
              p_prompt, p_sample, w_in, cmp_k_w1, cmp_k_w2, cmp_v_w1, cmp_v_w2, cmp_pos, hg_lb_logits,
              hg_norm, w_branch_nsa, w_branch_hgrn, w_out, norm_pre_mix, norm_post_mix, norm_pre_mlp,
              norm_post_mlp, w_mlp_up, w_mlp_down, norm_ple, w_ple_gate, w_ple_proj):
    lb_all = jnp.cumsum(jax.nn.softmax(hg_lb_logits.astype(jnp.float32), axis=0), axis=0)
    h_p, h_s = x_prompt, x_sample
    outs = [[] for _ in range(8)]
    for i in range(DEPTH):
        lp = {'w_in': w_in[i], 'cmp_k_w1': cmp_k_w1[i], 'cmp_k_w2': cmp_k_w2[i], 'cmp_v_w1': cmp_v_w1[i],
              'cmp_v_w2': cmp_v_w2[i], 'cmp_pos': cmp_pos[i], 'hg_lb': lb_all[i], 'hg_norm': hg_norm[i],
              'w_branch_nsa': w_branch_nsa[i], 'w_branch_hgrn': w_branch_hgrn[i], 'w_out': w_out[i],
              'norm_post_mix': norm_post_mix[i], 'norm_pre_mlp': norm_pre_mlp[i],
              'norm_post_mlp': norm_post_mlp[i], 'w_mlp_up': w_mlp_up[i], 'w_mlp_down': w_mlp_down[i],
              'norm_ple': norm_ple[i], 'w_ple_gate': w_ple_gate[i], 'w_ple_proj': w_ple_proj[i]}
        mix, c_p, s_p, w_p, st_p = mixer_prompt(rms_norm(h_p, norm_pre_mix[i]), lp)
        h_p = finish_layer(h_p, mix, p_prompt[i], lp)
        mix, c_s, s_s, w_s, st_s = mixer_sample(rms_norm(h_s, norm_pre_mix[i]), cache_cmp_kv, cache_sel_kv,
                                                cache_win_kv, state_hgrn, page_table, i, lp)
        h_s = finish_layer(h_s, mix, p_sample[i], lp)
        for lst, v in zip(outs, (c_p, s_p, w_p, st_p, c_s, s_s, w_s, st_s)):
            lst.append(v)
    cmp_p, sel_p, win_p, hg_p, cmp_s, sel_s, win_s, hg_s = [jnp.stack(lst, axis=0) for lst in outs]
    return (h_p, h_s, cmp_p, sel_p, win_p, hg_p, cmp_s, sel_s, win_s, hg_s)
```

```python
import functools
import math

import numpy as np
import jax
import jax.numpy as jnp
from jax import lax
from jax.experimental import pallas as pl
from jax.experimental.pallas import tpu as pltpu

F32 = jnp.float32
BF16 = jnp.bfloat16

D_MODEL = 1024
NSA_HEADS = 8
KVH = 2
GROUP = NSA_HEADS // KVH
HD = 64
NSA_W = NSA_HEADS * HD
KV_W = KVH * HD
CMP_BLOCK = 32
CMP_STRIDE = 16
CMP_HIDDEN = 2 * HD
SEL_BLOCK = 64
SEL_TOP = 16
SEL_LOCAL = 2
WINDOW = 512
HG_W = 512
HG_H = 4
HG_D = 128
HG_CHUNK = 64
D_FF = 4 * D_MODEL
PLE_DIM = 256
RMS_EPS = 1e-6
NEG = -1e30
M_INIT = -1e20
SCALE = HD ** -0.5
HG_SAFE_EXP = 60.0

LANE = 128
SUBLANE = 8
VMEM_LIMIT = 48 * 1024 * 1024

C_GM = 0
C_Q = 2048
C_HQ = 2560
C_HF = 3072
C_HI = 3584
C_HG = 4096
C_KVC = 4608
C_KVS = 4864
C_KVW = 5120
C_GN = 5376
Z_COLS = 5632
QB = 128
HI = lax.Precision.HIGHEST


def _cparams(sem, vmem=VMEM_LIMIT):
    return pltpu.CompilerParams(dimension_semantics=sem, vmem_limit_bytes=vmem)


def _rms(x, g):
    return x * lax.rsqrt(jnp.mean(x * x, axis=-1, keepdims=True) + RMS_EPS) * g


def _sigmoid(x):
    return 1.0 / (1.0 + jnp.exp(-x))


def _gelu_tanh(x):
    return 0.5 * x * (1.0 + jnp.tanh(math.sqrt(2.0 / math.pi) * (x + 0.044715 * (x * x * x))))


def _const_spec(shape):
    nd = len(shape)
    return pl.BlockSpec(shape, lambda *_: (0,) * nd)


def _norm_matmul_kernel(x_ref, g_ref, w_ref, o_ref, xn_ref):
    @pl.when(pl.program_id(1) == 0)
    def _():
        xn_ref[...] = _rms(x_ref[...], g_ref[...]).astype(BF16)

    o_ref[...] = jnp.dot(xn_ref[...], w_ref[...], preferred_element_type=F32)


def _norm_matmul(x, g, w, tm, tn):
    n, d = x.shape
    c = w.shape[1]
    return pl.pallas_call(
        _norm_matmul_kernel,
        grid=(n // tm, c // tn),
        in_specs=[pl.BlockSpec((tm, d), lambda i, j: (i, 0)),
                  pl.BlockSpec((1, d), lambda i, j: (0, 0)),
                  pl.BlockSpec((d, tn), lambda i, j: (0, j))],
        out_specs=pl.BlockSpec((tm, tn), lambda i, j: (i, j)),
        out_shape=jax.ShapeDtypeStruct((n, c), F32),
        scratch_shapes=[pltpu.VMEM((tm, d), BF16)],
        compiler_params=_cparams(("parallel", "arbitrary")),
        name="norm_in_proj",
    )(x, g, w)


def _compress_math(load_chunk, n_seg, w1_ref, w2_ref, pos_ref, kv):
    x = jnp.concatenate([load_chunk(2 * s + kv) for s in range(CMP_STRIDE)], axis=1).astype(BF16)
    h0 = jnp.dot(x, w1_ref[kv, 0], preferred_element_type=F32)
    h1 = jnp.dot(x, w1_ref[kv, 1], preferred_element_type=F32)
    posb = (jnp.dot(pos_ref[0], w1_ref[kv, 0], preferred_element_type=F32)
            + jnp.dot(pos_ref[1], w1_ref[kv, 1], preferred_element_type=F32))
    hid = h0 + pltpu.roll(h1, n_seg - 1, axis=0) + posb[0:1]
    return jnp.dot(_gelu_tanh(hid).astype(BF16), w2_ref[kv], preferred_element_type=F32)


def _compress_prompt_kernel(x_ref, w1_ref, w2_ref, pos_ref, kc_ref, vc_ref):
    n_seg = x_ref.shape[0]
    load = lambda c: x_ref[:, c * LANE:(c + 1) * LANE]
    kc_ref[...] = _compress_math(load, n_seg, w1_ref, w2_ref, pos_ref, 0).astype(BF16)
    vc_ref[...] = _compress_math(load, n_seg, w1_ref, w2_ref, pos_ref, 1).astype(BF16)


def _compress_prompt(segs, w1, w2, pos):
    b, n_seg, width = segs.shape
    return pl.pallas_call(
        _compress_prompt_kernel,
        grid=(b,),
        in_specs=[pl.BlockSpec((None, n_seg, width), lambda i: (i, 0, 0)),
                  _const_spec(w1.shape), _const_spec(w2.shape), _const_spec(pos.shape)],
        out_specs=[pl.BlockSpec((None, n_seg, KV_W), lambda i: (i, 0, 0)),
                   pl.BlockSpec((None, n_seg, KV_W), lambda i: (i, 0, 0))],
        out_shape=[jax.ShapeDtypeStruct((b, n_seg, KV_W), BF16),
                   jax.ShapeDtypeStruct((b, n_seg, KV_W), BF16)],
        compiler_params=_cparams(("parallel",)),
        name="compress_prompt",
    )(segs, w1, w2, pos)


def _nsa_prompt_kernel(q_ref, gn_ref, kc_ref, vct_ref, ks_ref, vst_ref, kw_ref, vwt_ref,
                       bias0_ref, slopes_ref, covert_ref, o_ref,
                       m_ref, l_ref, acc_ref, sel_ref):
    i = pl.program_id(1)
    t0 = i * QB
    n_seg = kc_ref.shape[0]
    n_sel = covert_ref.shape[0]
    nl = NSA_HEADS * QB
    half = GROUP * QB
    slopes = slopes_ref[...]

    q = q_ref[...] * SCALE
    zero = jnp.zeros((HD, QB), F32)
    cols = []
    for kvh in range(KVH):
        for p in range(GROUP // 2):
            c0 = (kvh * (GROUP // 2) + p) * LANE
            blk = q[:, c0:c0 + LANE].T
            for hh in range(2):
                piece = blk[hh * HD:(hh + 1) * HD, :]
                cols.append(jnp.concatenate([piece, zero] if kvh == 0 else [zero, piece], axis=0))
    qbdt = jnp.concatenate(cols, axis=1).astype(BF16)

    lane = lax.broadcasted_iota(jnp.int32, (QB, QB), 1)
    sub = lax.broadcasted_iota(jnp.int32, (QB, QB), 0)
    dl = (lane - sub).astype(F32)

    sc = jnp.dot(kc_ref[...], qbdt, preferred_element_type=F32)
    n_i = lax.broadcasted_iota(jnp.int32, (n_seg, QB), 0)
    t_i = lax.broadcasted_iota(jnp.int32, (n_seg, QB), 1)
    d1 = (t0 + t_i - (n_i * CMP_STRIDE + (CMP_BLOCK - 1))).astype(F32)
    dc = jnp.concatenate([d1] * NSA_HEADS, axis=1)
    okc = dc >= 0.0
    sc = jnp.where(okc, sc - slopes * dc, NEG)
    mc = jnp.max(sc, axis=0, keepdims=True)
    pc = jnp.where(okc, jnp.exp(sc - mc), 0.0)
    lc = jnp.sum(pc, axis=0, keepdims=True)
    pc = pc * (1.0 / jnp.where(lc > 0.0, lc, 1.0))
    pcb = pc.astype(BF16)
    ocmp = [jnp.dot(vct_ref[kvh * HD:(kvh + 1) * HD, :], pcb[:, kvh * half:(kvh + 1) * half],
                    preferred_element_type=F32) for kvh in range(KVH)]

    jblk = lax.broadcasted_iota(jnp.int32, (n_sel, QB), 0)
    qblk = (t0 + lax.broadcasted_iota(jnp.int32, (n_sel, QB), 1)) // SEL_BLOCK
    back = qblk - jblk
    visible = back >= 0
    forced = (jblk == 0) | (visible & (back < SEL_LOCAL))
    for kvh in range(KVH):
        psum = pc[:, kvh * half:kvh * half + QB]
        for g in range(1, GROUP):
            psum = psum + pc[:, kvh * half + g * QB:kvh * half + (g + 1) * QB]
        imp = jnp.dot(covert_ref[...], psum, precision=HI, preferred_element_type=F32)
        score = jnp.where(forced, 1e9, jnp.where(visible, imp, -1e9))
        rank = jnp.zeros((n_sel, QB), F32)
        for jp in range(n_sel):
            row = score[jp:jp + 1, :]
            ge = jnp.where(row >= score, 1.0, 0.0)
            gt = jnp.where(row > score, 1.0, 0.0)
            rank = rank + jnp.where(jblk > jp, ge, gt)
        sel_ref[kvh] = jnp.where(visible, jnp.where(rank < float(min(SEL_TOP, n_sel)), 1.0, 0.0), 0.0)

    def sweep(k_ref, vt_ref, lo, hi, tile_bias):
        m_ref[...] = jnp.full((1, nl), M_INIT, F32)
        l_ref[...] = jnp.zeros((1, nl), F32)
        acc_ref[...] = jnp.zeros((KVH, HD, half), F32)

        def body(jt, carry):
            off = (t0 - jt * QB).astype(F32)
            s = jnp.dot(k_ref[jt], qbdt, preferred_element_type=F32)
            s = s - bias0_ref[...] - slopes * off + tile_bias(jt, off)
            m_old = m_ref[...]
            m_new = jnp.maximum(m_old, jnp.max(s, axis=0, keepdims=True))
            alpha = jnp.exp(m_old - m_new)
            p = jnp.exp(s - m_new)
            l_ref[...] = alpha * l_ref[...] + jnp.sum(p, axis=0, keepdims=True)
            m_ref[...] = m_new
            pb = p.astype(BF16)
            vt = vt_ref[jt]
            for kvh in range(KVH):
                pv = jnp.dot(vt[kvh * HD:(kvh + 1) * HD, :], pb[:, kvh * half:(kvh + 1) * half],
                             preferred_element_type=F32)
                acc_ref[kvh] = alpha[:, kvh * half:(kvh + 1) * half] * acc_ref[kvh] + pv
            return carry

        lax.fori_loop(lo, hi, body, 0)
        inv = 1.0 / l_ref[...]
        return [acc_ref[kvh] * inv[:, kvh * half:(kvh + 1) * half] for kvh in range(KVH)]

    def sel_bias(jt, off):
        causal = dl >= -off
        parts = []
        for kvh in range(KVH):
            rows = sel_ref[kvh, pl.ds(jt * (QB // SEL_BLOCK), QB // SEL_BLOCK), :]
            blockmask = jnp.concatenate(
                [jnp.broadcast_to(rows[r:r + 1, :], (SEL_BLOCK, QB)) for r in range(QB // SEL_BLOCK)], axis=0)
            nb = jnp.where(causal, (blockmask - 1.0) * (-NEG), NEG)
            parts += [nb] * GROUP
        return jnp.concatenate(parts, axis=1)

    def win_bias(jt, off):
        nb = jnp.where(dl >= -off, jnp.where(dl < float(WINDOW) - off, 0.0, NEG), NEG)
        return jnp.concatenate([nb] * NSA_HEADS, axis=1)

    osel = sweep(ks_ref, vst_ref, 0, i + 1, sel_bias)
    owin = sweep(kw_ref, vwt_ref, jnp.maximum(i - WINDOW // QB, 0), i + 1, win_bias)

    gt_ = _sigmoid(gn_ref[...]).T
    for kvh in range(KVH):
        for p in range(GROUP // 2):
            pieces = []
            for hh in range(2):
                g = 2 * p + hh
                h = kvh * GROUP + g
                cs = slice(g * QB, (g + 1) * QB)
                pieces.append(gt_[3 * h:3 * h + 1, :] * ocmp[kvh][:, cs]
                              + gt_[3 * h + 1:3 * h + 2, :] * osel[kvh][:, cs]
                              + gt_[3 * h + 2:3 * h + 3, :] * owin[kvh][:, cs])
            c0 = (kvh * (GROUP // 2) + p) * LANE
            o_ref[:, c0:c0 + LANE] = jnp.concatenate(pieces, axis=0).T.astype(BF16)


def _alibi_slopes_lanes(width):
    h = np.arange(1, NSA_HEADS + 1, dtype=np.float32)
    return np.repeat(np.exp2(-8.0 * h / NSA_HEADS), width)[None, :].astype(np.float32)


def _cover_matrix(n_cmp_rows, n_sel_rows):
    n = np.arange(n_cmp_rows)[:, None] * CMP_STRIDE
    j = np.arange(n_sel_rows)[None, :] * SEL_BLOCK
    return ((n < j + SEL_BLOCK) & (n + CMP_BLOCK > j)).astype(np.float32)


def _nsa_prompt(z, kc, vct, ks, vst, kw, vwt, b, t):
    nq = t // QB
    n_seg = t // CMP_STRIDE
    n_sel = t // SEL_BLOCK
    nl = NSA_HEADS * QB
    slopes = _alibi_slopes_lanes(QB)
    dl = (np.arange(QB)[None, :] - np.arange(QB)[:, None]).astype(np.float32)
    bias0 = np.tile(dl, (1, NSA_HEADS)) * slopes
    covert = _cover_matrix(n_seg, n_sel).T.copy()
    tile4 = lambda: pl.BlockSpec((None, nq, QB, QB), lambda bi, i: (bi, 0, 0, 0))
    return pl.pallas_call(
        _nsa_prompt_kernel,
        grid=(b, nq),
        in_specs=[pl.BlockSpec((QB, NSA_W), lambda bi, i: (bi * nq + i, C_Q // NSA_W)),
                  pl.BlockSpec((QB, LANE), lambda bi, i: (bi * nq + i, C_GN // LANE)),
                  pl.BlockSpec((None, n_seg, KV_W), lambda bi, i: (bi, 0, 0)),
                  pl.BlockSpec((None, KV_W, n_seg), lambda bi, i: (bi, 0, 0)),
                  tile4(), tile4(), tile4(), tile4(),
                  _const_spec((QB, nl)), _const_spec((1, nl)), _const_spec((n_sel, n_seg))],
        out_specs=pl.BlockSpec((QB, NSA_W), lambda bi, i: (bi * nq + i, 0)),
        out_shape=jax.ShapeDtypeStruct((b * t, NSA_W), BF16),
        scratch_shapes=[pltpu.VMEM((1, nl), F32), pltpu.VMEM((1, nl), F32),
                        pltpu.VMEM((KVH, HD, GROUP * QB), F32), pltpu.VMEM((KVH, n_sel, QB), F32)],
        compiler_params=_cparams(("parallel", "arbitrary")),
        name="nsa_prompt",
    )(z, z, kc, vct, ks, vst, kw, vwt, jnp.asarray(bias0), jnp.asarray(slopes), jnp.asarray(covert))


def _hgrn_gates(pre, lb):
    log_f = jnp.log(lb + (1.0 - lb) * _sigmoid(pre))
    k = (1.0 - lb) * _sigmoid(-pre)
    return log_f, k


def _hgrn_out(o, gate, ng):
    outs = []
    for h in range(HG_H):
        sl = slice(h * HG_D, (h + 1) * HG_D)
        g = gate[:, sl]
        outs.append(_rms(o[:, sl], ng) * (g * _sigmoid(g)))
    return jnp.concatenate(outs, axis=1)


def _hgrn_prompt_kernel(hq_ref, hf_ref, hi_ref, hg_ref, lb_ref, ng_ref, tri_ref, o_ref, st_ref,
                        s_ref, oraw_ref):
    ci = pl.program_id(1)
    tc = hq_ref.shape[0]
    c = HG_CHUNK

    @pl.when(ci == 0)
    def _():
        s_ref[...] = jnp.zeros(s_ref.shape, F32)

    lb = lb_ref[...]
    tril = (lax.broadcasted_iota(jnp.int32, (c, c), 0) >= lax.broadcasted_iota(jnp.int32, (c, c), 1))
    row8 = lax.broadcasted_iota(jnp.int32, (SUBLANE, HG_D), 0)

    def chunk(cj, carry):
        r0 = pl.multiple_of(cj * c, c)
        q = hq_ref[pl.ds(r0, c), :]
        v = hi_ref[pl.ds(r0, c), :]
        log_f, k = _hgrn_gates(hf_ref[pl.ds(r0, c), :], lb)
        bcum = jnp.dot(tri_ref[...], log_f, precision=HI, preferred_element_type=F32)
        e = bcum - bcum[c // 2 - 1:c // 2, :]
        safe = jnp.max(jnp.abs(e)) < HG_SAFE_EXP

        @pl.when(safe)
        def _():
            b_last = bcum[c - 1:c, :]
            qt = (q * jnp.exp(e)).astype(BF16)
            kt = (k * jnp.exp(-e)).astype(BF16)
            qb = (q * jnp.exp(bcum)).astype(BF16)
            kh = (k * jnp.exp(b_last - bcum)).astype(BF16)
            dec = jnp.exp(b_last)
            vb = v.astype(BF16)
            for h in range(HG_H):
                sl = slice(h * HG_D, (h + 1) * HG_D)
                a = lax.dot_general(qt[:, sl], kt[:, sl], (((1,), (1,)), ((), ())), preferred_element_type=F32)
                a = jnp.where(tril, a, 0.0).astype(BF16)
                st = s_ref[h]
                o = (lax.dot_general(qb[:, sl], st.astype(BF16), (((1,), (1,)), ((), ())),
                                     preferred_element_type=F32)
                     + jnp.dot(a, vb[:, sl], preferred_element_type=F32))
                oraw_ref[pl.ds(r0, c), sl] = o
                s_ref[h] = st * dec[:, sl] + lax.dot_general(vb[:, sl], kh[:, sl], (((0,), (0,)), ((), ())),
                                                             preferred_element_type=F32)

        @pl.when(jnp.logical_not(safe))
        def _():
            for h in range(HG_H):
                sl = slice(h * HG_D, (h + 1) * HG_D)

                def tile(ti, carry2, sl=sl, h=h):
                    r = pl.multiple_of(r0 + ti * SUBLANE, SUBLANE)
                    q8 = hq_ref[pl.ds(r, SUBLANE), sl]
                    v8 = hi_ref[pl.ds(r, SUBLANE), sl]
                    lf8, k8 = _hgrn_gates(hf_ref[pl.ds(r, SUBLANE), sl], lb[:, sl])
                    f8 = jnp.exp(lf8)
                    st = s_ref[h]
                    rows_out = []
                    for u in range(SUBLANE):
                        vu = jnp.where(row8 == 0, jnp.broadcast_to(v8[u:u + 1, :], (SUBLANE, HG_D)), 0.0)
                        ku = jnp.broadcast_to(k8[u:u + 1, :], (SUBLANE, HG_D))
                        qu = jnp.broadcast_to(q8[u:u + 1, :], (SUBLANE, HG_D))
                        st = st * f8[u:u + 1, :] + lax.dot_general(
                            vu, ku, (((0,), (0,)), ((), ())), precision=HI, preferred_element_type=F32)
                        ou = lax.dot_general(qu, st, (((1,), (1,)), ((), ())), precision=HI,
                                             preferred_element_type=F32)
                        rows_out.append(ou[0:1, :])
                    s_ref[h] = st
                    oraw_ref[pl.ds(r, SUBLANE), sl] = jnp.concatenate(rows_out, axis=0)
                    return carry2

                lax.fori_loop(0, c // SUBLANE, tile, 0)

        return carry

    lax.fori_loop(0, tc // c, chunk, 0)
    o_ref[...] = _hgrn_out(oraw_ref[...], hg_ref[...], ng_ref[...]).astype(BF16)

    @pl.when(ci == pl.num_programs(1) - 1)
    def _():
        for h in range(HG_H):
            st_ref[h] = s_ref[h].T


def _hgrn_prompt(z, lb, ng, b, t, tc):
    nc = t // tc
    col = lambda c0: pl.BlockSpec((tc, HG_W), lambda bi, ci: (bi * nc + ci, c0 // HG_W))
    tri = np.tril(np.ones((HG_CHUNK, HG_CHUNK), np.float32))
    return pl.pallas_call(
        _hgrn_prompt_kernel,
        grid=(b, nc),
        in_specs=[col(C_HQ), col(C_HF), col(C_HI), col(C_HG),
                  _const_spec((1, HG_W)), _const_spec((1, HG_D)), _const_spec((HG_CHUNK, HG_CHUNK))],
        out_specs=[pl.BlockSpec((tc, HG_W), lambda bi, ci: (bi * nc + ci, 0)),
                   pl.BlockSpec((None, HG_H, HG_D, HG_D), lambda bi, ci: (bi, 0, 0, 0))],
        out_shape=[jax.ShapeDtypeStruct((b * t, HG_W), BF16),
                   jax.ShapeDtypeStruct((b, HG_H, HG_D, HG_D), F32)],
        scratch_shapes=[pltpu.VMEM((HG_H, HG_D, HG_D), F32), pltpu.VMEM((tc, HG_W), F32)],
        compiler_params=_cparams(("parallel", "arbitrary")),
        name="hgrn_prompt",
    )(z, z, z, z, lb, ng, jnp.asarray(tri))


def _merge_kernel(on_ref, oh_ref, ga_ref, gb_ref, x_ref, wn_ref, wh_ref, wo_ref, g_ref, o_ref):
    y = (_sigmoid(ga_ref[...]) * jnp.dot(on_ref[...], wn_ref[...], preferred_element_type=F32)
         + _sigmoid(gb_ref[...]) * jnp.dot(oh_ref[...], wh_ref[...], preferred_element_type=F32))
    mix = jnp.dot(y.astype(BF16), wo_ref[...], preferred_element_type=F32)
    o_ref[...] = x_ref[...] + _rms(mix, g_ref[...])


def _resident(shape):
    nd = len(shape)
    return pl.BlockSpec(shape, lambda *_: (0,) * nd, pipeline_mode=pl.Buffered(1))


def _merge(o_nsa, o_hg, z, x, wn, wh, wo, g, tm):
    n = x.shape[0]
    row = lambda w, cb: pl.BlockSpec((tm, w), lambda i: (i, cb))
    return pl.pallas_call(
        _merge_kernel,
        grid=(n // tm,),
        in_specs=[row(NSA_W, 0), row(HG_W, 0), row(D_MODEL, C_GM // D_MODEL), row(D_MODEL, C_GM // D_MODEL + 1),
                  row(D_MODEL, 0), _resident(wn.shape), _resident(wh.shape), _resident(wo.shape),
                  _resident((1, D_MODEL))],
        out_specs=row(D_MODEL, 0),
        out_shape=jax.ShapeDtypeStruct((n, D_MODEL), F32),
        compiler_params=_cparams(("parallel",)),
        name="merge_out_proj",
    )(o_nsa, o_hg, z, z, x, wn, wh, wo, g)


def _mlp_kernel(h_ref, p_ref, wu_ref, wd_ref, wg_ref, wp_ref, g1_ref, g2_ref, g3_ref, o_ref):
    h = h_ref[...]
    xn = _rms(h, g1_ref[...]).astype(BF16)
    ffn = jnp.zeros(h.shape, F32)
    step = D_MODEL
    for c0 in range(0, D_FF, step):
        up = jnp.dot(xn, wu_ref[:, c0:c0 + step], preferred_element_type=F32)
        act = jnp.square(jnp.maximum(up, 0.0)).astype(BF16)
        ffn = ffn + jnp.dot(act, wd_ref[c0:c0 + step, :], preferred_element_type=F32)
    h = h + _rms(ffn, g2_ref[...])
    gate = _sigmoid(jnp.dot(_rms(h, g3_ref[...]).astype(BF16), wg_ref[...], preferred_element_type=F32))
    o_ref[...] = h + gate * jnp.dot(p_ref[...].astype(BF16), wp_ref[...], preferred_element_type=F32)


def _mlp(h, p, wu, wd, wg, wp, g1, g2, g3, tm):
    n = h.shape[0]
    row = lambda w: pl.BlockSpec((tm, w), lambda i: (i, 0))
    gain = _resident((1, D_MODEL))
    return pl.pallas_call(
        _mlp_kernel,
        grid=(n // tm,),
        in_specs=[row(D_MODEL), row(PLE_DIM), _resident(wu.shape), _resident(wd.shape), _resident(wg.shape),
                  _resident(wp.shape), gain, gain, gain],
        out_specs=row(D_MODEL),
        out_shape=jax.ShapeDtypeStruct((n, D_MODEL), F32),
        compiler_params=_cparams(("parallel",)),
        name="mlp_ple",
    )(h, p, wu, wd, wg, wp, g1, g2, g3)


def _prep_w_in(w):
    sizes = (NSA_W, 2 * KV_W, 2 * KV_W, 2 * KV_W, 3 * NSA_HEADS, HG_W, HG_W, HG_W, HG_W)
    q, kvc, kvs, kvw, gn, hq, hf, hi, hg, gm = jnp.split(w, [int(v) for v in np.cumsum(sizes)], axis=1)
    pad = jnp.zeros((w.shape[0], Z_COLS - C_GN - 3 * NSA_HEADS), w.dtype)
    return jnp.concatenate([gm, q, hq, hf, hi, hg, kvc, kvs, kvw, gn, pad], axis=1).astype(BF16)


def _prep_compress(w1k, w2k, w1v, w2v, pos):
    eye = jnp.eye(KVH, dtype=F32)

    def big1(w1):
        t = jnp.einsum('rsdh,kq->rskdqh', w1, eye)
        return t.reshape(CMP_BLOCK // CMP_STRIDE, CMP_STRIDE * KV_W, KVH * CMP_HIDDEN)

    def big2(w2):
        return jnp.einsum('hd,kq->khqd', w2, eye).reshape(KVH * CMP_HIDDEN, KV_W)

    w1 = jnp.stack([big1(w1k), big1(w1v)]).astype(BF16)
    w2 = jnp.stack([big2(w2k), big2(w2v)]).astype(BF16)
    posb = jnp.broadcast_to(pos[:, :, None, :], pos.shape[:2] + (KVH, HD)).reshape(pos.shape[0], 1, -1)
    posb = jnp.broadcast_to(posb, (pos.shape[0], SUBLANE, posb.shape[-1])).astype(BF16)
    return w1, w2, posb


def _prep_layer(i, lb_all, w_in, cmp_k_w1, cmp_k_w2, cmp_v_w1, cmp_v_w2, cmp_pos, hg_norm, w_branch_nsa,
                w_branch_hgrn, w_out, norm_pre_mix, norm_post_mix, norm_pre_mlp, norm_post_mlp, w_mlp_up,
                w_mlp_down, norm_ple, w_ple_gate, w_ple_proj):
    w1, w2, posb = _prep_compress(cmp_k_w1[i], cmp_k_w2[i], cmp_v_w1[i], cmp_v_w2[i], cmp_pos[i])
    row = lambda a: a[i].reshape(1, -1).astype(F32)
    return {
        'w_in': _prep_w_in(w_in[i]), 'cmp_w1': w1, 'cmp_w2': w2, 'cmp_pos': posb,
        'hg_lb': lb_all[i].reshape(1, HG_W), 'hg_norm': row(hg_norm),
        'w_bn': w_branch_nsa[i].astype(BF16), 'w_bh': w_branch_hgrn[i].astype(BF16), 'w_out': w_out[i].astype(BF16),
        'w_up': w_mlp_up[i].astype(BF16), 'w_down': w_mlp_down[i].astype(BF16),
        'w_gate': w_ple_gate[i].astype(BF16), 'w_proj': w_ple_proj[i].astype(BF16),
        'g_pre_mix': row(norm_pre_mix), 'g_post_mix': row(norm_post_mix), 'g_pre_mlp': row(norm_pre_mlp),
        'g_post_mlp': row(norm_post_mlp), 'g_ple': row(norm_ple),
    }


def _key_tiles(rows, b, t):
    r = rows.reshape(b, t // QB, QB, 2 * KV_W)
    return r[..., :KV_W].astype(BF16), jnp.swapaxes(r[..., KV_W:], 2, 3).astype(BF16)


def _layer_prompt(x, p, lw, b, t):
    n = b * t
    z = _norm_matmul(x, lw['g_pre_mix'], lw['w_in'], tm=min(1024, n), tn=512)
    kv_c = z[:, C_KVC:C_KVC + 2 * KV_W]
    kv_s = z[:, C_KVS:C_KVS + 2 * KV_W]
    kv_w = z[:, C_KVW:C_KVW + 2 * KV_W]
    kc, vc = _compress_prompt(kv_c.reshape(b, t // CMP_STRIDE, CMP_STRIDE * 2 * KV_W),
                              lw['cmp_w1'], lw['cmp_w2'], lw['cmp_pos'])
    vct = jnp.swapaxes(vc, 1, 2)
    ks, vst = _key_tiles(kv_s, b, t)
    kw, vwt = _key_tiles(kv_w, b, t)
    o_nsa = _nsa_prompt(z, kc, vct, ks, vst, kw, vwt, b, t)
    o_hg, st = _hgrn_prompt(z, lw['hg_lb'], lw['hg_norm'], b, t, tc=min(256, t))
    h1 = _merge(o_nsa, o_hg, z, x, lw['w_bn'], lw['w_bh'], lw['w_out'], lw['g_post_mix'], tm=min(512, n))
    h2 = _mlp(h1, p, lw['w_up'], lw['w_down'], lw['w_gate'], lw['w_proj'],
              lw['g_pre_mlp'], lw['g_post_mlp'], lw['g_ple'], tm=min(512, n))
    kv6 = lambda a: a.reshape(b, t, 2, KVH, HD)
    wb = min(WINDOW, t)
    return h2, kv6(kv_c), kv6(kv_s), kv6(kv_w)[:, t - wb:], st


def _page_copies(pt_ref, b, cache_ref, buf_ref, sem, rows_per_page):
    n_pages = pt_ref.shape[1]

    def copy(p):
        return pltpu.make_async_copy(cache_ref.at[pt_ref[b, p]],
                                     buf_ref.at[pl.ds(p * rows_per_page, rows_per_page)], sem)

    def start():
        lax.fori_loop(0, n_pages, lambda p, c: (copy(p).start(), c)[1], 0)

    def wait():
        lax.fori_loop(0, n_pages, lambda p, c: (copy(p).wait(), c)[1], 0)

    return start, wait


def _compress_sample_kernel(pt_ref, cache_ref, w1_ref, w2_ref, pos_ref, kc_ref, vc_ref, buf_ref, sem):
    b = pl.program_id(0)
    n_seg = buf_ref.shape[0]
    start, wait = _page_copies(pt_ref, b, cache_ref, buf_ref, sem, n_seg // pt_ref.shape[1])
    start()
    wait()
    load = lambda c: buf_ref[:, c * LANE:(c + 1) * LANE]
    kc_ref[...] = _compress_math(load, n_seg, w1_ref, w2_ref, pos_ref, 0).astype(BF16)
    vc_ref[...] = _compress_math(load, n_seg, w1_ref, w2_ref, pos_ref, 1).astype(BF16)


def _compress_sample(page_table, cache_segs, w1, w2, pos):
    bs, n_pages = page_table.shape
    segs_per_page, width = cache_segs.shape[1:]
    n_seg = n_pages * segs_per_page
    const = lambda shape: pl.BlockSpec(shape, lambda i, pt: (0,) * len(shape))
    out = pl.BlockSpec((None, n_seg, KV_W), lambda i, pt: (i, 0, 0))
    return pl.pallas_call(
        _compress_sample_kernel,
        grid_spec=pltpu.PrefetchScalarGridSpec(
            num_scalar_prefetch=1, grid=(bs,),
            in_specs=[pl.BlockSpec(memory_space=pl.ANY), const(w1.shape), const(w2.shape), const(pos.shape)],
            out_specs=[out, out],
            scratch_shapes=[pltpu.VMEM((n_seg, width), F32), pltpu.SemaphoreType.DMA(())]),
        out_shape=[jax.ShapeDtypeStruct((bs, n_seg, KV_W), BF16)] * 2,
        compiler_params=_cparams(("arbitrary",)),
        name="compress_sample",
    )(page_table, cache_segs, w1, w2, pos)


_NT = (((1,), (1,)), ((), ()))


def _nsa_sample_kernel(pt_ref, qbd_ref, gl_ref, kc_ref, vc_ref, snew_ref, cwin_ref, wnew_ref, csel_ref,
                       slope_ref, tq_ref, tq8_ref, cover_ref, gsum_ref, gexp_ref, o_ref,
                       buf_ref, s_ref, sem):
    b = pl.program_id(0)
    past = buf_ref.shape[0]
    page = past // pt_ref.shape[1]
    n_seg = kc_ref.shape[0]
    npad = cover_ref.shape[1]
    n_sel = past // SEL_BLOCK + 1
    nr = qbd_ref.shape[0]
    wb = cwin_ref.shape[0]
    start, wait = _page_copies(pt_ref, b, csel_ref, buf_ref, sem, page)
    start()

    qb = (qbd_ref[...] * SCALE).astype(BF16)
    slope = slope_ref[...]
    qpos = tq_ref[...] + float(past)
    zeros_pad = jnp.zeros((QB - snew_ref.shape[0], KV_W), F32)
    t_new = lax.broadcasted_iota(jnp.int32, (1, QB), 1).astype(F32)

    sc = lax.dot_general(qb, kc_ref[...], _NT, preferred_element_type=F32)
    n_i = lax.broadcasted_iota(jnp.int32, (1, n_seg), 1)
    dcmp = qpos - (n_i * CMP_STRIDE + (CMP_BLOCK - 1)).astype(F32)
    okc = dcmp >= 0.0
    sc = jnp.where(okc, sc - slope * dcmp, NEG)
    mc = jnp.max(sc, axis=-1, keepdims=True)
    pc = jnp.where(okc, jnp.exp(sc - mc), 0.0)
    lc = jnp.sum(pc, axis=-1, keepdims=True)
    pc = pc * (1.0 / jnp.where(lc > 0.0, lc, 1.0))
    o_cmp = jnp.dot(pc.astype(BF16), vc_ref[...], preferred_element_type=F32)

    psum = jnp.dot(gsum_ref[...], pc, precision=HI, preferred_element_type=F32)
    imp = jnp.dot(psum, cover_ref[...], precision=HI, preferred_element_type=F32)
    nq8 = gsum_ref.shape[0]
    jblk = lax.broadcasted_iota(jnp.int32, (nq8, npad), 1)
    qblk = (tq8_ref[...].astype(jnp.int32) + past) // SEL_BLOCK
    back = qblk - jblk
    visible = back >= 0
    forced = (jblk == 0) | (visible & (back < SEL_LOCAL))
    score = jnp.where(forced, 1e9, jnp.where(visible, imp, -1e9))
    rank = jnp.zeros((nq8, npad), F32)
    for jp in range(n_sel):
        col = score[:, jp:jp + 1]
        ge = jnp.where(col >= score, 1.0, 0.0)
        gt = jnp.where(col > score, 1.0, 0.0)
        rank = rank + jnp.where(jblk > jp, ge, gt)
    sel8 = jnp.where(visible, jnp.where(rank < float(min(SEL_TOP, n_sel)), 1.0, 0.0), 0.0)
    negb = (jnp.dot(gexp_ref[...], sel8, precision=HI, preferred_element_type=F32) - 1.0) * (-NEG)

    def tile_bias(jt):
        lane = lax.broadcasted_iota(jnp.int32, (nr, QB), 1)
        return jnp.where(lane < SEL_BLOCK, negb[:, 2 * jt:2 * jt + 1], negb[:, 2 * jt + 1:2 * jt + 2])

    wait()
    ck = 4 * QB
    for c in range(past // ck):
        kch = buf_ref[c * ck:(c + 1) * ck, 0:KV_W].astype(BF16)
        s = lax.dot_general(qb, kch, _NT, preferred_element_type=F32)
        kpos = (lax.broadcasted_iota(jnp.int32, (1, ck), 1) + c * ck).astype(F32)
        bias = jnp.concatenate([tile_bias(c * (ck // QB) + u) for u in range(ck // QB)], axis=1)
        s_ref[:, c * ck:(c + 1) * ck] = s - slope * (qpos - kpos) + bias
    knew = jnp.concatenate([snew_ref[:, 0:KV_W], zeros_pad], axis=0).astype(BF16)
    vnew = jnp.concatenate([snew_ref[:, KV_W:2 * KV_W], zeros_pad], axis=0).astype(BF16)
    dnew = tq_ref[...] - t_new
    s = lax.dot_general(qb, knew, _NT, preferred_element_type=F32)
    s_ref[:, past:past + QB] = jnp.where(dnew >= 0.0, s - slope * dnew + tile_bias(past // QB), NEG)
    s_all = s_ref[...]
    ms = jnp.max(s_all, axis=-1, keepdims=True)
    ps = jnp.exp(s_all - ms)
    ls = jnp.sum(ps, axis=-1, keepdims=True)
    psb = ps.astype(BF16)
    o_sel = jnp.dot(psb[:, past:past + QB], vnew, preferred_element_type=F32)
    for c in range(past // ck):
        o_sel = o_sel + jnp.dot(psb[:, c * ck:(c + 1) * ck], buf_ref[c * ck:(c + 1) * ck, KV_W:2 * KV_W].astype(BF16),
                                preferred_element_type=F32)
    o_sel = o_sel * (1.0 / ls)

    s1 = lax.dot_general(qb, cwin_ref[:, 0:KV_W].astype(BF16), _NT, preferred_element_type=F32)
    d1 = float(wb) + tq_ref[...] - lax.broadcasted_iota(jnp.int32, (1, wb), 1).astype(F32)
    ok1 = d1 < float(WINDOW)
    s1 = jnp.where(ok1, s1 - slope * d1, NEG)
    wk = jnp.concatenate([wnew_ref[:, 0:KV_W], zeros_pad], axis=0).astype(BF16)
    wv = jnp.concatenate([wnew_ref[:, KV_W:2 * KV_W], zeros_pad], axis=0).astype(BF16)
    ok2 = dnew >= 0.0
    s2 = jnp.where(ok2, lax.dot_general(qb, wk, _NT, preferred_element_type=F32) - slope * dnew, NEG)
    mw = jnp.maximum(jnp.max(s1, axis=-1, keepdims=True), jnp.max(s2, axis=-1, keepdims=True))
    p1 = jnp.where(ok1, jnp.exp(s1 - mw), 0.0)
    p2 = jnp.where(ok2, jnp.exp(s2 - mw), 0.0)
    lw_ = jnp.sum(p1, axis=-1, keepdims=True) + jnp.sum(p2, axis=-1, keepdims=True)
    o_win = (jnp.dot(p1.astype(BF16), cwin_ref[:, KV_W:2 * KV_W].astype(BF16), preferred_element_type=F32)
             + jnp.dot(p2.astype(BF16), wv, preferred_element_type=F32)) * (1.0 / lw_)

    sig = _sigmoid(gl_ref[...])
    o_ref[...] = sig[:, 0:1] * o_cmp + sig[:, 1:2] * o_sel + sig[:, 2:3] * o_win


def _nsa_sample(page_table, qbd, gl, kc, vc, snew, cwin, wnew, csel, ts):
    bs, n_pages = page_table.shape
    page = csel.shape[1]
    past = n_pages * page
    n_seg = kc.shape[1]
    n_sel = past // SEL_BLOCK + 1
    npad = -(-(n_sel + 1) // LANE) * LANE
    nr = KVH * GROUP * ts
    wb = cwin.shape[1]
    r = np.arange(nr)
    slope = np.exp2(-8.0 * ((r // ts) + 1) / NSA_HEADS).astype(np.float32)[:, None]
    tq = (r % ts).astype(np.float32)[:, None]
    r8 = np.arange(KVH * ts)
    tq8 = (r8 % ts).astype(np.float32)[:, None]
    cover = np.zeros((n_seg, npad), np.float32)
    cover[:, :n_sel] = _cover_matrix(n_seg, n_sel)
    gsum = ((r[None, :] // (GROUP * ts) == r8[:, None] // ts) & (r[None, :] % ts == r8[:, None] % ts)).astype(np.float32)
    consts = [slope, tq, tq8, cover, gsum, gsum.T.copy()]
    const = lambda shape: pl.BlockSpec(shape, lambda i, pt: (0,) * len(shape))
    per = lambda *s: pl.BlockSpec((None,) + s, lambda i, pt: (i,) + (0,) * len(s))
    return pl.pallas_call(
        _nsa_sample_kernel,
        grid_spec=pltpu.PrefetchScalarGridSpec(
            num_scalar_prefetch=1, grid=(bs,),
            in_specs=[per(nr, KV_W), per(nr, LANE), per(n_seg, KV_W), per(n_seg, KV_W), per(SUBLANE, 2 * KV_W),
                      per(wb, 2 * KV_W), per(SUBLANE, 2 * KV_W), pl.BlockSpec(memory_space=pl.ANY)]
                     + [const(c.shape) for c in consts],
            out_specs=per(nr, KV_W),
            scratch_shapes=[pltpu.VMEM((past, 2 * KV_W), F32), pltpu.VMEM((nr, past + QB), F32),
                            pltpu.SemaphoreType.DMA(())]),
        out_shape=jax.ShapeDtypeStruct((bs, nr, KV_W), F32),
        compiler_params=_cparams(("arbitrary",)),
        name="nsa_sample",
    )(page_table, qbd, gl, kc, vc, snew, cwin, wnew, csel, *[jnp.asarray(c) for c in consts])


def _hgrn_sample_kernel(ts, hq_ref, hf_ref, hi_ref, hg_ref, lb_ref, ng_ref, s0_ref, o_ref, s1_ref):
    rows = hq_ref.shape[0]
    q = hq_ref[...]
    v = hi_ref[...]
    log_f, k = _hgrn_gates(hf_ref[...], lb_ref[...])
    tloc = lax.broadcasted_iota(jnp.int32, (rows, HG_W), 0) % ts
    up = lambda a, d: pltpu.roll(a, d, axis=0)
    down = lambda a, d: pltpu.roll(a, rows - d, axis=0)

    bcum = log_f
    for d in range(1, ts):
        bcum = bcum + jnp.where(tloc >= d, up(log_f, d), 0.0)
    b_last = bcum
    for d in range(1, ts):
        b_last = jnp.where(tloc == ts - 1 - d, down(bcum, d), b_last)

    o_intra = [jnp.zeros((rows, HG_D), F32) for _ in range(HG_H)]
    for d in range(ts):
        kd, bd, vd = (k, bcum, v) if d == 0 else (up(k, d), up(bcum, d), up(v, d))
        w = jnp.where(tloc >= d, q * kd * jnp.exp(jnp.where(tloc >= d, bcum - bd, 0.0)), 0.0)
        for h in range(HG_H):
            sl = slice(h * HG_D, (h + 1) * HG_D)
            o_intra[h] = o_intra[h] + jnp.sum(w[:, sl], axis=-1, keepdims=True) * vd[:, sl]

    qb = q * jnp.exp(bcum)
    kh = k * jnp.exp(b_last - bcum)
    per_tile = SUBLANE // ts
    row8 = lax.broadcasted_iota(jnp.int32, (SUBLANE, HG_D), 0) // ts
    tiles = []
    for j in range(rows // SUBLANE):
        r8 = slice(j * SUBLANE, (j + 1) * SUBLANE)
        heads = []
        for h in range(HG_H):
            sl = slice(h * HG_D, (h + 1) * HG_D)
            o_inter = jnp.zeros((SUBLANE, HG_D), F32)
            for u in range(per_tile):
                seq = j * per_tile + u
                mine = row8 == u
                s0 = s0_ref[seq, h]
                o_inter = o_inter + jnp.dot(jnp.where(mine, qb[r8, sl], 0.0).astype(BF16), s0.astype(BF16),
                                            preferred_element_type=F32)
                upd = lax.dot_general(jnp.where(mine, kh[r8, sl], 0.0).astype(BF16), v[r8, sl].astype(BF16),
                                      (((0,), (0,)), ((), ())), preferred_element_type=F32)
                r_last = j * SUBLANE + u * ts + ts - 1
                dec = jnp.exp(bcum[r_last:r_last + 1, sl])
                s1_ref[seq, h] = jnp.broadcast_to(dec, (HG_D, HG_D)).T * s0 + upd
            heads.append(o_inter + o_intra[h][r8, :])
        tiles.append(jnp.concatenate(heads, axis=1))
    o = jnp.concatenate(tiles, axis=0)
    o_ref[...] = _hgrn_out(o, hg_ref[...], ng_ref[...]).astype(BF16)


def _hgrn_sample(z, lb, ng, s0, bs, ts, nb):
    rows = nb * ts
    col = lambda c0: pl.BlockSpec((rows, HG_W), lambda i: (i, c0 // HG_W))
    st = pl.BlockSpec((nb, HG_H, HG_D, HG_D), lambda i: (i, 0, 0, 0))
    return pl.pallas_call(
        functools.partial(_hgrn_sample_kernel, ts),
        grid=(bs // nb,),
        in_specs=[col(C_HQ), col(C_HF), col(C_HI), col(C_HG), _const_spec((1, HG_W)), _const_spec((1, HG_D)), st],
        out_specs=[pl.BlockSpec((rows, HG_W), lambda i: (i, 0)), st],
        out_shape=[jax.ShapeDtypeStruct((bs * ts, HG_W), BF16),
                   jax.ShapeDtypeStruct((bs, HG_H, HG_D, HG_D), F32)],
        compiler_params=_cparams(("parallel",)),
        name="hgrn_sample",
    )(z, z, z, z, lb, ng, s0)


def _layer_sample(x, p, cache_cmp, cache_sel, cache_win, state, page_table, lw, bs, ts):
    n = bs * ts
    assert SUBLANE % ts == 0 and n % SUBLANE == 0
    n_phys, page = cache_cmp.shape[:2]
    z = _norm_matmul(x, lw['g_pre_mix'], lw['w_in'], tm=min(512, n), tn=512)
    kv_c = z[:, C_KVC:C_KVC + 2 * KV_W]
    kv_s = z[:, C_KVS:C_KVS + 2 * KV_W]
    kv_w = z[:, C_KVW:C_KVW + 2 * KV_W]

    kc, vc = _compress_sample(page_table, cache_cmp.reshape(n_phys, page // CMP_STRIDE, CMP_STRIDE * 2 * KV_W),
                              lw['cmp_w1'], lw['cmp_w2'], lw['cmp_pos'])

    eye = jnp.eye(KVH, dtype=F32)
    q5 = z[:, C_Q:C_Q + NSA_W].reshape(bs, ts, KVH, GROUP, HD).transpose(0, 2, 3, 1, 4)
    qbd = jnp.einsum('bkgtd,kq->bkgtqd', q5, eye).reshape(bs, KVH * GROUP * ts, KV_W)
    g5 = z[:, C_GN:C_GN + 3 * NSA_HEADS].reshape(bs, ts, KVH, GROUP, 3).transpose(0, 2, 3, 1, 4)
    gl = jnp.pad(g5.reshape(bs, KVH * GROUP * ts, 3), ((0, 0), (0, 0), (0, LANE - 3)))
    pad_rows = lambda a: jnp.pad(a.reshape(bs, ts, 2 * KV_W), ((0, 0), (0, SUBLANE - ts), (0, 0)))
    cwin = cache_win.reshape(bs, -1, 2 * KV_W)
    o_rows = _nsa_sample(page_table, qbd, gl, kc, vc, pad_rows(kv_s), cwin, pad_rows(kv_w),
                         cache_sel.reshape(n_phys, page, 2 * KV_W), ts)
    o6 = o_rows.reshape(bs, KVH, GROUP, ts, KVH, HD)
    o_nsa = jnp.stack([o6[:, kvh, :, :, kvh, :] for kvh in range(KVH)], axis=1)
    o_nsa = o_nsa.transpose(0, 3, 1, 2, 4).reshape(n, NSA_W).astype(BF16)

    o_hg, st = _hgrn_sample(z, lw['hg_lb'], lw['hg_norm'], state, bs, ts, nb=min(8, bs))
    h1 = _merge(o_nsa, o_hg, z, x, lw['w_bn'], lw['w_bh'], lw['w_out'], lw['g_post_mix'], tm=min(512, n))
    h2 = _mlp(h1, p, lw['w_up'], lw['w_down'], lw['w_gate'], lw['w_proj'],
              lw['g_pre_mlp'], lw['g_post_mlp'], lw['g_ple'], tm=min(512, n))
    kv6 = lambda a: a.reshape(bs, ts, 2, KVH, HD)
    win_buf = jnp.concatenate([cache_win, kv6(kv_w)], axis=1)[:, ts:]
    return h2, kv6(kv_c), kv6(kv_s), win_buf, st


def kernel(x_prompt, x_sample, cache_cmp_kv, cache_sel_kv, cache_win_kv, state_hgrn, page_table, p_prompt,
           p_sample, w_in, cmp_k_w1, cmp_k_w2, cmp_v_w1, cmp_v_w2, cmp_pos, hg_lb_logits, hg_norm, w_branch_nsa,
           w_branch_hgrn, w_out, norm_pre_mix, norm_post_mix, norm_pre_mlp, norm_post_mlp, w_mlp_up, w_mlp_down,
           norm_ple, w_ple_gate, w_ple_proj):
    depth = w_in.shape[0]
    b, t, d = x_prompt.shape
    bs, ts, _ = x_sample.shape
    lb_all = jnp.cumsum(jax.nn.softmax(hg_lb_logits.astype(F32), axis=0), axis=0)
    h_p = x_prompt.reshape(b * t, d)
    h_s = x_sample.reshape(bs * ts, d)
    outs = [[] for _ in range(8)]
    for i in range(depth):
        lw = _prep_layer(i, lb_all, w_in, cmp_k_w1, cmp_k_w2, cmp_v_w1, cmp_v_w2, cmp_pos, hg_norm, w_branch_nsa,
                         w_branch_hgrn, w_out, norm_pre_mix, norm_post_mix, norm_pre_mlp, norm_post_mlp, w_mlp_up,
                         w_mlp_down, norm_ple, w_ple_gate, w_ple_proj)
        h_p, *res_p = _layer_prompt(h_p, p_prompt[i].reshape(b * t, -1), lw, b, t)
        h_s, *res_s = _layer_sample(h_s, p_sample[i].reshape(bs * ts, -1), cache_cmp_kv[i], cache_sel_kv[i],
                                    cache_win_kv[i], state_hgrn[i], page_table, lw, bs, ts)
        for lst, v in zip(outs, res_p + res_s):
            lst.append(v)
    return (h_p.reshape(b, t, d), h_s.reshape(bs, ts, d)) + tuple(jnp.stack(lst, axis=0) for lst in outs)
```

```python
import functools
import math

import numpy as np
import jax
import jax.numpy as jnp
from jax import lax
from jax.experimental import pallas as pl
from jax.experimental.pallas import tpu as pltpu

F32 = jnp.float32
BF16 = jnp.bfloat16

D_MODEL = 1024
NSA_HEADS = 8
KVH = 2
GROUP = NSA_HEADS // KVH
HD = 64
NSA_W = NSA_HEADS * HD
KV_W = KVH * HD
CMP_BLOCK = 32
CMP_STRIDE = 16
CMP_HIDDEN = 2 * HD
SEL_BLOCK = 64
SEL_TOP = 16
SEL_LOCAL = 2
WINDOW = 512
HG_W = 512
HG_H = 4
HG_D = 128
HG_CHUNK = 64
D_FF = 4 * D_MODEL
PLE_DIM = 256
RMS_EPS = 1e-6
NEG = -1e30
M_INIT = -1e20
SCALE = HD ** -0.5
HG_SAFE_EXP = 60.0

LANE = 128
SUBLANE = 8
VMEM_LIMIT = 48 * 1024 * 1024

C_GM = 0
C_Q = 2048
C_HQ = 2560
C_HF = 3072
C_HI = 3584
C_HG = 4096
C_KVC = 4608
C_KVS = 4864
C_KVW = 5120
C_GN = 5376
Z_COLS = 5632
QB = 128
HI = lax.Precision.HIGHEST


def _cparams(sem, vmem=VMEM_LIMIT):
    return pltpu.CompilerParams(dimension_semantics=sem, vmem_limit_bytes=vmem)


def _rms(x, g):
    return x * lax.rsqrt(jnp.mean(x * x, axis=-1, keepdims=True) + RMS_EPS) * g


def _sigmoid(x):
    return 1.0 / (1.0 + jnp.exp(-x))


def _gelu_tanh(x):
    return 0.5 * x * (1.0 + jnp.tanh(math.sqrt(2.0 / math.pi) * (x + 0.044715 * (x * x * x))))


def _const_spec(shape):
    nd = len(shape)
    return pl.BlockSpec(shape, lambda *_: (0,) * nd)


def _norm_matmul_kernel(x_ref, g_ref, w_ref, o_ref, xn_ref):
    @pl.when(pl.program_id(1) == 0)
    def _():
        xn_ref[...] = _rms(x_ref[...], g_ref[...]).astype(BF16)

    o_ref[...] = jnp.dot(xn_ref[...], w_ref[...], preferred_element_type=F32)


def _norm_matmul(x, g, w, tm, tn):
    n, d = x.shape
    c = w.shape[1]
    return pl.pallas_call(
        _norm_matmul_kernel,
        grid=(n // tm, c // tn),
        in_specs=[pl.BlockSpec((tm, d), lambda i, j: (i, 0)),
                  pl.BlockSpec((1, d), lambda i, j: (0, 0)),
                  pl.BlockSpec((d, tn), lambda i, j: (0, j))],
        out_specs=pl.BlockSpec((tm, tn), lambda i, j: (i, j)),
        out_shape=jax.ShapeDtypeStruct((n, c), F32),
        scratch_shapes=[pltpu.VMEM((tm, d), BF16)],
        compiler_params=_cparams(("parallel", "arbitrary")),
        name="norm_in_proj",
    )(x, g, w)


def _compress_math(load_rows, n_seg, w1_ref, w2_ref, pos_ref, kv):
    x = jnp.concatenate([load_rows(s) for s in range(CMP_STRIDE)], axis=1).astype(BF16)
    h0 = jnp.dot(x, w1_ref[kv, 0], preferred_element_type=F32)
    h1 = jnp.dot(x, w1_ref[kv, 1], preferred_element_type=F32)
    posb = (jnp.dot(pos_ref[0], w1_ref[kv, 0], preferred_element_type=F32)
            + jnp.dot(pos_ref[1], w1_ref[kv, 1], preferred_element_type=F32))
    hid = h0 + pltpu.roll(h1, n_seg - 1, axis=0) + posb[0:1]
    return jnp.dot(_gelu_tanh(hid).astype(BF16), w2_ref[kv], preferred_element_type=F32)


def _compress_prompt_kernel(xk_ref, xv_ref, w1_ref, w2_ref, pos_ref, kc_ref, vc_ref):
    n_seg = xk_ref.shape[0] // CMP_STRIDE
    for kv, x_ref, out_ref in ((0, xk_ref, kc_ref), (1, xv_ref, vc_ref)):
        load = lambda s, x_ref=x_ref: x_ref[pl.ds(s, n_seg, stride=CMP_STRIDE), :]
        out_ref[...] = _compress_math(load, n_seg, w1_ref, w2_ref, pos_ref, kv).astype(BF16)


def _compress_prompt(z, w1, w2, pos, b, t):
    n_seg = t // CMP_STRIDE
    return pl.pallas_call(
        _compress_prompt_kernel,
        grid=(b,),
        in_specs=[pl.BlockSpec((t, KV_W), lambda i: (i, C_KVC // KV_W)),
                  pl.BlockSpec((t, KV_W), lambda i: (i, C_KVC // KV_W + 1)),
                  _const_spec(w1.shape), _const_spec(w2.shape), _const_spec(pos.shape)],
        out_specs=[pl.BlockSpec((None, n_seg, KV_W), lambda i: (i, 0, 0)),
                   pl.BlockSpec((None, n_seg, KV_W), lambda i: (i, 0, 0))],
        out_shape=[jax.ShapeDtypeStruct((b, n_seg, KV_W), BF16),
                   jax.ShapeDtypeStruct((b, n_seg, KV_W), BF16)],
        compiler_params=_cparams(("parallel",)),
        name="compress_prompt",
    )(z, z, w1, w2, pos)


def _nsa_prompt_kernel(q_ref, gn_ref, kc_ref, vct_ref, ks_ref, vst_ref, kw_ref, vwt_ref,
                       bias0_ref, slopes_ref, covert_ref, o_ref,
                       m_ref, l_ref, acc_ref, sel_ref):
    i = pl.program_id(1)
    t0 = i * QB
    n_seg = kc_ref.shape[0]
    n_sel = covert_ref.shape[0]
    nl = NSA_HEADS * QB
    half = GROUP * QB
    slopes = slopes_ref[...]

    q = q_ref[...] * SCALE
    zero = jnp.zeros((HD, QB), F32)
    cols = []
    for kvh in range(KVH):
        for p in range(GROUP // 2):
            c0 = (kvh * (GROUP // 2) + p) * LANE
            blk = q[:, c0:c0 + LANE].T
            for hh in range(2):
                piece = blk[hh * HD:(hh + 1) * HD, :]
                cols.append(jnp.concatenate([piece, zero] if kvh == 0 else [zero, piece], axis=0))
    qbdt = jnp.concatenate(cols, axis=1).astype(BF16)

    lane = lax.broadcasted_iota(jnp.int32, (QB, QB), 1)
    sub = lax.broadcasted_iota(jnp.int32, (QB, QB), 0)
    dl = (lane - sub).astype(F32)

    sc = jnp.dot(kc_ref[...], qbdt, preferred_element_type=F32)
    n_i = lax.broadcasted_iota(jnp.int32, (n_seg, QB), 0)
    t_i = lax.broadcasted_iota(jnp.int32, (n_seg, QB), 1)
    d1 = (t0 + t_i - (n_i * CMP_STRIDE + (CMP_BLOCK - 1))).astype(F32)
    dc = jnp.concatenate([d1] * NSA_HEADS, axis=1)
    okc = dc >= 0.0
    sc = jnp.where(okc, sc - slopes * dc, NEG)
    mc = jnp.max(sc, axis=0, keepdims=True)
    pc = jnp.where(okc, jnp.exp(sc - mc), 0.0)
    lc = jnp.sum(pc, axis=0, keepdims=True)
    pc = pc * (1.0 / jnp.where(lc > 0.0, lc, 1.0))
    pcb = pc.astype(BF16)
    ocmp = [jnp.dot(vct_ref[kvh * HD:(kvh + 1) * HD, :], pcb[:, kvh * half:(kvh + 1) * half],
                    preferred_element_type=F32) for kvh in range(KVH)]

    jblk = lax.broadcasted_iota(jnp.int32, (n_sel, QB), 0)
    qblk = (t0 + lax.broadcasted_iota(jnp.int32, (n_sel, QB), 1)) // SEL_BLOCK
    back = qblk - jblk
    visible = back >= 0
    forced = (jblk == 0) | (visible & (back < SEL_LOCAL))
    for kvh in range(KVH):
        psum = pc[:, kvh * half:kvh * half + QB]
        for g in range(1, GROUP):
            psum = psum + pc[:, kvh * half + g * QB:kvh * half + (g + 1) * QB]
        imp = jnp.dot(covert_ref[...], psum, precision=HI, preferred_element_type=F32)
        score = jnp.where(forced, 1e9, jnp.where(visible, imp, -1e9))
        rank = jnp.zeros((n_sel, QB), F32)
        for jp in range(n_sel):
            row = score[jp:jp + 1, :]
            ge = jnp.where(row >= score, 1.0, 0.0)
            gt = jnp.where(row > score, 1.0, 0.0)
            rank = rank + jnp.where(jblk > jp, ge, gt)
        sel_ref[kvh] = jnp.where(visible, jnp.where(rank < float(min(SEL_TOP, n_sel)), 1.0, 0.0), 0.0)

    def sweep(k_ref, vt_ref, lo, hi, tile_bias):
        m_ref[...] = jnp.full((1, nl), M_INIT, F32)
        l_ref[...] = jnp.zeros((1, nl), F32)
        acc_ref[...] = jnp.zeros((KVH, HD, half), F32)

        def body(jt, carry):
            off = (t0 - jt * QB).astype(F32)
            s = jnp.dot(k_ref[jt], qbdt, preferred_element_type=F32)
            s = s - bias0_ref[...] - slopes * off + tile_bias(jt, off)
            m_old = m_ref[...]
            m_new = jnp.maximum(m_old, jnp.max(s, axis=0, keepdims=True))
            alpha = jnp.exp(m_old - m_new)
            p = jnp.exp(s - m_new)
            l_ref[...] = alpha * l_ref[...] + jnp.sum(p, axis=0, keepdims=True)
            m_ref[...] = m_new
            pb = p.astype(BF16)
            vt = vt_ref[jt]
            for kvh in range(KVH):
                pv = jnp.dot(vt[kvh * HD:(kvh + 1) * HD, :], pb[:, kvh * half:(kvh + 1) * half],
                             preferred_element_type=F32)
                acc_ref[kvh] = alpha[:, kvh * half:(kvh + 1) * half] * acc_ref[kvh] + pv
            return carry

        lax.fori_loop(lo, hi, body, 0)
        inv = 1.0 / l_ref[...]
        return [acc_ref[kvh] * inv[:, kvh * half:(kvh + 1) * half] for kvh in range(KVH)]

    def sel_bias(jt, off):
        causal = dl >= -off
        parts = []
        for kvh in range(KVH):
            rows = sel_ref[kvh, pl.ds(jt * (QB // SEL_BLOCK), QB // SEL_BLOCK), :]
            blockmask = jnp.concatenate(
                [jnp.broadcast_to(rows[r:r + 1, :], (SEL_BLOCK, QB)) for r in range(QB // SEL_BLOCK)], axis=0)
            nb = jnp.where(causal, (blockmask - 1.0) * (-NEG), NEG)
            parts += [nb] * GROUP
        return jnp.concatenate(parts, axis=1)

    def win_bias(jt, off):
        nb = jnp.where(dl >= -off, jnp.where(dl < float(WINDOW) - off, 0.0, NEG), NEG)
        return jnp.concatenate([nb] * NSA_HEADS, axis=1)

    osel = sweep(ks_ref, vst_ref, 0, i + 1, sel_bias)
    owin = sweep(kw_ref, vwt_ref, jnp.maximum(i - WINDOW // QB, 0), i + 1, win_bias)

    gt_ = _sigmoid(gn_ref[...]).T
    for kvh in range(KVH):
        for p in range(GROUP // 2):
            pieces = []
            for hh in range(2):
                g = 2 * p + hh
                h = kvh * GROUP + g
                cs = slice(g * QB, (g + 1) * QB)
                pieces.append(gt_[3 * h:3 * h + 1, :] * ocmp[kvh][:, cs]
                              + gt_[3 * h + 1:3 * h + 2, :] * osel[kvh][:, cs]
                              + gt_[3 * h + 2:3 * h + 3, :] * owin[kvh][:, cs])
            c0 = (kvh * (GROUP // 2) + p) * LANE
            o_ref[:, c0:c0 + LANE] = jnp.concatenate(pieces, axis=0).T.astype(BF16)


PB = 256
N_AUG = 16
LOG2E = math.log2(math.e)


def _split3(x):
    hi = x.astype(BF16).astype(F32)
    r = x - hi
    mid = r.astype(BF16).astype(F32)
    return hi, mid, r - mid


def _nsa_prompt256_kernel(q_ref, gn_ref, kc_ref, vct_ref, ks_ref, vst_ref, kw_ref, vwt_ref,
                          augc_ref, slopes_ref, covert_ref, o_ref,
                          qa_ref, m_ref, l_ref, acc_ref, sel_ref, score_ref):
    i = pl.program_id(1)
    t0 = i * PB
    n_seg = kc_ref.shape[0]
    n_sel = covert_ref.shape[0]
    nl = NSA_HEADS * PB
    half = GROUP * PB
    slopes2 = slopes_ref[...] * LOG2E
    tlane = (lax.broadcasted_iota(jnp.int32, (1, nl), 1) % PB).astype(F32)
    blocks_per_tile = PB // SEL_BLOCK

    q = q_ref[...] * (SCALE * LOG2E)
    zero = jnp.zeros((HD, PB), F32)
    cols = []
    for kvh in range(KVH):
        for p in range(GROUP // 2):
            c0 = (kvh * (GROUP // 2) + p) * LANE
            blk = q[:, c0:c0 + LANE].T
            for hh in range(2):
                piece = blk[hh * HD:(hh + 1) * HD, :]
                cols.append(jnp.concatenate([piece, zero] if kvh == 0 else [zero, piece], axis=0))
    qbdt = jnp.concatenate(cols, axis=1).astype(BF16)
    qa_ref[0:KV_W, :] = qbdt
    qa_ref[KV_W + N_AUG:2 * KV_W, :] = jnp.zeros((KV_W - N_AUG, nl), BF16)

    lane = lax.broadcasted_iota(jnp.int32, (PB, PB), 1)
    sub = lax.broadcasted_iota(jnp.int32, (PB, PB), 0)
    causal_bias = jnp.where(lane >= sub, 0.0, NEG)
    edge_bias = jnp.where(lane < sub, 0.0, NEG)

    sc = jnp.dot(kc_ref[...], qbdt, preferred_element_type=F32)
    n_i = lax.broadcasted_iota(jnp.int32, (n_seg, PB), 0)
    t_i = lax.broadcasted_iota(jnp.int32, (n_seg, PB), 1)
    d1 = (t0 + t_i - (n_i * CMP_STRIDE + (CMP_BLOCK - 1))).astype(F32)
    dc = jnp.concatenate([d1] * NSA_HEADS, axis=1)
    okc = dc >= 0.0
    sc = jnp.where(okc, sc - slopes2 * dc, NEG)
    mc = jnp.max(sc, axis=0, keepdims=True)
    pc = jnp.where(okc, jnp.exp2(sc - mc), 0.0)
    lc = jnp.sum(pc, axis=0, keepdims=True)
    pc = pc * (1.0 / jnp.where(lc > 0.0, lc, 1.0))
    pcb = pc.astype(BF16)
    ocmp = [jnp.dot(vct_ref[kvh * HD:(kvh + 1) * HD, :], pcb[:, kvh * half:(kvh + 1) * half],
                    preferred_element_type=F32) for kvh in range(KVH)]

    jblk = lax.broadcasted_iota(jnp.int32, (n_sel, PB), 0)
    qblk = (t0 + lax.broadcasted_iota(jnp.int32, (n_sel, PB), 1)) // SEL_BLOCK
    back = qblk - jblk
    visible = back >= 0
    forced = (jblk == 0) | (visible & (back < SEL_LOCAL))
    n_groups = jnp.minimum((t0 + PB - 1) // SEL_BLOCK // SUBLANE + 1, n_sel // SUBLANE)
    for kvh in range(KVH):
        psum = pc[:, kvh * half:kvh * half + PB]
        for g in range(1, GROUP):
            psum = psum + pc[:, kvh * half + g * PB:kvh * half + (g + 1) * PB]
        imp = jnp.dot(covert_ref[...], psum, precision=HI, preferred_element_type=F32)
        score = jnp.where(forced, 1e9, jnp.where(visible, imp, -1e9))
        score_ref[...] = score

        def rank_group(gi, rank):
            rows8 = score_ref[pl.ds(pl.multiple_of(gi * SUBLANE, SUBLANE), SUBLANE), :]
            for u in range(SUBLANE):
                row = rows8[u:u + 1, :]
                ge = jnp.where(row >= score, 1.0, 0.0)
                gt = jnp.where(row > score, 1.0, 0.0)
                rank = rank + jnp.where(jblk > gi * SUBLANE + u, ge, gt)
            return rank

        rank = lax.fori_loop(0, n_groups, rank_group, jnp.zeros((n_sel, PB), F32))
        sel_ref[kvh] = jnp.where(visible, jnp.where(rank < float(min(SEL_TOP, n_sel)), 0.0, NEG), NEG)

    sl3 = _split3(slopes2)

    def step(k_ref, vt_ref, jt, use_sel, extra):
        off = (t0 - jt * PB).astype(F32)
        c3 = _split3(-slopes2 * (tlane + off))
        rows = list(sl3) + list(c3)
        if use_sel:
            tiles_per_group = SUBLANE // blocks_per_tile
            base = pl.multiple_of((jt // tiles_per_group) * SUBLANE, SUBLANE)
            which = jt % tiles_per_group
            per_kvh = []
            for kvh in range(KVH):
                rows8 = sel_ref[kvh, pl.ds(base, SUBLANE), :]
                mine = rows8[0:blocks_per_tile, :]
                for w in range(1, tiles_per_group):
                    mine = jnp.where(which == w, rows8[w * blocks_per_tile:(w + 1) * blocks_per_tile, :], mine)
                per_kvh.append(mine)
            rows.append(jnp.concatenate([per_kvh[kvh] for kvh in range(KVH) for _ in range(GROUP)], axis=1))
        rows.append(jnp.zeros((N_AUG - sum(r.shape[0] for r in rows), nl), F32))
        qa_ref[KV_W:KV_W + N_AUG, :] = jnp.concatenate(rows, axis=0).astype(BF16)
        ka = jnp.concatenate([k_ref[jt], augc_ref[...]], axis=1)
        s = jnp.dot(ka, qa_ref[...], preferred_element_type=F32)
        if extra is not None:
            s = s + jnp.concatenate([extra] * NSA_HEADS, axis=1)
        m_old = m_ref[...]
        m_new = jnp.maximum(m_old, jnp.max(s, axis=0, keepdims=True))
        alpha = jnp.exp2(m_old - m_new)
        p = jnp.exp2(s - m_new)
        l_ref[...] = alpha * l_ref[...] + jnp.sum(p, axis=0, keepdims=True)
        m_ref[...] = m_new
        pb = p.astype(BF16)
        vt = vt_ref[jt]
        for kvh in range(KVH):
            pv = jnp.dot(vt[kvh * HD:(kvh + 1) * HD, :], pb[:, kvh * half:(kvh + 1) * half],
                         preferred_element_type=F32)
            acc_ref[kvh] = alpha[:, kvh * half:(kvh + 1) * half] * acc_ref[kvh] + pv

    def sweep(k_ref, vt_ref, lo, use_sel, with_edge):
        m_ref[...] = jnp.full((1, nl), M_INIT, F32)
        l_ref[...] = jnp.zeros((1, nl), F32)
        acc_ref[...] = jnp.zeros((KVH, HD, half), F32)
        if with_edge:
            @pl.when(i >= WINDOW // PB)
            def _():
                step(k_ref, vt_ref, i - WINDOW // PB, use_sel, edge_bias)

        def body(jt, carry):
            step(k_ref, vt_ref, jt, use_sel, None)
            return carry

        lax.fori_loop(lo, i, body, 0)
        step(k_ref, vt_ref, i, use_sel, causal_bias)
        inv = 1.0 / l_ref[...]
        return [acc_ref[kvh] * inv[:, kvh * half:(kvh + 1) * half] for kvh in range(KVH)]

    osel = sweep(ks_ref, vst_ref, 0, True, False)
    owin = sweep(kw_ref, vwt_ref, jnp.maximum(i - WINDOW // PB + 1, 0), False, True)

    gt_ = _sigmoid(gn_ref[...]).T
    for kvh in range(KVH):
        for p in range(GROUP // 2):
            pieces = []
            for hh in range(2):
                g = 2 * p + hh
                h = kvh * GROUP + g
                cs = slice(g * PB, (g + 1) * PB)
                pieces.append(gt_[3 * h:3 * h + 1, :] * ocmp[kvh][:, cs]
                              + gt_[3 * h + 1:3 * h + 2, :] * osel[kvh][:, cs]
                              + gt_[3 * h + 2:3 * h + 3, :] * owin[kvh][:, cs])
            c0 = (kvh * (GROUP // 2) + p) * LANE
            o_ref[:, c0:c0 + LANE] = jnp.concatenate(pieces, axis=0).T.astype(BF16)


def _aug_key_columns():
    a = np.zeros((PB, KV_W), np.float32)
    s = np.arange(PB)
    a[:, 0:3] = s[:, None]
    a[:, 3:6] = 1.0
    for r in range(PB // SEL_BLOCK):
        a[:, 6 + r] = (s // SEL_BLOCK == r)
    return a


def _nsa_prompt256(z, kc, vct, ks, vst, kw, vwt, b, t):
    nq = t // PB
    n_seg = t // CMP_STRIDE
    n_sel = t // SEL_BLOCK
    nl = NSA_HEADS * PB
    assert 6 + PB // SEL_BLOCK <= N_AUG and n_sel % SUBLANE == 0 and WINDOW % PB == 0
    slopes = _alibi_slopes_lanes(PB)
    covert = _cover_matrix(n_seg, n_sel).T.copy()
    tile4 = lambda r, c: pl.BlockSpec((None, nq, r, c), lambda bi, i: (bi, 0, 0, 0))
    return pl.pallas_call(
        _nsa_prompt256_kernel,
        grid=(b, nq),
        in_specs=[pl.BlockSpec((PB, NSA_W), lambda bi, i: (bi * nq + i, C_Q // NSA_W)),
                  pl.BlockSpec((PB, LANE), lambda bi, i: (bi * nq + i, C_GN // LANE)),
                  pl.BlockSpec((None, n_seg, KV_W), lambda bi, i: (bi, 0, 0)),
                  pl.BlockSpec((None, KV_W, n_seg), lambda bi, i: (bi, 0, 0)),
                  tile4(PB, KV_W), tile4(KV_W, PB), tile4(PB, KV_W), tile4(KV_W, PB),
                  _const_spec((PB, KV_W)), _const_spec((1, nl)), _const_spec((n_sel, n_seg))],
        out_specs=pl.BlockSpec((PB, NSA_W), lambda bi, i: (bi * nq + i, 0)),
        out_shape=jax.ShapeDtypeStruct((b * t, NSA_W), BF16),
        scratch_shapes=[pltpu.VMEM((2 * KV_W, nl), BF16), pltpu.VMEM((1, nl), F32), pltpu.VMEM((1, nl), F32),
                        pltpu.VMEM((KVH, HD, GROUP * PB), F32), pltpu.VMEM((KVH, n_sel, PB), F32),
                        pltpu.VMEM((n_sel, PB), F32)],
        compiler_params=_cparams(("parallel", "arbitrary")),
        name="nsa_prompt",
    )(z, z, kc, vct, ks, vst, kw, vwt, jnp.asarray(_aug_key_columns(), BF16), jnp.asarray(slopes),
      jnp.asarray(covert))


def _alibi_slopes_lanes(width):
    h = np.arange(1, NSA_HEADS + 1, dtype=np.float32)
    return np.repeat(np.exp2(-8.0 * h / NSA_HEADS), width)[None, :].astype(np.float32)


def _cover_matrix(n_cmp_rows, n_sel_rows):
    n = np.arange(n_cmp_rows)[:, None] * CMP_STRIDE
    j = np.arange(n_sel_rows)[None, :] * SEL_BLOCK
    return ((n < j + SEL_BLOCK) & (n + CMP_BLOCK > j)).astype(np.float32)


def _nsa_prompt(z, kc, vct, ks, vst, kw, vwt, b, t):
    nq = t // QB
    n_seg = t // CMP_STRIDE
    n_sel = t // SEL_BLOCK
    nl = NSA_HEADS * QB
    slopes = _alibi_slopes_lanes(QB)
    dl = (np.arange(QB)[None, :] - np.arange(QB)[:, None]).astype(np.float32)
    bias0 = np.tile(dl, (1, NSA_HEADS)) * slopes
    covert = _cover_matrix(n_seg, n_sel).T.copy()
    tile4 = lambda: pl.BlockSpec((None, nq, QB, QB), lambda bi, i: (bi, 0, 0, 0))
    return pl.pallas_call(
        _nsa_prompt_kernel,
        grid=(b, nq),
        in_specs=[pl.BlockSpec((QB, NSA_W), lambda bi, i: (bi * nq + i, C_Q // NSA_W)),
                  pl.BlockSpec((QB, LANE), lambda bi, i: (bi * nq + i, C_GN // LANE)),
                  pl.BlockSpec((None, n_seg, KV_W), lambda bi, i: (bi, 0, 0)),
                  pl.BlockSpec((None, KV_W, n_seg), lambda bi, i: (bi, 0, 0)),
                  tile4(), tile4(), tile4(), tile4(),
                  _const_spec((QB, nl)), _const_spec((1, nl)), _const_spec((n_sel, n_seg))],
        out_specs=pl.BlockSpec((QB, NSA_W), lambda bi, i: (bi * nq + i, 0)),
        out_shape=jax.ShapeDtypeStruct((b * t, NSA_W), BF16),
        scratch_shapes=[pltpu.VMEM((1, nl), F32), pltpu.VMEM((1, nl), F32),
                        pltpu.VMEM((KVH, HD, GROUP * QB), F32), pltpu.VMEM((KVH, n_sel, QB), F32)],
        compiler_params=_cparams(("parallel", "arbitrary")),
        name="nsa_prompt",
    )(z, z, kc, vct, ks, vst, kw, vwt, jnp.asarray(bias0), jnp.asarray(slopes), jnp.asarray(covert))


def _hgrn_gates(pre, lb):
    log_f = jnp.log(lb + (1.0 - lb) * _sigmoid(pre))
    k = (1.0 - lb) * _sigmoid(-pre)
    return log_f, k


def _hgrn_out(o, gate, ng):
    outs = []
    for h in range(HG_H):
        sl = slice(h * HG_D, (h + 1) * HG_D)
        g = gate[:, sl]
        outs.append(_rms(o[:, sl], ng) * (g * _sigmoid(g)))
    return jnp.concatenate(outs, axis=1)


def _hgrn_prompt_kernel(hq_ref, hf_ref, hi_ref, hg_ref, lb_ref, ng_ref, tri_ref, o_ref, st_ref,
                        s_ref, oraw_ref):
    ci = pl.program_id(1)
    tc = hq_ref.shape[0]
    c = HG_CHUNK

    @pl.when(ci == 0)
    def _():
        s_ref[...] = jnp.zeros(s_ref.shape, F32)

    lb = lb_ref[...]
    tril = (lax.broadcasted_iota(jnp.int32, (c, c), 0) >= lax.broadcasted_iota(jnp.int32, (c, c), 1))
    row8 = lax.broadcasted_iota(jnp.int32, (SUBLANE, HG_D), 0)

    def chunk(cj, carry):
        r0 = pl.multiple_of(cj * c, c)
        q = hq_ref[pl.ds(r0, c), :]
        v = hi_ref[pl.ds(r0, c), :]
        log_f, k = _hgrn_gates(hf_ref[pl.ds(r0, c), :], lb)
        bcum = jnp.dot(tri_ref[...], log_f, precision=HI, preferred_element_type=F32)
        e = bcum - bcum[c // 2 - 1:c // 2, :]
        safe = jnp.max(jnp.abs(e)) < HG_SAFE_EXP

        @pl.when(safe)
        def _():
            b_last = bcum[c - 1:c, :]
            qt = (q * jnp.exp(e)).astype(BF16)
            kt = (k * jnp.exp(-e)).astype(BF16)
            qb = (q * jnp.exp(bcum)).astype(BF16)
            kh = (k * jnp.exp(b_last - bcum)).astype(BF16)
            dec = jnp.exp(b_last)
            vb = v.astype(BF16)
            for h in range(HG_H):
                sl = slice(h * HG_D, (h + 1) * HG_D)
                a = lax.dot_general(qt[:, sl], kt[:, sl], (((1,), (1,)), ((), ())), preferred_element_type=F32)
                a = jnp.where(tril, a, 0.0).astype(BF16)
                st = s_ref[h]
                o = (lax.dot_general(qb[:, sl], st.astype(BF16), (((1,), (1,)), ((), ())),
                                     preferred_element_type=F32)
                     + jnp.dot(a, vb[:, sl], preferred_element_type=F32))
                oraw_ref[pl.ds(r0, c), sl] = o
                s_ref[h] = st * dec[:, sl] + lax.dot_general(vb[:, sl], kh[:, sl], (((0,), (0,)), ((), ())),
                                                             preferred_element_type=F32)

        @pl.when(jnp.logical_not(safe))
        def _():
            for h in range(HG_H):
                sl = slice(h * HG_D, (h + 1) * HG_D)

                def tile(ti, carry2, sl=sl, h=h):
                    r = pl.multiple_of(r0 + ti * SUBLANE, SUBLANE)
                    q8 = hq_ref[pl.ds(r, SUBLANE), sl]
                    v8 = hi_ref[pl.ds(r, SUBLANE), sl]
                    lf8, k8 = _hgrn_gates(hf_ref[pl.ds(r, SUBLANE), sl], lb[:, sl])
                    f8 = jnp.exp(lf8)
                    st = s_ref[h]
                    rows_out = []
                    for u in range(SUBLANE):
                        vu = jnp.where(row8 == 0, jnp.broadcast_to(v8[u:u + 1, :], (SUBLANE, HG_D)), 0.0)
                        ku = jnp.broadcast_to(k8[u:u + 1, :], (SUBLANE, HG_D))
                        qu = jnp.broadcast_to(q8[u:u + 1, :], (SUBLANE, HG_D))
                        st = st * f8[u:u + 1, :] + lax.dot_general(
                            vu, ku, (((0,), (0,)), ((), ())), precision=HI, preferred_element_type=F32)
                        ou = lax.dot_general(qu, st, (((1,), (1,)), ((), ())), precision=HI,
                                             preferred_element_type=F32)
                        rows_out.append(ou[0:1, :])
                    s_ref[h] = st
                    oraw_ref[pl.ds(r, SUBLANE), sl] = jnp.concatenate(rows_out, axis=0)
                    return carry2

                lax.fori_loop(0, c // SUBLANE, tile, 0)

        return carry

    lax.fori_loop(0, tc // c, chunk, 0)
    o_ref[...] = _hgrn_out(oraw_ref[...], hg_ref[...], ng_ref[...]).astype(BF16)

    @pl.when(ci == pl.num_programs(1) - 1)
    def _():
        for h in range(HG_H):
            st_ref[h] = s_ref[h].T


def _hgrn_prompt(z, lb, ng, b, t, tc):
    nc = t // tc
    col = lambda c0: pl.BlockSpec((tc, HG_W), lambda bi, ci: (bi * nc + ci, c0 // HG_W))
    tri = np.tril(np.ones((HG_CHUNK, HG_CHUNK), np.float32))
    return pl.pallas_call(
        _hgrn_prompt_kernel,
        grid=(b, nc),
        in_specs=[col(C_HQ), col(C_HF), col(C_HI), col(C_HG),
                  _const_spec((1, HG_W)), _const_spec((1, HG_D)), _const_spec((HG_CHUNK, HG_CHUNK))],
        out_specs=[pl.BlockSpec((tc, HG_W), lambda bi, ci: (bi * nc + ci, 0)),
                   pl.BlockSpec((None, HG_H, HG_D, HG_D), lambda bi, ci: (bi, 0, 0, 0))],
        out_shape=[jax.ShapeDtypeStruct((b * t, HG_W), BF16),
                   jax.ShapeDtypeStruct((b, HG_H, HG_D, HG_D), F32)],
        scratch_shapes=[pltpu.VMEM((HG_H, HG_D, HG_D), F32), pltpu.VMEM((tc, HG_W), F32)],
        compiler_params=_cparams(("parallel", "arbitrary")),
        name="hgrn_prompt",
    )(z, z, z, z, lb, ng, jnp.asarray(tri))


def _merge_kernel(on_ref, oh_ref, ga_ref, gb_ref, x_ref, wn_ref, wh_ref, wo_ref, g_ref, o_ref):
    y = (_sigmoid(ga_ref[...]) * jnp.dot(on_ref[...], wn_ref[...], preferred_element_type=F32)
         + _sigmoid(gb_ref[...]) * jnp.dot(oh_ref[...], wh_ref[...], preferred_element_type=F32))
    mix = jnp.dot(y.astype(BF16), wo_ref[...], preferred_element_type=F32)
    o_ref[...] = x_ref[...] + _rms(mix, g_ref[...])


def _resident(shape):
    nd = len(shape)
    return pl.BlockSpec(shape, lambda *_: (0,) * nd, pipeline_mode=pl.Buffered(1))


def _merge(o_nsa, o_hg, z, x, wn, wh, wo, g, tm):
    n = x.shape[0]
    row = lambda w, cb: pl.BlockSpec((tm, w), lambda i: (i, cb))
    return pl.pallas_call(
        _merge_kernel,
        grid=(n // tm,),
        in_specs=[row(NSA_W, 0), row(HG_W, 0), row(D_MODEL, C_GM // D_MODEL), row(D_MODEL, C_GM // D_MODEL + 1),
                  row(D_MODEL, 0), _resident(wn.shape), _resident(wh.shape), _resident(wo.shape),
                  _resident((1, D_MODEL))],
        out_specs=row(D_MODEL, 0),
        out_shape=jax.ShapeDtypeStruct((n, D_MODEL), F32),
        compiler_params=_cparams(("parallel",)),
        name="merge_out_proj",
    )(o_nsa, o_hg, z, z, x, wn, wh, wo, g)


def _mlp_kernel(h_ref, p_ref, wu_ref, wd_ref, wg_ref, wp_ref, g1_ref, g2_ref, g3_ref, o_ref):
    h = h_ref[...]
    xn = _rms(h, g1_ref[...]).astype(BF16)
    ffn = jnp.zeros(h.shape, F32)
    step = D_MODEL
    for c0 in range(0, D_FF, step):
        up = jnp.dot(xn, wu_ref[:, c0:c0 + step], preferred_element_type=F32)
        act = jnp.square(jnp.maximum(up, 0.0)).astype(BF16)
        ffn = ffn + jnp.dot(act, wd_ref[c0:c0 + step, :], preferred_element_type=F32)
    h = h + _rms(ffn, g2_ref[...])
    gate = _sigmoid(jnp.dot(_rms(h, g3_ref[...]).astype(BF16), wg_ref[...], preferred_element_type=F32))
    o_ref[...] = h + gate * jnp.dot(p_ref[...].astype(BF16), wp_ref[...], preferred_element_type=F32)


def _mlp(h, p, wu, wd, wg, wp, g1, g2, g3, tm):
    n = h.shape[0]
    row = lambda w: pl.BlockSpec((tm, w), lambda i: (i, 0))
    gain = _resident((1, D_MODEL))
    return pl.pallas_call(
        _mlp_kernel,
        grid=(n // tm,),
        in_specs=[row(D_MODEL), row(PLE_DIM), _resident(wu.shape), _resident(wd.shape), _resident(wg.shape),
                  _resident(wp.shape), gain, gain, gain],
        out_specs=row(D_MODEL),
        out_shape=jax.ShapeDtypeStruct((n, D_MODEL), F32),
        compiler_params=_cparams(("parallel",)),
        name="mlp_ple",
    )(h, p, wu, wd, wg, wp, g1, g2, g3)


def _prep_w_in(w):
    sizes = (NSA_W, 2 * KV_W, 2 * KV_W, 2 * KV_W, 3 * NSA_HEADS, HG_W, HG_W, HG_W, HG_W)
    q, kvc, kvs, kvw, gn, hq, hf, hi, hg, gm = jnp.split(w, [int(v) for v in np.cumsum(sizes)], axis=1)
    pad = jnp.zeros((w.shape[0], Z_COLS - C_GN - 3 * NSA_HEADS), w.dtype)
    return jnp.concatenate([gm, q, hq, hf, hi, hg, kvc, kvs, kvw, gn, pad], axis=1).astype(BF16)


def _prep_compress(w1k, w2k, w1v, w2v, pos):
    eye = jnp.eye(KVH, dtype=F32)

    def big1(w1):
        t = jnp.einsum('rsdh,kq->rskdqh', w1, eye)
        return t.reshape(CMP_BLOCK // CMP_STRIDE, CMP_STRIDE * KV_W, KVH * CMP_HIDDEN)

    def big2(w2):
        return jnp.einsum('hd,kq->khqd', w2, eye).reshape(KVH * CMP_HIDDEN, KV_W)

    w1 = jnp.stack([big1(w1k), big1(w1v)]).astype(BF16)
    w2 = jnp.stack([big2(w2k), big2(w2v)]).astype(BF16)
    posb = jnp.broadcast_to(pos[:, :, None, :], pos.shape[:2] + (KVH, HD)).reshape(pos.shape[0], 1, -1)
    posb = jnp.broadcast_to(posb, (pos.shape[0], SUBLANE, posb.shape[-1])).astype(BF16)
    return w1, w2, posb


def _prep_layer(i, lb_all, w_in, cmp_k_w1, cmp_k_w2, cmp_v_w1, cmp_v_w2, cmp_pos, hg_norm, w_branch_nsa,
                w_branch_hgrn, w_out, norm_pre_mix, norm_post_mix, norm_pre_mlp, norm_post_mlp, w_mlp_up,
                w_mlp_down, norm_ple, w_ple_gate, w_ple_proj):
    w1, w2, posb = _prep_compress(cmp_k_w1[i], cmp_k_w2[i], cmp_v_w1[i], cmp_v_w2[i], cmp_pos[i])
    row = lambda a: a[i].reshape(1, -1).astype(F32)
    return {
        'w_in': _prep_w_in(w_in[i]), 'cmp_w1': w1, 'cmp_w2': w2, 'cmp_pos': posb,
        'hg_lb': lb_all[i].reshape(1, HG_W), 'hg_norm': row(hg_norm),
        'w_bn': w_branch_nsa[i].astype(BF16), 'w_bh': w_branch_hgrn[i].astype(BF16), 'w_out': w_out[i].astype(BF16),
        'w_up': w_mlp_up[i].astype(BF16), 'w_down': w_mlp_down[i].astype(BF16),
        'w_gate': w_ple_gate[i].astype(BF16), 'w_proj': w_ple_proj[i].astype(BF16),
        'g_pre_mix': row(norm_pre_mix), 'g_post_mix': row(norm_post_mix), 'g_pre_mlp': row(norm_pre_mlp),
        'g_post_mlp': row(norm_post_mlp), 'g_ple': row(norm_ple),
    }


def _key_tiles(rows, b, t):
    r = rows.reshape(b, t // PB, PB, 2 * KV_W)
    return r[..., :KV_W].astype(BF16), jnp.swapaxes(r[..., KV_W:], 2, 3).astype(BF16)


def _layer_prompt(x, p, lw, b, t):
    n = b * t
    z = _norm_matmul(x, lw['g_pre_mix'], lw['w_in'], tm=min(1024, n), tn=512)
    kv_c = z[:, C_KVC:C_KVC + 2 * KV_W]
    kv_s = z[:, C_KVS:C_KVS + 2 * KV_W]
    kv_w = z[:, C_KVW:C_KVW + 2 * KV_W]
    kc, vc = _compress_prompt(z, lw['cmp_w1'], lw['cmp_w2'], lw['cmp_pos'], b, t)
    vct = jnp.swapaxes(vc, 1, 2)
    ks, vst = _key_tiles(kv_s, b, t)
    kw, vwt = _key_tiles(kv_w, b, t)
    o_nsa = _nsa_prompt256(z, kc, vct, ks, vst, kw, vwt, b, t)
    o_hg, st = _hgrn_prompt(z, lw['hg_lb'], lw['hg_norm'], b, t, tc=min(256, t))
    h1 = _merge(o_nsa, o_hg, z, x, lw['w_bn'], lw['w_bh'], lw['w_out'], lw['g_post_mix'], tm=min(512, n))
    h2 = _mlp(h1, p, lw['w_up'], lw['w_down'], lw['w_gate'], lw['w_proj'],
              lw['g_pre_mlp'], lw['g_post_mlp'], lw['g_ple'], tm=min(512, n))
    kv6 = lambda a: a.reshape(b, t, 2, KVH, HD)
    wb = min(WINDOW, t)
    return h2, kv6(kv_c), kv6(kv_s), kv6(kv_w)[:, t - wb:], st


def _page_fetch(pt_ref, cache_ref, buf_ref, sem):
    b = pl.program_id(0)
    n_pages = pt_ref.shape[1]
    slot = b % 2

    def copy(seq, sl, p):
        return pltpu.make_async_copy(cache_ref.at[pt_ref[seq, p]], buf_ref.at[sl, p], sem.at[sl])

    def start(seq, sl):
        lax.fori_loop(0, n_pages, lambda p, c: (copy(seq, sl, p).start(), c)[1], 0)

    @pl.when(b == 0)
    def _():
        start(0, 0)

    @pl.when(b + 1 < pl.num_programs(0))
    def _():
        start(b + 1, 1 - slot)

    def wait():
        lax.fori_loop(0, n_pages, lambda p, c: (copy(b, slot, p).wait(), c)[1], 0)

    return slot, wait


def _compress_sample_kernel(pt_ref, cache_ref, w1_ref, w2_ref, pos_ref, kc_ref, vc_ref, buf_ref, rows_ref, sem):
    n_pages, page = pt_ref.shape[1], buf_ref.shape[-1]
    n_seg = n_pages * page // CMP_STRIDE
    slot, wait = _page_fetch(pt_ref, cache_ref, buf_ref, sem)
    wait()
    for kv, out_ref in ((0, kc_ref), (1, vc_ref)):
        def to_rows(p, c, kv=kv):
            rows_ref[pl.ds(pl.multiple_of(p * page, page), page), :] = buf_ref[slot, p, kv].T
            return c

        lax.fori_loop(0, n_pages, to_rows, 0)
        load = lambda s: rows_ref[pl.ds(s, n_seg, stride=CMP_STRIDE), :]
        out_ref[...] = _compress_math(load, n_seg, w1_ref, w2_ref, pos_ref, kv).astype(BF16)


def _compress_sample(page_table, cache_t, w1, w2, pos):
    bs, n_pages = page_table.shape
    page = cache_t.shape[-1]
    n_seg = n_pages * page // CMP_STRIDE
    const = lambda shape: pl.BlockSpec(shape, lambda i, pt: (0,) * len(shape))
    out = pl.BlockSpec((None, n_seg, KV_W), lambda i, pt: (i, 0, 0))
    return pl.pallas_call(
        _compress_sample_kernel,
        grid_spec=pltpu.PrefetchScalarGridSpec(
            num_scalar_prefetch=1, grid=(bs,),
            in_specs=[pl.BlockSpec(memory_space=pl.ANY), const(w1.shape), const(w2.shape), const(pos.shape)],
            out_specs=[out, out],
            scratch_shapes=[pltpu.VMEM((2, n_pages, 2, KV_W, page), F32), pltpu.VMEM((n_pages * page, KV_W), F32),
                            pltpu.SemaphoreType.DMA((2,))]),
        out_shape=[jax.ShapeDtypeStruct((bs, n_seg, KV_W), BF16)] * 2,
        compiler_params=_cparams(("arbitrary",)),
        name="compress_sample",
    )(page_table, cache_t, w1, w2, pos)


_NT = (((1,), (1,)), ((), ()))


def _nsa_sample_kernel(pt_ref, qbd_ref, gl_ref, kc_ref, vc_ref, snew_ref, cwin_ref, wnew_ref, csel_ref,
                       slope_ref, tq_ref, tq8_ref, cover_ref, gsum_ref, gexp_ref, o_ref,
                       buf_ref, s_ref, sem):
    n_pages, page = pt_ref.shape[1], buf_ref.shape[-1]
    past = n_pages * page
    n_seg = kc_ref.shape[0]
    npad = cover_ref.shape[1]
    n_sel = past // SEL_BLOCK + 1
    nr = qbd_ref.shape[0]
    wb = cwin_ref.shape[-1]
    slot, wait = _page_fetch(pt_ref, csel_ref, buf_ref, sem)

    qb = (qbd_ref[...] * SCALE).astype(BF16)
    slope = slope_ref[...]
    qpos = tq_ref[...] + float(past)
    zeros_pad = jnp.zeros((QB - snew_ref.shape[0], KV_W), F32)
    t_new = lax.broadcasted_iota(jnp.int32, (1, QB), 1).astype(F32)

    sc = lax.dot_general(qb, kc_ref[...], _NT, preferred_element_type=F32)
    n_i = lax.broadcasted_iota(jnp.int32, (1, n_seg), 1)
    dcmp = qpos - (n_i * CMP_STRIDE + (CMP_BLOCK - 1)).astype(F32)
    okc = dcmp >= 0.0
    sc = jnp.where(okc, sc - slope * dcmp, NEG)
    mc = jnp.max(sc, axis=-1, keepdims=True)
    pc = jnp.where(okc, jnp.exp(sc - mc), 0.0)
    lc = jnp.sum(pc, axis=-1, keepdims=True)
    pc = pc * (1.0 / jnp.where(lc > 0.0, lc, 1.0))
    o_cmp = jnp.dot(pc.astype(BF16), vc_ref[...], preferred_element_type=F32)

    psum = jnp.dot(gsum_ref[...], pc, precision=HI, preferred_element_type=F32)
    imp = jnp.dot(psum, cover_ref[...], precision=HI, preferred_element_type=F32)
    nq8 = gsum_ref.shape[0]
    jblk = lax.broadcasted_iota(jnp.int32, (nq8, npad), 1)
    qblk = (tq8_ref[...].astype(jnp.int32) + past) // SEL_BLOCK
    back = qblk - jblk
    visible = back >= 0
    forced = (jblk == 0) | (visible & (back < SEL_LOCAL))
    score = jnp.where(forced, 1e9, jnp.where(visible, imp, -1e9))
    rank = jnp.zeros((nq8, npad), F32)
    for jp in range(n_sel):
        col = score[:, jp:jp + 1]
        ge = jnp.where(col >= score, 1.0, 0.0)
        gt = jnp.where(col > score, 1.0, 0.0)
        rank = rank + jnp.where(jblk > jp, ge, gt)
    sel8 = jnp.where(visible, jnp.where(rank < float(min(SEL_TOP, n_sel)), 1.0, 0.0), 0.0)
    negb = (jnp.dot(gexp_ref[...], sel8, precision=HI, preferred_element_type=F32) - 1.0) * (-NEG)

    def tile_bias(jt):
        lane = lax.broadcasted_iota(jnp.int32, (nr, QB), 1)
        return jnp.where(lane < SEL_BLOCK, negb[:, 2 * jt:2 * jt + 1], negb[:, 2 * jt + 1:2 * jt + 2])

    wait()
    ppc = 4
    ck = ppc * page
    tiles_per_page = page // QB

    def chunk_t(c, kv):
        return jnp.concatenate([buf_ref[slot, c * ppc + u, kv] for u in range(ppc)], axis=1).astype(BF16)

    for c in range(n_pages // ppc):
        s = jnp.dot(qb, chunk_t(c, 0), preferred_element_type=F32)
        kpos = (lax.broadcasted_iota(jnp.int32, (1, ck), 1) + c * ck).astype(F32)
        bias = jnp.concatenate([tile_bias(c * ppc * tiles_per_page + u) for u in range(ppc * tiles_per_page)],
                               axis=1)
        s_ref[:, c * ck:(c + 1) * ck] = s - slope * (qpos - kpos) + bias
    knew = jnp.concatenate([snew_ref[:, 0:KV_W], zeros_pad], axis=0).astype(BF16)
    vnew = jnp.concatenate([snew_ref[:, KV_W:2 * KV_W], zeros_pad], axis=0).astype(BF16)
    dnew = tq_ref[...] - t_new
    s = lax.dot_general(qb, knew, _NT, preferred_element_type=F32)
    s_ref[:, past:past + QB] = jnp.where(dnew >= 0.0, s - slope * dnew + tile_bias(past // QB), NEG)
    s_all = s_ref[...]
    ms = jnp.max(s_all, axis=-1, keepdims=True)
    ps = jnp.exp(s_all - ms)
    ls = jnp.sum(ps, axis=-1, keepdims=True)
    psb = ps.astype(BF16)
    o_sel = jnp.dot(psb[:, past:past + QB], vnew, preferred_element_type=F32)
    for c in range(n_pages // ppc):
        o_sel = o_sel + lax.dot_general(psb[:, c * ck:(c + 1) * ck], chunk_t(c, 1), _NT,
                                        preferred_element_type=F32)
    o_sel = o_sel * (1.0 / ls)

    s1 = jnp.dot(qb, cwin_ref[0].astype(BF16), preferred_element_type=F32)
    d1 = float(wb) + tq_ref[...] - lax.broadcasted_iota(jnp.int32, (1, wb), 1).astype(F32)
    ok1 = d1 < float(WINDOW)
    s1 = jnp.where(ok1, s1 - slope * d1, NEG)
    wk = jnp.concatenate([wnew_ref[:, 0:KV_W], zeros_pad], axis=0).astype(BF16)
    wv = jnp.concatenate([wnew_ref[:, KV_W:2 * KV_W], zeros_pad], axis=0).astype(BF16)
    ok2 = dnew >= 0.0
    s2 = jnp.where(ok2, lax.dot_general(qb, wk, _NT, preferred_element_type=F32) - slope * dnew, NEG)
    mw = jnp.maximum(jnp.max(s1, axis=-1, keepdims=True), jnp.max(s2, axis=-1, keepdims=True))
    p1 = jnp.where(ok1, jnp.exp(s1 - mw), 0.0)
    p2 = jnp.where(ok2, jnp.exp(s2 - mw), 0.0)
    lw_ = jnp.sum(p1, axis=-1, keepdims=True) + jnp.sum(p2, axis=-1, keepdims=True)
    o_win = (lax.dot_general(p1.astype(BF16), cwin_ref[1].astype(BF16), _NT, preferred_element_type=F32)
             + jnp.dot(p2.astype(BF16), wv, preferred_element_type=F32)) * (1.0 / lw_)

    sig = _sigmoid(gl_ref[...])
    o_ref[...] = sig[:, 0:1] * o_cmp + sig[:, 1:2] * o_sel + sig[:, 2:3] * o_win


def _nsa_sample(page_table, qbd, gl, kc, vc, snew, cwin, wnew, csel, ts):
    bs, n_pages = page_table.shape
    page = csel.shape[-1]
    past = n_pages * page
    n_seg = kc.shape[1]
    n_sel = past // SEL_BLOCK + 1
    npad = -(-(n_sel + 1) // LANE) * LANE
    nr = KVH * GROUP * ts
    wb = cwin.shape[-1]
    r = np.arange(nr)
    slope = np.exp2(-8.0 * ((r // ts) + 1) / NSA_HEADS).astype(np.float32)[:, None]
    tq = (r % ts).astype(np.float32)[:, None]
    r8 = np.arange(KVH * ts)
    tq8 = (r8 % ts).astype(np.float32)[:, None]
    cover = np.zeros((n_seg, npad), np.float32)
    cover[:, :n_sel] = _cover_matrix(n_seg, n_sel)
    gsum = ((r[None, :] // (GROUP * ts) == r8[:, None] // ts) & (r[None, :] % ts == r8[:, None] % ts)).astype(np.float32)
    consts = [slope, tq, tq8, cover, gsum, gsum.T.copy()]
    const = lambda shape: pl.BlockSpec(shape, lambda i, pt: (0,) * len(shape))
    per = lambda *s: pl.BlockSpec((None,) + s, lambda i, pt: (i,) + (0,) * len(s))
    return pl.pallas_call(
        _nsa_sample_kernel,
        grid_spec=pltpu.PrefetchScalarGridSpec(
            num_scalar_prefetch=1, grid=(bs,),
            in_specs=[per(nr, KV_W), per(nr, LANE), per(n_seg, KV_W), per(n_seg, KV_W), per(SUBLANE, 2 * KV_W),
                      per(2, KV_W, wb), per(SUBLANE, 2 * KV_W), pl.BlockSpec(memory_space=pl.ANY)]
                     + [const(c.shape) for c in consts],
            out_specs=per(nr, KV_W),
            scratch_shapes=[pltpu.VMEM((2, n_pages, 2, KV_W, page), F32), pltpu.VMEM((nr, past + QB), F32),
                            pltpu.SemaphoreType.DMA((2,))]),
        out_shape=jax.ShapeDtypeStruct((bs, nr, KV_W), F32),
        compiler_params=_cparams(("arbitrary",)),
        name="nsa_sample",
    )(page_table, qbd, gl, kc, vc, snew, cwin, wnew, csel, *[jnp.asarray(c) for c in consts])


def _hgrn_sample_kernel(ts, hq_ref, hf_ref, hi_ref, hg_ref, lb_ref, ng_ref, s0_ref, o_ref, s1_ref):
    rows = hq_ref.shape[0]
    q = hq_ref[...]
    v = hi_ref[...]
    log_f, k = _hgrn_gates(hf_ref[...], lb_ref[...])
    tloc = lax.broadcasted_iota(jnp.int32, (rows, HG_W), 0) % ts
    up = lambda a, d: pltpu.roll(a, d, axis=0)
    down = lambda a, d: pltpu.roll(a, rows - d, axis=0)

    bcum = log_f
    for d in range(1, ts):
        bcum = bcum + jnp.where(tloc >= d, up(log_f, d), 0.0)
    b_last = bcum
    for d in range(1, ts):
        b_last = jnp.where(tloc == ts - 1 - d, down(bcum, d), b_last)

    o_intra = [jnp.zeros((rows, HG_D), F32) for _ in range(HG_H)]
    for d in range(ts):
        kd, bd, vd = (k, bcum, v) if d == 0 else (up(k, d), up(bcum, d), up(v, d))
        w = jnp.where(tloc >= d, q * kd * jnp.exp(jnp.where(tloc >= d, bcum - bd, 0.0)), 0.0)
        for h in range(HG_H):
            sl = slice(h * HG_D, (h + 1) * HG_D)
            o_intra[h] = o_intra[h] + jnp.sum(w[:, sl], axis=-1, keepdims=True) * vd[:, sl]

    qb = q * jnp.exp(bcum)
    kh = k * jnp.exp(b_last - bcum)
    per_tile = SUBLANE // ts
    row8 = lax.broadcasted_iota(jnp.int32, (SUBLANE, HG_D), 0) // ts
    tiles = []
    for j in range(rows // SUBLANE):
        r8 = slice(j * SUBLANE, (j + 1) * SUBLANE)
        heads = []
        for h in range(HG_H):
            sl = slice(h * HG_D, (h + 1) * HG_D)
            o_inter = jnp.zeros((SUBLANE, HG_D), F32)
            for u in range(per_tile):
                seq = j * per_tile + u
                mine = row8 == u
                s0 = s0_ref[seq, h]
                o_inter = o_inter + jnp.dot(jnp.where(mine, qb[r8, sl], 0.0).astype(BF16), s0.astype(BF16),
                                            preferred_element_type=F32)
                upd = lax.dot_general(jnp.where(mine, kh[r8, sl], 0.0).astype(BF16), v[r8, sl].astype(BF16),
                                      (((0,), (0,)), ((), ())), preferred_element_type=F32)
                r_last = j * SUBLANE + u * ts + ts - 1
                dec = jnp.exp(bcum[r_last:r_last + 1, sl])
                s1_ref[seq, h] = jnp.broadcast_to(dec, (HG_D, HG_D)).T * s0 + upd
            heads.append(o_inter + o_intra[h][r8, :])
        tiles.append(jnp.concatenate(heads, axis=1))
    o = jnp.concatenate(tiles, axis=0)
    o_ref[...] = _hgrn_out(o, hg_ref[...], ng_ref[...]).astype(BF16)


def _hgrn_sample(z, lb, ng, s0, bs, ts, nb):
    rows = nb * ts
    col = lambda c0: pl.BlockSpec((rows, HG_W), lambda i: (i, c0 // HG_W))
    st = pl.BlockSpec((nb, HG_H, HG_D, HG_D), lambda i: (i, 0, 0, 0))
    return pl.pallas_call(
        functools.partial(_hgrn_sample_kernel, ts),
        grid=(bs // nb,),
        in_specs=[col(C_HQ), col(C_HF), col(C_HI), col(C_HG), _const_spec((1, HG_W)), _const_spec((1, HG_D)), st],
        out_specs=[pl.BlockSpec((rows, HG_W), lambda i: (i, 0)), st],
        out_shape=[jax.ShapeDtypeStruct((bs * ts, HG_W), BF16),
                   jax.ShapeDtypeStruct((bs, HG_H, HG_D, HG_D), F32)],
        compiler_params=_cparams(("parallel",)),
        name="hgrn_sample",
    )(z, z, z, z, lb, ng, s0)


def _layer_sample(x, p, cache_cmp, cache_sel, cache_win, state, page_table, lw, bs, ts):
    n = bs * ts
    assert SUBLANE % ts == 0 and n % SUBLANE == 0
    z = _norm_matmul(x, lw['g_pre_mix'], lw['w_in'], tm=min(512, n), tn=512)
    kv_c = z[:, C_KVC:C_KVC + 2 * KV_W]
    kv_s = z[:, C_KVS:C_KVS + 2 * KV_W]
    kv_w = z[:, C_KVW:C_KVW + 2 * KV_W]

    rows_last = lambda a: jnp.transpose(a, (0, 2, 3, 4, 1)).reshape(a.shape[0], 2, KV_W, a.shape[1])
    kc, vc = _compress_sample(page_table, rows_last(cache_cmp), lw['cmp_w1'], lw['cmp_w2'], lw['cmp_pos'])

    eye = jnp.eye(KVH, dtype=F32)
    q5 = z[:, C_Q:C_Q + NSA_W].reshape(bs, ts, KVH, GROUP, HD).transpose(0, 2, 3, 1, 4)
    qbd = jnp.einsum('bkgtd,kq->bkgtqd', q5, eye).reshape(bs, KVH * GROUP * ts, KV_W)
    g5 = z[:, C_GN:C_GN + 3 * NSA_HEADS].reshape(bs, ts, KVH, GROUP, 3).transpose(0, 2, 3, 1, 4)
    gl = jnp.pad(g5.reshape(bs, KVH * GROUP * ts, 3), ((0, 0), (0, 0), (0, LANE - 3)))
    pad_rows = lambda a: jnp.pad(a.reshape(bs, ts, 2 * KV_W), ((0, 0), (0, SUBLANE - ts), (0, 0)))
    o_rows = _nsa_sample(page_table, qbd, gl, kc, vc, pad_rows(kv_s), rows_last(cache_win), pad_rows(kv_w),
                         rows_last(cache_sel), ts)
    o6 = o_rows.reshape(bs, KVH, GROUP, ts, KVH, HD)
    o_nsa = jnp.stack([o6[:, kvh, :, :, kvh, :] for kvh in range(KVH)], axis=1)
    o_nsa = o_nsa.transpose(0, 3, 1, 2, 4).reshape(n, NSA_W).astype(BF16)

    o_hg, st = _hgrn_sample(z, lw['hg_lb'], lw['hg_norm'], state, bs, ts, nb=min(8, bs))
    h1 = _merge(o_nsa, o_hg, z, x, lw['w_bn'], lw['w_bh'], lw['w_out'], lw['g_post_mix'], tm=min(512, n))
    h2 = _mlp(h1, p, lw['w_up'], lw['w_down'], lw['w_gate'], lw['w_proj'],
              lw['g_pre_mlp'], lw['g_post_mlp'], lw['g_ple'], tm=min(512, n))
    kv6 = lambda a: a.reshape(bs, ts, 2, KVH, HD)
    win_buf = jnp.concatenate([cache_win, kv6(kv_w)], axis=1)[:, ts:]
    return h2, kv6(kv_c), kv6(kv_s), win_buf, st


def kernel(x_prompt, x_sample, cache_cmp_kv, cache_sel_kv, cache_win_kv, state_hgrn, page_table, p_prompt,
           p_sample, w_in, cmp_k_w1, cmp_k_w2, cmp_v_w1, cmp_v_w2, cmp_pos, hg_lb_logits, hg_norm, w_branch_nsa,
           w_branch_hgrn, w_out, norm_pre_mix, norm_post_mix, norm_pre_mlp, norm_post_mlp, w_mlp_up, w_mlp_down,
           norm_ple, w_ple_gate, w_ple_proj):
    depth = w_in.shape[0]
    b, t, d = x_prompt.shape
    bs, ts, _ = x_sample.shape
    lb_all = jnp.cumsum(jax.nn.softmax(hg_lb_logits.astype(F32), axis=0), axis=0)
    h_p = x_prompt.reshape(b * t, d)
    h_s = x_sample.reshape(bs * ts, d)
    outs = [[] for _ in range(8)]
    for i in range(depth):
        lw = _prep_layer(i, lb_all, w_in, cmp_k_w1, cmp_k_w2, cmp_v_w1, cmp_v_w2, cmp_pos, hg_norm, w_branch_nsa,
                         w_branch_hgrn, w_out, norm_pre_mix, norm_post_mix, norm_pre_mlp, norm_post_mlp, w_mlp_up,
                         w_mlp_down, norm_ple, w_ple_gate, w_ple_proj)
        h_p, *res_p = _layer_prompt(h_p, p_prompt[i].reshape(b * t, -1), lw, b, t)
        h_s, *res_s = _layer_sample(h_s, p_sample[i].reshape(bs * ts, -1), cache_cmp_kv[i], cache_sel_kv[i],
                                    cache_win_kv[i], state_hgrn[i], page_table, lw, bs, ts)
        for lst, v in zip(outs, res_p + res_s):
            lst.append(v)
    return (h_p.reshape(b, t, d), h_s.reshape(bs, ts, d)) + tuple(jnp.stack(lst, axis=0) for lst in outs)
```

```python
import functools
import math

import numpy as np
import jax
import jax.numpy as jnp
from jax import lax
from jax.experimental import pallas as pl
from jax.experimental.pallas import tpu as pltpu

F32 = jnp.float32
BF16 = jnp.bfloat16

D_MODEL = 1024
NSA_HEADS = 8
KVH = 2
GROUP = NSA_HEADS // KVH
HD = 64
NSA_W = NSA_HEADS * HD
KV_W = KVH * HD
CMP_BLOCK = 32
CMP_STRIDE = 16
CMP_HIDDEN = 2 * HD
SEL_BLOCK = 64
SEL_TOP = 16
SEL_LOCAL = 2
WINDOW = 512
HG_W = 512
HG_H = 4
HG_D = 128
HG_CHUNK = 64
D_FF = 4 * D_MODEL
PLE_DIM = 256
RMS_EPS = 1e-6
NEG = -1e30
M_INIT = -1e20
SCALE = HD ** -0.5
HG_SAFE_EXP = 60.0

LANE = 128
SUBLANE = 8
VMEM_LIMIT = 48 * 1024 * 1024

C_GM = 0
C_Q = 2048
C_HQ = 2560
C_HF = 3072
C_HI = 3584
C_HG = 4096
C_KVC = 4608
C_KVS = 4864
C_KVW = 5120
C_GN = 5376
Z_COLS = 5632
QB = 128
HI = lax.Precision.HIGHEST


def _cparams(sem, vmem=VMEM_LIMIT):
    return pltpu.CompilerParams(dimension_semantics=sem, vmem_limit_bytes=vmem)


def _rms(x, g):
    return x * lax.rsqrt(jnp.mean(x * x, axis=-1, keepdims=True) + RMS_EPS) * g


def _sigmoid(x):
    return 1.0 / (1.0 + jnp.exp(-x))


def _gelu_tanh(x):
    return 0.5 * x * (1.0 + jnp.tanh(math.sqrt(2.0 / math.pi) * (x + 0.044715 * (x * x * x))))


def _const_spec(shape):
    nd = len(shape)
    return pl.BlockSpec(shape, lambda *_: (0,) * nd)


def _norm_matmul_kernel(x_ref, g_ref, w_ref, o_ref, xn_ref):
    @pl.when(pl.program_id(1) == 0)
    def _():
        xn_ref[...] = _rms(x_ref[...], g_ref[...]).astype(BF16)

    o_ref[...] = jnp.dot(xn_ref[...], w_ref[...], preferred_element_type=F32)


def _norm_matmul(x, g, w, tm, tn):
    n, d = x.shape
    c = w.shape[1]
    return pl.pallas_call(
        _norm_matmul_kernel,
        grid=(n // tm, c // tn),
        in_specs=[pl.BlockSpec((tm, d), lambda i, j: (i, 0)),
                  pl.BlockSpec((1, d), lambda i, j: (0, 0)),
                  pl.BlockSpec((d, tn), lambda i, j: (0, j))],
        out_specs=pl.BlockSpec((tm, tn), lambda i, j: (i, j)),
        out_shape=jax.ShapeDtypeStruct((n, c), F32),
        scratch_shapes=[pltpu.VMEM((tm, d), BF16)],
        compiler_params=_cparams(("parallel", "arbitrary")),
        name="norm_in_proj",
    )(x, g, w)


def _compress_math(load_rows, n_seg, w1_ref, w2_ref, pos_ref, kv):
    x = jnp.concatenate([load_rows(s) for s in range(CMP_STRIDE)], axis=1).astype(BF16)
    h0 = jnp.dot(x, w1_ref[kv, 0], preferred_element_type=F32)
    h1 = jnp.dot(x, w1_ref[kv, 1], preferred_element_type=F32)
    posb = (jnp.dot(pos_ref[0], w1_ref[kv, 0], preferred_element_type=F32)
            + jnp.dot(pos_ref[1], w1_ref[kv, 1], preferred_element_type=F32))
    hid = h0 + pltpu.roll(h1, n_seg - 1, axis=0) + posb[0:1]
    return jnp.dot(_gelu_tanh(hid).astype(BF16), w2_ref[kv], preferred_element_type=F32)


def _compress_prompt_kernel(xk_ref, xv_ref, w1_ref, w2_ref, pos_ref, kc_ref, vc_ref):
    n_seg = xk_ref.shape[0] // CMP_STRIDE
    for kv, x_ref, out_ref in ((0, xk_ref, kc_ref), (1, xv_ref, vc_ref)):
        load = lambda s, x_ref=x_ref: x_ref[pl.ds(s, n_seg, stride=CMP_STRIDE), :]
        out_ref[...] = _compress_math(load, n_seg, w1_ref, w2_ref, pos_ref, kv).astype(BF16)


def _compress_prompt(z, w1, w2, pos, b, t):
    n_seg = t // CMP_STRIDE
    return pl.pallas_call(
        _compress_prompt_kernel,
        grid=(b,),
        in_specs=[pl.BlockSpec((t, KV_W), lambda i: (i, C_KVC // KV_W)),
                  pl.BlockSpec((t, KV_W), lambda i: (i, C_KVC // KV_W + 1)),
                  _const_spec(w1.shape), _const_spec(w2.shape), _const_spec(pos.shape)],
        out_specs=[pl.BlockSpec((None, n_seg, KV_W), lambda i: (i, 0, 0)),
                   pl.BlockSpec((None, n_seg, KV_W), lambda i: (i, 0, 0))],
        out_shape=[jax.ShapeDtypeStruct((b, n_seg, KV_W), BF16),
                   jax.ShapeDtypeStruct((b, n_seg, KV_W), BF16)],
        compiler_params=_cparams(("parallel",)),
        name="compress_prompt",
    )(z, z, w1, w2, pos)


def _nsa_prompt_kernel(q_ref, gn_ref, kc_ref, vct_ref, ks_ref, vst_ref, kw_ref, vwt_ref,
                       bias0_ref, slopes_ref, covert_ref, o_ref,
                       m_ref, l_ref, acc_ref, sel_ref):
    i = pl.program_id(1)
    t0 = i * QB
    n_seg = kc_ref.shape[0]
    n_sel = covert_ref.shape[0]
    nl = NSA_HEADS * QB
    half = GROUP * QB
    slopes = slopes_ref[...]

    q = q_ref[...] * SCALE
    zero = jnp.zeros((HD, QB), F32)
    cols = []
    for kvh in range(KVH):
        for p in range(GROUP // 2):
            c0 = (kvh * (GROUP // 2) + p) * LANE
            blk = q[:, c0:c0 + LANE].T
            for hh in range(2):
                piece = blk[hh * HD:(hh + 1) * HD, :]
                cols.append(jnp.concatenate([piece, zero] if kvh == 0 else [zero, piece], axis=0))
    qbdt = jnp.concatenate(cols, axis=1).astype(BF16)

    lane = lax.broadcasted_iota(jnp.int32, (QB, QB), 1)
    sub = lax.broadcasted_iota(jnp.int32, (QB, QB), 0)
    dl = (lane - sub).astype(F32)

    sc = jnp.dot(kc_ref[...], qbdt, preferred_element_type=F32)
    n_i = lax.broadcasted_iota(jnp.int32, (n_seg, QB), 0)
    t_i = lax.broadcasted_iota(jnp.int32, (n_seg, QB), 1)
    d1 = (t0 + t_i - (n_i * CMP_STRIDE + (CMP_BLOCK - 1))).astype(F32)
    dc = jnp.concatenate([d1] * NSA_HEADS, axis=1)
    okc = dc >= 0.0
    sc = jnp.where(okc, sc - slopes * dc, NEG)
    mc = jnp.max(sc, axis=0, keepdims=True)
    pc = jnp.where(okc, jnp.exp(sc - mc), 0.0)
    lc = jnp.sum(pc, axis=0, keepdims=True)
    pc = pc * (1.0 / jnp.where(lc > 0.0, lc, 1.0))
    pcb = pc.astype(BF16)
    ocmp = [jnp.dot(vct_ref[kvh * HD:(kvh + 1) * HD, :], pcb[:, kvh * half:(kvh + 1) * half],
                    preferred_element_type=F32) for kvh in range(KVH)]

    jblk = lax.broadcasted_iota(jnp.int32, (n_sel, QB), 0)
    qblk = (t0 + lax.broadcasted_iota(jnp.int32, (n_sel, QB), 1)) // SEL_BLOCK
    back = qblk - jblk
    visible = back >= 0
    forced = (jblk == 0) | (visible & (back < SEL_LOCAL))
    for kvh in range(KVH):
        psum = pc[:, kvh * half:kvh * half + QB]
        for g in range(1, GROUP):
            psum = psum + pc[:, kvh * half + g * QB:kvh * half + (g + 1) * QB]
        imp = jnp.dot(covert_ref[...], psum, precision=HI, preferred_element_type=F32)
        score = jnp.where(forced, 1e9, jnp.where(visible, imp, -1e9))
        rank = jnp.zeros((n_sel, QB), F32)
        for jp in range(n_sel):
            row = score[jp:jp + 1, :]
            ge = jnp.where(row >= score, 1.0, 0.0)
            gt = jnp.where(row > score, 1.0, 0.0)
            rank = rank + jnp.where(jblk > jp, ge, gt)
        sel_ref[kvh] = jnp.where(visible, jnp.where(rank < float(min(SEL_TOP, n_sel)), 1.0, 0.0), 0.0)

    def sweep(k_ref, vt_ref, lo, hi, tile_bias):
        m_ref[...] = jnp.full((1, nl), M_INIT, F32)
        l_ref[...] = jnp.zeros((1, nl), F32)
        acc_ref[...] = jnp.zeros((KVH, HD, half), F32)

        def body(jt, carry):
            off = (t0 - jt * QB).astype(F32)
            s = jnp.dot(k_ref[jt], qbdt, preferred_element_type=F32)
            s = s - bias0_ref[...] - slopes * off + tile_bias(jt, off)
            m_old = m_ref[...]
            m_new = jnp.maximum(m_old, jnp.max(s, axis=0, keepdims=True))
            alpha = jnp.exp(m_old - m_new)
            p = jnp.exp(s - m_new)
            l_ref[...] = alpha * l_ref[...] + jnp.sum(p, axis=0, keepdims=True)
            m_ref[...] = m_new
            pb = p.astype(BF16)
            vt = vt_ref[jt]
            for kvh in range(KVH):
                pv = jnp.dot(vt[kvh * HD:(kvh + 1) * HD, :], pb[:, kvh * half:(kvh + 1) * half],
                             preferred_element_type=F32)
                acc_ref[kvh] = alpha[:, kvh * half:(kvh + 1) * half] * acc_ref[kvh] + pv
            return carry

        lax.fori_loop(lo, hi, body, 0)
        inv = 1.0 / l_ref[...]
        return [acc_ref[kvh] * inv[:, kvh * half:(kvh + 1) * half] for kvh in range(KVH)]

    def sel_bias(jt, off):
        causal = dl >= -off
        parts = []
        for kvh in range(KVH):
            rows = sel_ref[kvh, pl.ds(jt * (QB // SEL_BLOCK), QB // SEL_BLOCK), :]
            blockmask = jnp.concatenate(
                [jnp.broadcast_to(rows[r:r + 1, :], (SEL_BLOCK, QB)) for r in range(QB // SEL_BLOCK)], axis=0)
            nb = jnp.where(causal, (blockmask - 1.0) * (-NEG), NEG)
            parts += [nb] * GROUP
        return jnp.concatenate(parts, axis=1)

    def win_bias(jt, off):
        nb = jnp.where(dl >= -off, jnp.where(dl < float(WINDOW) - off, 0.0, NEG), NEG)
        return jnp.concatenate([nb] * NSA_HEADS, axis=1)

    osel = sweep(ks_ref, vst_ref, 0, i + 1, sel_bias)
    owin = sweep(kw_ref, vwt_ref, jnp.maximum(i - WINDOW // QB, 0), i + 1, win_bias)

    gt_ = _sigmoid(gn_ref[...]).T
    for kvh in range(KVH):
        for p in range(GROUP // 2):
            pieces = []
            for hh in range(2):
                g = 2 * p + hh
                h = kvh * GROUP + g
                cs = slice(g * QB, (g + 1) * QB)
                pieces.append(gt_[3 * h:3 * h + 1, :] * ocmp[kvh][:, cs]
                              + gt_[3 * h + 1:3 * h + 2, :] * osel[kvh][:, cs]
                              + gt_[3 * h + 2:3 * h + 3, :] * owin[kvh][:, cs])
            c0 = (kvh * (GROUP // 2) + p) * LANE
            o_ref[:, c0:c0 + LANE] = jnp.concatenate(pieces, axis=0).T.astype(BF16)


PB = 256
N_AUG = 16
LOG2E = math.log2(math.e)


def _split3(x):
    hi = x.astype(BF16).astype(F32)
    r = x - hi
    mid = r.astype(BF16).astype(F32)
    return hi, mid, r - mid


def _nsa_prompt256_kernel(q_ref, gn_ref, kc_ref, vct_ref, zks_ref, zvs_ref, zkw_ref, zvw_ref,
                          augc_ref, slopes_ref, covert_ref, o_ref,
                          qa_ref, m_ref, l_ref, acc_ref, sel_ref, score_ref, ks_ref, vst_ref, kw_ref, vwt_ref,
                          sa_ref, sb_ref):
    i = pl.program_id(1)
    t0 = i * PB

    @pl.when(i == 0)
    def _():
        for zk, zv, k_dst, vt_dst in ((zks_ref, zvs_ref, ks_ref, vst_ref), (zkw_ref, zvw_ref, kw_ref, vwt_ref)):
            for j in range(k_dst.shape[0]):
                k_dst[j] = zk[j * PB:(j + 1) * PB, :].astype(BF16)
                vt_dst[j] = zv[j * PB:(j + 1) * PB, :].T.astype(BF16)
    n_seg = kc_ref.shape[0]
    n_sel = covert_ref.shape[0]
    nl = NSA_HEADS * PB
    half = GROUP * PB
    slopes2 = slopes_ref[...] * LOG2E
    tlane = (lax.broadcasted_iota(jnp.int32, (1, nl), 1) % PB).astype(F32)
    blocks_per_tile = PB // SEL_BLOCK

    q = q_ref[...] * (SCALE * LOG2E)
    zero = jnp.zeros((HD, PB), F32)
    cols = []
    for kvh in range(KVH):
        for p in range(GROUP // 2):
            c0 = (kvh * (GROUP // 2) + p) * LANE
            blk = q[:, c0:c0 + LANE].T
            for hh in range(2):
                piece = blk[hh * HD:(hh + 1) * HD, :]
                cols.append(jnp.concatenate([piece, zero] if kvh == 0 else [zero, piece], axis=0))
    qbdt = jnp.concatenate(cols, axis=1).astype(BF16)
    qa_ref[0:KV_W, :] = qbdt
    qa_ref[KV_W + N_AUG:2 * KV_W, :] = jnp.zeros((KV_W - N_AUG, nl), BF16)

    lane = lax.broadcasted_iota(jnp.int32, (PB, PB), 1)
    sub = lax.broadcasted_iota(jnp.int32, (PB, PB), 0)
    causal_bias = jnp.where(lane >= sub, 0.0, NEG)
    edge_bias = jnp.where(lane < sub, 0.0, NEG)

    sc = jnp.dot(kc_ref[...], qbdt, preferred_element_type=F32)
    n_i = lax.broadcasted_iota(jnp.int32, (n_seg, PB), 0)
    t_i = lax.broadcasted_iota(jnp.int32, (n_seg, PB), 1)
    d1 = (t0 + t_i - (n_i * CMP_STRIDE + (CMP_BLOCK - 1))).astype(F32)
    dc = jnp.concatenate([d1] * NSA_HEADS, axis=1)
    okc = dc >= 0.0
    sc = jnp.where(okc, sc - slopes2 * dc, NEG)
    mc = jnp.max(sc, axis=0, keepdims=True)
    pc = jnp.where(okc, jnp.exp2(sc - mc), 0.0)
    lc = jnp.sum(pc, axis=0, keepdims=True)
    pc = pc * (1.0 / jnp.where(lc > 0.0, lc, 1.0))
    pcb = pc.astype(BF16)
    ocmp = [jnp.dot(vct_ref[kvh * HD:(kvh + 1) * HD, :], pcb[:, kvh * half:(kvh + 1) * half],
                    preferred_element_type=F32) for kvh in range(KVH)]

    jblk = lax.broadcasted_iota(jnp.int32, (n_sel, PB), 0)
    qblk = (t0 + lax.broadcasted_iota(jnp.int32, (n_sel, PB), 1)) // SEL_BLOCK
    back = qblk - jblk
    visible = back >= 0
    forced = (jblk == 0) | (visible & (back < SEL_LOCAL))
    n_groups = jnp.minimum((t0 + PB - 1) // SEL_BLOCK // SUBLANE + 1, n_sel // SUBLANE)
    for kvh in range(KVH):
        psum = pc[:, kvh * half:kvh * half + PB]
        for g in range(1, GROUP):
            psum = psum + pc[:, kvh * half + g * PB:kvh * half + (g + 1) * PB]
        imp = jnp.dot(covert_ref[...], psum, precision=HI, preferred_element_type=F32)
        score = jnp.where(forced, 1e9, jnp.where(visible, imp, -1e9))
        score_ref[...] = score

        def rank_group(gi, rank):
            rows8 = score_ref[pl.ds(pl.multiple_of(gi * SUBLANE, SUBLANE), SUBLANE), :]
            for u in range(SUBLANE):
                row = rows8[u:u + 1, :]
                ge = jnp.where(row >= score, 1.0, 0.0)
                gt = jnp.where(row > score, 1.0, 0.0)
                rank = rank + jnp.where(jblk > gi * SUBLANE + u, ge, gt)
            return rank

        rank = lax.fori_loop(0, n_groups, rank_group, jnp.zeros((n_sel, PB), F32))
        sel_ref[kvh] = jnp.where(visible, jnp.where(rank < float(min(SEL_TOP, n_sel)), 0.0, NEG), NEG)

    sl3 = _split3(slopes2)

    def scores(k_ref, jt, use_sel, dst_ref):
        off = (t0 - jt * PB).astype(F32)
        c3 = _split3(-slopes2 * (tlane + off))
        rows = list(sl3) + list(c3)
        if use_sel:
            tiles_per_group = SUBLANE // blocks_per_tile
            base = pl.multiple_of((jt // tiles_per_group) * SUBLANE, SUBLANE)
            which = jt % tiles_per_group
            per_kvh = []
            for kvh in range(KVH):
                rows8 = sel_ref[kvh, pl.ds(base, SUBLANE), :]
                mine = rows8[0:blocks_per_tile, :]
                for w in range(1, tiles_per_group):
                    mine = jnp.where(which == w, rows8[w * blocks_per_tile:(w + 1) * blocks_per_tile, :], mine)
                per_kvh.append(mine)
            rows.append(jnp.concatenate([per_kvh[kvh] for kvh in range(KVH) for _ in range(GROUP)], axis=1))
        rows.append(jnp.zeros((N_AUG - sum(r.shape[0] for r in rows), nl), F32))
        qa_ref[KV_W:KV_W + N_AUG, :] = jnp.concatenate(rows, axis=0).astype(BF16)
        ka = jnp.concatenate([k_ref[jt], augc_ref[...]], axis=1)
        dst_ref[...] = jnp.dot(ka, qa_ref[...], preferred_element_type=F32)

    def softmax_pv(vt_ref, jt, src_ref, extra):
        s = src_ref[...]
        if extra is not None:
            s = s + jnp.concatenate([extra] * NSA_HEADS, axis=1)
        m_old = m_ref[...]
        m_new = jnp.maximum(m_old, jnp.max(s, axis=0, keepdims=True))
        alpha = jnp.exp2(m_old - m_new)
        p = jnp.exp2(s - m_new)
        l_ref[...] = alpha * l_ref[...] + jnp.sum(p, axis=0, keepdims=True)
        m_ref[...] = m_new
        pb = p.astype(BF16)
        vt = vt_ref[jt]
        for kvh in range(KVH):
            pv = jnp.dot(vt[kvh * HD:(kvh + 1) * HD, :], pb[:, kvh * half:(kvh + 1) * half],
                         preferred_element_type=F32)
            acc_ref[kvh] = alpha[:, kvh * half:(kvh + 1) * half] * acc_ref[kvh] + pv

    def reset():
        m_ref[...] = jnp.full((1, nl), M_INIT, F32)
        l_ref[...] = jnp.zeros((1, nl), F32)
        acc_ref[...] = jnp.zeros((KVH, HD, half), F32)

    def result():
        inv = 1.0 / l_ref[...]
        return [acc_ref[kvh] * inv[:, kvh * half:(kvh + 1) * half] for kvh in range(KVH)]

    def step(k_ref, vt_ref, jt, use_sel, extra):
        scores(k_ref, jt, use_sel, sa_ref)
        softmax_pv(vt_ref, jt, sa_ref, extra)

    reset()
    n_pairs = (i + 2) // 2
    last = pl.num_programs(1) - 1
    odd_i = (i % 2) == 1
    scores(ks_ref, 0, True, sa_ref)

    def pair(u, carry):
        scores(ks_ref, 2 * u + 1, True, sb_ref)
        softmax_pv(vst_ref, 2 * u, sa_ref, None)
        scores(ks_ref, 2 * u + 2, True, sa_ref)
        softmax_pv(vst_ref, 2 * u + 1, sb_ref, None)
        return carry

    lax.fori_loop(0, n_pairs - 1, pair, 0)
    j0 = 2 * n_pairs - 2
    j1 = jnp.minimum(j0 + 1, last)
    scores(ks_ref, j1, True, sb_ref)
    softmax_pv(vst_ref, j0, sa_ref, jnp.where(odd_i, 0.0, causal_bias))
    softmax_pv(vst_ref, j1, sb_ref, jnp.where(odd_i, causal_bias, NEG))
    osel = result()

    reset()

    @pl.when(i >= WINDOW // PB)
    def _():
        step(kw_ref, vwt_ref, i - WINDOW // PB, False, edge_bias)

    @pl.when(i >= 1)
    def _():
        step(kw_ref, vwt_ref, i - 1, False, None)

    step(kw_ref, vwt_ref, i, False, causal_bias)
    owin = result()

    gt_ = _sigmoid(gn_ref[...]).T
    for kvh in range(KVH):
        for p in range(GROUP // 2):
            pieces = []
            for hh in range(2):
                g = 2 * p + hh
                h = kvh * GROUP + g
                cs = slice(g * PB, (g + 1) * PB)
                pieces.append(gt_[3 * h:3 * h + 1, :] * ocmp[kvh][:, cs]
                              + gt_[3 * h + 1:3 * h + 2, :] * osel[kvh][:, cs]
                              + gt_[3 * h + 2:3 * h + 3, :] * owin[kvh][:, cs])
            c0 = (kvh * (GROUP // 2) + p) * LANE
            o_ref[:, c0:c0 + LANE] = jnp.concatenate(pieces, axis=0).T.astype(BF16)


def _aug_key_columns():
    a = np.zeros((PB, KV_W), np.float32)
    s = np.arange(PB)
    a[:, 0:3] = s[:, None]
    a[:, 3:6] = 1.0
    for r in range(PB // SEL_BLOCK):
        a[:, 6 + r] = (s // SEL_BLOCK == r)
    return a


def _nsa_prompt256(z, kc, vct, b, t):
    nq = t // PB
    n_seg = t // CMP_STRIDE
    n_sel = t // SEL_BLOCK
    nl = NSA_HEADS * PB
    assert 6 + PB // SEL_BLOCK <= N_AUG and n_sel % SUBLANE == 0 and WINDOW == 2 * PB
    slopes = _alibi_slopes_lanes(PB)
    covert = _cover_matrix(n_seg, n_sel).T.copy()
    kv_col = lambda c0: pl.BlockSpec((t, KV_W), lambda bi, i: (bi, c0 // KV_W))
    return pl.pallas_call(
        _nsa_prompt256_kernel,
        grid=(b, nq),
        in_specs=[pl.BlockSpec((PB, NSA_W), lambda bi, i: (bi * nq + i, C_Q // NSA_W)),
                  pl.BlockSpec((PB, LANE), lambda bi, i: (bi * nq + i, C_GN // LANE)),
                  pl.BlockSpec((None, n_seg, KV_W), lambda bi, i: (bi, 0, 0)),
                  pl.BlockSpec((None, KV_W, n_seg), lambda bi, i: (bi, 0, 0)),
                  kv_col(C_KVS), kv_col(C_KVS + KV_W), kv_col(C_KVW), kv_col(C_KVW + KV_W),
                  _const_spec((PB, KV_W)), _const_spec((1, nl)), _const_spec((n_sel, n_seg))],
        out_specs=pl.BlockSpec((PB, NSA_W), lambda bi, i: (bi * nq + i, 0)),
        out_shape=jax.ShapeDtypeStruct((b * t, NSA_W), BF16),
        scratch_shapes=[pltpu.VMEM((2 * KV_W, nl), BF16), pltpu.VMEM((1, nl), F32), pltpu.VMEM((1, nl), F32),
                        pltpu.VMEM((KVH, HD, GROUP * PB), F32), pltpu.VMEM((KVH, n_sel, PB), F32),
                        pltpu.VMEM((n_sel, PB), F32),
                        pltpu.VMEM((nq, PB, KV_W), BF16), pltpu.VMEM((nq, KV_W, PB), BF16),
                        pltpu.VMEM((nq, PB, KV_W), BF16), pltpu.VMEM((nq, KV_W, PB), BF16),
                        pltpu.VMEM((PB, nl), F32), pltpu.VMEM((PB, nl), F32)],
        compiler_params=_cparams(("parallel", "arbitrary")),
        name="nsa_prompt",
    )(z, z, kc, vct, z, z, z, z, jnp.asarray(_aug_key_columns(), BF16), jnp.asarray(slopes),
      jnp.asarray(covert))


def _alibi_slopes_lanes(width):
    h = np.arange(1, NSA_HEADS + 1, dtype=np.float32)
    return np.repeat(np.exp2(-8.0 * h / NSA_HEADS), width)[None, :].astype(np.float32)


def _cover_matrix(n_cmp_rows, n_sel_rows):
    n = np.arange(n_cmp_rows)[:, None] * CMP_STRIDE
    j = np.arange(n_sel_rows)[None, :] * SEL_BLOCK
    return ((n < j + SEL_BLOCK) & (n + CMP_BLOCK > j)).astype(np.float32)


def _nsa_prompt(z, kc, vct, ks, vst, kw, vwt, b, t):
    nq = t // QB
    n_seg = t // CMP_STRIDE
    n_sel = t // SEL_BLOCK
    nl = NSA_HEADS * QB
    slopes = _alibi_slopes_lanes(QB)
    dl = (np.arange(QB)[None, :] - np.arange(QB)[:, None]).astype(np.float32)
    bias0 = np.tile(dl, (1, NSA_HEADS)) * slopes
    covert = _cover_matrix(n_seg, n_sel).T.copy()
    tile4 = lambda: pl.BlockSpec((None, nq, QB, QB), lambda bi, i: (bi, 0, 0, 0))
    return pl.pallas_call(
        _nsa_prompt_kernel,
        grid=(b, nq),
        in_specs=[pl.BlockSpec((QB, NSA_W), lambda bi, i: (bi * nq + i, C_Q // NSA_W)),
                  pl.BlockSpec((QB, LANE), lambda bi, i: (bi * nq + i, C_GN // LANE)),
                  pl.BlockSpec((None, n_seg, KV_W), lambda bi, i: (bi, 0, 0)),
                  pl.BlockSpec((None, KV_W, n_seg), lambda bi, i: (bi, 0, 0)),
                  tile4(), tile4(), tile4(), tile4(),
                  _const_spec((QB, nl)), _const_spec((1, nl)), _const_spec((n_sel, n_seg))],
        out_specs=pl.BlockSpec((QB, NSA_W), lambda bi, i: (bi * nq + i, 0)),
        out_shape=jax.ShapeDtypeStruct((b * t, NSA_W), BF16),
        scratch_shapes=[pltpu.VMEM((1, nl), F32), pltpu.VMEM((1, nl), F32),
                        pltpu.VMEM((KVH, HD, GROUP * QB), F32), pltpu.VMEM((KVH, n_sel, QB), F32)],
        compiler_params=_cparams(("parallel", "arbitrary")),
        name="nsa_prompt",
    )(z, z, kc, vct, ks, vst, kw, vwt, jnp.asarray(bias0), jnp.asarray(slopes), jnp.asarray(covert))


def _hgrn_gates(pre, lb):
    log_f = jnp.log(lb + (1.0 - lb) * _sigmoid(pre))
    k = (1.0 - lb) * _sigmoid(-pre)
    return log_f, k


def _hgrn_out(o, gate, ng):
    outs = []
    for h in range(HG_H):
        sl = slice(h * HG_D, (h + 1) * HG_D)
        g = gate[:, sl]
        outs.append(_rms(o[:, sl], ng) * (g * _sigmoid(g)))
    return jnp.concatenate(outs, axis=1)


def _hgrn_prompt_kernel(hq_ref, hf_ref, hi_ref, hg_ref, lb_ref, ng_ref, tri_ref, o_ref, st_ref,
                        s_ref, oraw_ref):
    ci = pl.program_id(1)
    tc = hq_ref.shape[0]
    c = HG_CHUNK

    @pl.when(ci == 0)
    def _():
        s_ref[...] = jnp.zeros(s_ref.shape, F32)

    lb = lb_ref[...]
    tril = (lax.broadcasted_iota(jnp.int32, (c, c), 0) >= lax.broadcasted_iota(jnp.int32, (c, c), 1))
    row8 = lax.broadcasted_iota(jnp.int32, (SUBLANE, HG_D), 0)

    n_chunks = tc // c
    q_all = hq_ref[...]
    v_all = hi_ref[...]
    log_f, k_all = _hgrn_gates(hf_ref[...], lb)
    chunks = []
    worst = None
    for g in range(n_chunks):
        rows = slice(g * c, (g + 1) * c)
        bcum = jnp.dot(tri_ref[...], log_f[rows], precision=HI, preferred_element_type=F32)
        e = bcum - bcum[c // 2 - 1:c // 2, :]
        chunks.append((rows, bcum, e))
        worst = jnp.abs(e) if worst is None else jnp.maximum(worst, jnp.abs(e))
    safe = jnp.max(worst) < HG_SAFE_EXP

    @pl.when(safe)
    def _():
        for rows, bcum, e in chunks:
            q, k, v = q_all[rows], k_all[rows], v_all[rows]
            b_last = bcum[c - 1:c, :]
            qt = (q * jnp.exp(e)).astype(BF16)
            kt = (k * jnp.exp(-e)).astype(BF16)
            qb = (q * jnp.exp(bcum)).astype(BF16)
            kh = (k * jnp.exp(b_last - bcum)).astype(BF16)
            dec = jnp.exp(b_last)
            vb = v.astype(BF16)
            for h in range(HG_H):
                sl = slice(h * HG_D, (h + 1) * HG_D)
                a = lax.dot_general(qt[:, sl], kt[:, sl], (((1,), (1,)), ((), ())), preferred_element_type=F32)
                a = jnp.where(tril, a, 0.0).astype(BF16)
                st = s_ref[h]
                o = (lax.dot_general(qb[:, sl], st.astype(BF16), (((1,), (1,)), ((), ())),
                                     preferred_element_type=F32)
                     + jnp.dot(a, vb[:, sl], preferred_element_type=F32))
                oraw_ref[rows, sl] = o
                s_ref[h] = st * dec[:, sl] + lax.dot_general(vb[:, sl], kh[:, sl], (((0,), (0,)), ((), ())),
                                                             preferred_element_type=F32)

    @pl.when(jnp.logical_not(safe))
    def _():
        for h in range(HG_H):
            sl = slice(h * HG_D, (h + 1) * HG_D)

            def tile(ti, carry, sl=sl, h=h):
                r = pl.multiple_of(ti * SUBLANE, SUBLANE)
                q8 = hq_ref[pl.ds(r, SUBLANE), sl]
                v8 = hi_ref[pl.ds(r, SUBLANE), sl]
                lf8, k8 = _hgrn_gates(hf_ref[pl.ds(r, SUBLANE), sl], lb[:, sl])
                f8 = jnp.exp(lf8)
                st = s_ref[h]
                rows_out = []
                for u in range(SUBLANE):
                    vu = jnp.where(row8 == 0, jnp.broadcast_to(v8[u:u + 1, :], (SUBLANE, HG_D)), 0.0)
                    ku = jnp.broadcast_to(k8[u:u + 1, :], (SUBLANE, HG_D))
                    qu = jnp.broadcast_to(q8[u:u + 1, :], (SUBLANE, HG_D))
                    st = st * f8[u:u + 1, :] + lax.dot_general(
                        vu, ku, (((0,), (0,)), ((), ())), precision=HI, preferred_element_type=F32)
                    ou = lax.dot_general(qu, st, (((1,), (1,)), ((), ())), precision=HI,
                                         preferred_element_type=F32)
                    rows_out.append(ou[0:1, :])
                s_ref[h] = st
                oraw_ref[pl.ds(r, SUBLANE), sl] = jnp.concatenate(rows_out, axis=0)
                return carry

            lax.fori_loop(0, tc // SUBLANE, tile, 0)

    o_ref[...] = _hgrn_out(oraw_ref[...], hg_ref[...], ng_ref[...]).astype(BF16)

    @pl.when(ci == pl.num_programs(1) - 1)
    def _():
        for h in range(HG_H):
            st_ref[h] = s_ref[h].T


def _hgrn_prompt(z, lb, ng, b, t, tc):
    nc = t // tc
    col = lambda c0: pl.BlockSpec((tc, HG_W), lambda bi, ci: (bi * nc + ci, c0 // HG_W))
    tri = np.tril(np.ones((HG_CHUNK, HG_CHUNK), np.float32))
    return pl.pallas_call(
        _hgrn_prompt_kernel,
        grid=(b, nc),
        in_specs=[col(C_HQ), col(C_HF), col(C_HI), col(C_HG),
                  _const_spec((1, HG_W)), _const_spec((1, HG_D)), _const_spec((HG_CHUNK, HG_CHUNK))],
        out_specs=[pl.BlockSpec((tc, HG_W), lambda bi, ci: (bi * nc + ci, 0)),
                   pl.BlockSpec((None, HG_H, HG_D, HG_D), lambda bi, ci: (bi, 0, 0, 0))],
        out_shape=[jax.ShapeDtypeStruct((b * t, HG_W), BF16),
                   jax.ShapeDtypeStruct((b, HG_H, HG_D, HG_D), F32)],
        scratch_shapes=[pltpu.VMEM((HG_H, HG_D, HG_D), F32), pltpu.VMEM((tc, HG_W), F32)],
        compiler_params=_cparams(("parallel", "arbitrary")),
        name="hgrn_prompt",
    )(z, z, z, z, lb, ng, jnp.asarray(tri))


def _merge_kernel(on_ref, oh_ref, ga_ref, gb_ref, x_ref, wn_ref, wh_ref, wo_ref, g_ref, o_ref):
    y = (_sigmoid(ga_ref[...]) * jnp.dot(on_ref[...], wn_ref[...], preferred_element_type=F32)
         + _sigmoid(gb_ref[...]) * jnp.dot(oh_ref[...], wh_ref[...], preferred_element_type=F32))
    mix = jnp.dot(y.astype(BF16), wo_ref[...], preferred_element_type=F32)
    o_ref[...] = x_ref[...] + _rms(mix, g_ref[...])


def _resident(shape):
    nd = len(shape)
    return pl.BlockSpec(shape, lambda *_: (0,) * nd, pipeline_mode=pl.Buffered(1))


def _merge(o_nsa, o_hg, z, x, wn, wh, wo, g, tm):
    n = x.shape[0]
    row = lambda w, cb: pl.BlockSpec((tm, w), lambda i: (i, cb))
    return pl.pallas_call(
        _merge_kernel,
        grid=(n // tm,),
        in_specs=[row(NSA_W, 0), row(HG_W, 0), row(D_MODEL, C_GM // D_MODEL), row(D_MODEL, C_GM // D_MODEL + 1),
                  row(D_MODEL, 0), _resident(wn.shape), _resident(wh.shape), _resident(wo.shape),
                  _resident((1, D_MODEL))],
        out_specs=row(D_MODEL, 0),
        out_shape=jax.ShapeDtypeStruct((n, D_MODEL), F32),
        compiler_params=_cparams(("parallel",)),
        name="merge_out_proj",
    )(o_nsa, o_hg, z, z, x, wn, wh, wo, g)


def _mlp_kernel(h_ref, p_ref, wu_ref, wd_ref, wg_ref, wp_ref, g1_ref, g2_ref, g3_ref, o_ref):
    h = h_ref[...]
    xn = _rms(h, g1_ref[...]).astype(BF16)
    ffn = jnp.zeros(h.shape, F32)
    step = D_MODEL
    for c0 in range(0, D_FF, step):
        up = jnp.dot(xn, wu_ref[:, c0:c0 + step], preferred_element_type=F32)
        act = jnp.square(jnp.maximum(up, 0.0)).astype(BF16)
        ffn = ffn + jnp.dot(act, wd_ref[c0:c0 + step, :], preferred_element_type=F32)
    h = h + _rms(ffn, g2_ref[...])
    gate = _sigmoid(jnp.dot(_rms(h, g3_ref[...]).astype(BF16), wg_ref[...], preferred_element_type=F32))
    o_ref[...] = h + gate * jnp.dot(p_ref[...].astype(BF16), wp_ref[...], preferred_element_type=F32)


def _mlp(h, p, wu, wd, wg, wp, g1, g2, g3, tm):
    n = h.shape[0]
    row = lambda w: pl.BlockSpec((tm, w), lambda i: (i, 0))
    gain = _resident((1, D_MODEL))
    return pl.pallas_call(
        _mlp_kernel,
        grid=(n // tm,),
        in_specs=[row(D_MODEL), row(PLE_DIM), _resident(wu.shape), _resident(wd.shape), _resident(wg.shape),
                  _resident(wp.shape), gain, gain, gain],
        out_specs=row(D_MODEL),
        out_shape=jax.ShapeDtypeStruct((n, D_MODEL), F32),
        compiler_params=_cparams(("parallel",)),
        name="mlp_ple",
    )(h, p, wu, wd, wg, wp, g1, g2, g3)


def _prep_w_in(w):
    sizes = (NSA_W, 2 * KV_W, 2 * KV_W, 2 * KV_W, 3 * NSA_HEADS, HG_W, HG_W, HG_W, HG_W)
    q, kvc, kvs, kvw, gn, hq, hf, hi, hg, gm = jnp.split(w, [int(v) for v in np.cumsum(sizes)], axis=1)
    pad = jnp.zeros((w.shape[0], Z_COLS - C_GN - 3 * NSA_HEADS), w.dtype)
    return jnp.concatenate([gm, q, hq, hf, hi, hg, kvc, kvs, kvw, gn, pad], axis=1).astype(BF16)


def _prep_compress(w1k, w2k, w1v, w2v, pos):
    eye = jnp.eye(KVH, dtype=F32)

    def big1(w1):
        t = jnp.einsum('rsdh,kq->rskdqh', w1, eye)
        return t.reshape(CMP_BLOCK // CMP_STRIDE, CMP_STRIDE * KV_W, KVH * CMP_HIDDEN)

    def big2(w2):
        return jnp.einsum('hd,kq->khqd', w2, eye).reshape(KVH * CMP_HIDDEN, KV_W)

    w1 = jnp.stack([big1(w1k), big1(w1v)]).astype(BF16)
    w2 = jnp.stack([big2(w2k), big2(w2v)]).astype(BF16)
    posb = jnp.broadcast_to(pos[:, :, None, :], pos.shape[:2] + (KVH, HD)).reshape(pos.shape[0], 1, -1)
    posb = jnp.broadcast_to(posb, (pos.shape[0], SUBLANE, posb.shape[-1])).astype(BF16)
    return w1, w2, posb


def _prep_layer(i, lb_all, w_in, cmp_k_w1, cmp_k_w2, cmp_v_w1, cmp_v_w2, cmp_pos, hg_norm, w_branch_nsa,
                w_branch_hgrn, w_out, norm_pre_mix, norm_post_mix, norm_pre_mlp, norm_post_mlp, w_mlp_up,
                w_mlp_down, norm_ple, w_ple_gate, w_ple_proj):
    w1, w2, posb = _prep_compress(cmp_k_w1[i], cmp_k_w2[i], cmp_v_w1[i], cmp_v_w2[i], cmp_pos[i])
    row = lambda a: a[i].reshape(1, -1).astype(F32)
    return {
        'w_in': _prep_w_in(w_in[i]), 'cmp_w1': w1, 'cmp_w2': w2, 'cmp_pos': posb,
        'hg_lb': lb_all[i].reshape(1, HG_W), 'hg_norm': row(hg_norm),
        'w_bn': w_branch_nsa[i].astype(BF16), 'w_bh': w_branch_hgrn[i].astype(BF16), 'w_out': w_out[i].astype(BF16),
        'w_up': w_mlp_up[i].astype(BF16), 'w_down': w_mlp_down[i].astype(BF16),
        'w_gate': w_ple_gate[i].astype(BF16), 'w_proj': w_ple_proj[i].astype(BF16),
        'g_pre_mix': row(norm_pre_mix), 'g_post_mix': row(norm_post_mix), 'g_pre_mlp': row(norm_pre_mlp),
        'g_post_mlp': row(norm_post_mlp), 'g_ple': row(norm_ple),
    }


def _key_tiles(rows, b, t):
    r = rows.reshape(b, t // PB, PB, 2 * KV_W)
    return r[..., :KV_W].astype(BF16), jnp.swapaxes(r[..., KV_W:], 2, 3).astype(BF16)


def _layer_prompt(x, p, lw, b, t):
    n = b * t
    z = _norm_matmul(x, lw['g_pre_mix'], lw['w_in'], tm=min(1024, n), tn=Z_COLS // 4)
    kv_c = z[:, C_KVC:C_KVC + 2 * KV_W]
    kv_s = z[:, C_KVS:C_KVS + 2 * KV_W]
    kv_w = z[:, C_KVW:C_KVW + 2 * KV_W]
    kc, vc = _compress_prompt(z, lw['cmp_w1'], lw['cmp_w2'], lw['cmp_pos'], b, t)
    vct = jnp.swapaxes(vc, 1, 2)
    o_nsa = _nsa_prompt256(z, kc, vct, b, t)
    o_hg, st = _hgrn_prompt(z, lw['hg_lb'], lw['hg_norm'], b, t, tc=min(256, t))
    h1 = _merge(o_nsa, o_hg, z, x, lw['w_bn'], lw['w_bh'], lw['w_out'], lw['g_post_mix'], tm=min(512, n))
    h2 = _mlp(h1, p, lw['w_up'], lw['w_down'], lw['w_gate'], lw['w_proj'],
              lw['g_pre_mlp'], lw['g_post_mlp'], lw['g_ple'], tm=min(512, n))
    kv6 = lambda a: a.reshape(b, t, 2, KVH, HD)
    wb = min(WINDOW, t)
    return h2, kv6(kv_c), kv6(kv_s), kv6(kv_w)[:, t - wb:], st


def _page_fetch(pt_ref, cache_ref, buf_ref, sem):
    b = pl.program_id(0)
    n_pages = pt_ref.shape[1]
    slot = b % 2

    def copy(seq, sl, p):
        return pltpu.make_async_copy(cache_ref.at[pt_ref[seq, p]], buf_ref.at[sl, p], sem.at[sl])

    def start(seq, sl):
        lax.fori_loop(0, n_pages, lambda p, c: (copy(seq, sl, p).start(), c)[1], 0)

    @pl.when(b == 0)
    def _():
        start(0, 0)

    @pl.when(b + 1 < pl.num_programs(0))
    def _():
        start(b + 1, 1 - slot)

    def wait():
        lax.fori_loop(0, n_pages, lambda p, c: (copy(b, slot, p).wait(), c)[1], 0)

    return slot, wait


def _compress_sample_kernel(pt_ref, cache_ref, w1_ref, w2_ref, pos_ref, kc_ref, vc_ref, buf_ref, rows_ref, sem):
    n_pages, page = pt_ref.shape[1], buf_ref.shape[-1]
    n_seg = n_pages * page // CMP_STRIDE
    slot, wait = _page_fetch(pt_ref, cache_ref, buf_ref, sem)
    wait()
    for kv, out_ref in ((0, kc_ref), (1, vc_ref)):
        for p in range(n_pages):
            rows_ref[kv, p * page:(p + 1) * page, :] = buf_ref[slot, p, kv].T
        load = lambda s, kv=kv: rows_ref[kv, pl.ds(s, n_seg, stride=CMP_STRIDE), :]
        out_ref[...] = _compress_math(load, n_seg, w1_ref, w2_ref, pos_ref, kv).astype(BF16)


def _compress_sample(page_table, cache_t, w1, w2, pos):
    bs, n_pages = page_table.shape
    page = cache_t.shape[-1]
    n_seg = n_pages * page // CMP_STRIDE
    const = lambda shape: pl.BlockSpec(shape, lambda i, pt: (0,) * len(shape))
    out = pl.BlockSpec((None, n_seg, KV_W), lambda i, pt: (i, 0, 0))
    return pl.pallas_call(
        _compress_sample_kernel,
        grid_spec=pltpu.PrefetchScalarGridSpec(
            num_scalar_prefetch=1, grid=(bs,),
            in_specs=[pl.BlockSpec(memory_space=pl.ANY), const(w1.shape), const(w2.shape), const(pos.shape)],
            out_specs=[out, out],
            scratch_shapes=[pltpu.VMEM((2, n_pages, 2, KV_W, page), F32),
                            pltpu.VMEM((2, n_pages * page, KV_W), F32), pltpu.SemaphoreType.DMA((2,))]),
        out_shape=[jax.ShapeDtypeStruct((bs, n_seg, KV_W), BF16)] * 2,
        compiler_params=_cparams(("arbitrary",)),
        name="compress_sample",
    )(page_table, cache_t, w1, w2, pos)


_NT = (((1,), (1,)), ((), ()))


def _nsa_sample_kernel(pt_ref, qbd_ref, gl_ref, kc_ref, vc_ref, snew_ref, cwin_ref, wnew_ref, csel_ref,
                       slope_ref, tq_ref, tq8_ref, cover_ref, gsum_ref, gexp_ref, o_ref,
                       buf_ref, s_ref, sem):
    n_pages, page = pt_ref.shape[1], buf_ref.shape[-1]
    past = n_pages * page
    n_seg = kc_ref.shape[0]
    npad = cover_ref.shape[1]
    n_sel = past // SEL_BLOCK + 1
    nr = qbd_ref.shape[0]
    wb = cwin_ref.shape[-1]
    slot, wait = _page_fetch(pt_ref, csel_ref, buf_ref, sem)

    qb = (qbd_ref[...] * SCALE).astype(BF16)
    slope = slope_ref[...]
    qpos = tq_ref[...] + float(past)
    zeros_pad = jnp.zeros((QB - snew_ref.shape[0], KV_W), F32)
    t_new = lax.broadcasted_iota(jnp.int32, (1, QB), 1).astype(F32)

    sc = lax.dot_general(qb, kc_ref[...], _NT, preferred_element_type=F32)
    n_i = lax.broadcasted_iota(jnp.int32, (1, n_seg), 1)
    dcmp = qpos - (n_i * CMP_STRIDE + (CMP_BLOCK - 1)).astype(F32)
    okc = dcmp >= 0.0
    sc = jnp.where(okc, sc - slope * dcmp, NEG)
    mc = jnp.max(sc, axis=-1, keepdims=True)
    pc = jnp.where(okc, jnp.exp(sc - mc), 0.0)
    lc = jnp.sum(pc, axis=-1, keepdims=True)
    pc = pc * (1.0 / jnp.where(lc > 0.0, lc, 1.0))
    o_cmp = jnp.dot(pc.astype(BF16), vc_ref[...], preferred_element_type=F32)

    psum = jnp.dot(gsum_ref[...], pc, precision=HI, preferred_element_type=F32)
    imp = jnp.dot(psum, cover_ref[...], precision=HI, preferred_element_type=F32)
    nq8 = gsum_ref.shape[0]
    jblk = lax.broadcasted_iota(jnp.int32, (nq8, npad), 1)
    qblk = (tq8_ref[...].astype(jnp.int32) + past) // SEL_BLOCK
    back = qblk - jblk
    visible = back >= 0
    forced = (jblk == 0) | (visible & (back < SEL_LOCAL))
    score = jnp.where(forced, 1e9, jnp.where(visible, imp, -1e9))
    rank = jnp.zeros((nq8, npad), F32)
    for jp in range(n_sel):
        col = score[:, jp:jp + 1]
        ge = jnp.where(col >= score, 1.0, 0.0)
        gt = jnp.where(col > score, 1.0, 0.0)
        rank = rank + jnp.where(jblk > jp, ge, gt)
    sel8 = jnp.where(visible, jnp.where(rank < float(min(SEL_TOP, n_sel)), 1.0, 0.0), 0.0)
    negb = (jnp.dot(gexp_ref[...], sel8, precision=HI, preferred_element_type=F32) - 1.0) * (-NEG)

    def tile_bias(jt):
        lane = lax.broadcasted_iota(jnp.int32, (nr, QB), 1)
        return jnp.where(lane < SEL_BLOCK, negb[:, 2 * jt:2 * jt + 1], negb[:, 2 * jt + 1:2 * jt + 2])

    wait()
    ppc = 4
    ck = ppc * page
    tiles_per_page = page // QB

    def chunk_t(c, kv):
        return jnp.concatenate([buf_ref[slot, c * ppc + u, kv] for u in range(ppc)], axis=1).astype(BF16)

    for c in range(n_pages // ppc):
        s = jnp.dot(qb, chunk_t(c, 0), preferred_element_type=F32)
        kpos = (lax.broadcasted_iota(jnp.int32, (1, ck), 1) + c * ck).astype(F32)
        bias = jnp.concatenate([tile_bias(c * ppc * tiles_per_page + u) for u in range(ppc * tiles_per_page)],
                               axis=1)
        s_ref[:, c * ck:(c + 1) * ck] = s - slope * (qpos - kpos) + bias
    knew = jnp.concatenate([snew_ref[:, 0:KV_W], zeros_pad], axis=0).astype(BF16)
    vnew = jnp.concatenate([snew_ref[:, KV_W:2 * KV_W], zeros_pad], axis=0).astype(BF16)
    dnew = tq_ref[...] - t_new
    s = lax.dot_general(qb, knew, _NT, preferred_element_type=F32)
    s_ref[:, past:past + QB] = jnp.where(dnew >= 0.0, s - slope * dnew + tile_bias(past // QB), NEG)
    s_all = s_ref[...]
    ms = jnp.max(s_all, axis=-1, keepdims=True)
    ps = jnp.exp(s_all - ms)
    ls = jnp.sum(ps, axis=-1, keepdims=True)
    psb = ps.astype(BF16)
    o_sel = jnp.dot(psb[:, past:past + QB], vnew, preferred_element_type=F32)
    for c in range(n_pages // ppc):
        o_sel = o_sel + lax.dot_general(psb[:, c * ck:(c + 1) * ck], chunk_t(c, 1), _NT,
                                        preferred_element_type=F32)
    o_sel = o_sel * (1.0 / ls)

    s1 = jnp.dot(qb, cwin_ref[0].astype(BF16), preferred_element_type=F32)
    d1 = float(wb) + tq_ref[...] - lax.broadcasted_iota(jnp.int32, (1, wb), 1).astype(F32)
    ok1 = d1 < float(WINDOW)
    s1 = jnp.where(ok1, s1 - slope * d1, NEG)
    wk = jnp.concatenate([wnew_ref[:, 0:KV_W], zeros_pad], axis=0).astype(BF16)
    wv = jnp.concatenate([wnew_ref[:, KV_W:2 * KV_W], zeros_pad], axis=0).astype(BF16)
    ok2 = dnew >= 0.0
    s2 = jnp.where(ok2, lax.dot_general(qb, wk, _NT, preferred_element_type=F32) - slope * dnew, NEG)
    mw = jnp.maximum(jnp.max(s1, axis=-1, keepdims=True), jnp.max(s2, axis=-1, keepdims=True))
    p1 = jnp.where(ok1, jnp.exp(s1 - mw), 0.0)
    p2 = jnp.where(ok2, jnp.exp(s2 - mw), 0.0)
    lw_ = jnp.sum(p1, axis=-1, keepdims=True) + jnp.sum(p2, axis=-1, keepdims=True)
    o_win = (lax.dot_general(p1.astype(BF16), cwin_ref[1].astype(BF16), _NT, preferred_element_type=F32)
             + jnp.dot(p2.astype(BF16), wv, preferred_element_type=F32)) * (1.0 / lw_)

    sig = _sigmoid(gl_ref[...])
    o_ref[...] = sig[:, 0:1] * o_cmp + sig[:, 1:2] * o_sel + sig[:, 2:3] * o_win


def _nsa_sample(page_table, qbd, gl, kc, vc, snew, cwin, wnew, csel, ts):
    bs, n_pages = page_table.shape
    page = csel.shape[-1]
    past = n_pages * page
    n_seg = kc.shape[1]
    n_sel = past // SEL_BLOCK + 1
    npad = -(-(n_sel + 1) // LANE) * LANE
    nr = KVH * GROUP * ts
    wb = cwin.shape[-1]
    r = np.arange(nr)
    slope = np.exp2(-8.0 * ((r // ts) + 1) / NSA_HEADS).astype(np.float32)[:, None]
    tq = (r % ts).astype(np.float32)[:, None]
    r8 = np.arange(KVH * ts)
    tq8 = (r8 % ts).astype(np.float32)[:, None]
    cover = np.zeros((n_seg, npad), np.float32)
    cover[:, :n_sel] = _cover_matrix(n_seg, n_sel)
    gsum = ((r[None, :] // (GROUP * ts) == r8[:, None] // ts) & (r[None, :] % ts == r8[:, None] % ts)).astype(np.float32)
    consts = [slope, tq, tq8, cover, gsum, gsum.T.copy()]
    const = lambda shape: pl.BlockSpec(shape, lambda i, pt: (0,) * len(shape))
    per = lambda *s: pl.BlockSpec((None,) + s, lambda i, pt: (i,) + (0,) * len(s))
    return pl.pallas_call(
        _nsa_sample_kernel,
        grid_spec=pltpu.PrefetchScalarGridSpec(
            num_scalar_prefetch=1, grid=(bs,),
            in_specs=[per(nr, KV_W), per(nr, LANE), per(n_seg, KV_W), per(n_seg, KV_W), per(SUBLANE, 2 * KV_W),
                      per(2, KV_W, wb), per(SUBLANE, 2 * KV_W), pl.BlockSpec(memory_space=pl.ANY)]
                     + [const(c.shape) for c in consts],
            out_specs=per(nr, KV_W),
            scratch_shapes=[pltpu.VMEM((2, n_pages, 2, KV_W, page), F32), pltpu.VMEM((nr, past + QB), F32),
                            pltpu.SemaphoreType.DMA((2,))]),
        out_shape=jax.ShapeDtypeStruct((bs, nr, KV_W), F32),
        compiler_params=_cparams(("arbitrary",)),
        name="nsa_sample",
    )(page_table, qbd, gl, kc, vc, snew, cwin, wnew, csel, *[jnp.asarray(c) for c in consts])


def _hgrn_sample_kernel(ts, hq_ref, hf_ref, hi_ref, hg_ref, lb_ref, ng_ref, s0_ref, o_ref, s1_ref):
    rows = hq_ref.shape[0]
    q = hq_ref[...]
    v = hi_ref[...]
    log_f, k = _hgrn_gates(hf_ref[...], lb_ref[...])
    tloc = lax.broadcasted_iota(jnp.int32, (rows, HG_W), 0) % ts
    up = lambda a, d: pltpu.roll(a, d, axis=0)
    down = lambda a, d: pltpu.roll(a, rows - d, axis=0)

    bcum = log_f
    for d in range(1, ts):
        bcum = bcum + jnp.where(tloc >= d, up(log_f, d), 0.0)
    b_last = bcum
    for d in range(1, ts):
        b_last = jnp.where(tloc == ts - 1 - d, down(bcum, d), b_last)

    o_intra = [jnp.zeros((rows, HG_D), F32) for _ in range(HG_H)]
    for d in range(ts):
        kd, bd, vd = (k, bcum, v) if d == 0 else (up(k, d), up(bcum, d), up(v, d))
        w = jnp.where(tloc >= d, q * kd * jnp.exp(jnp.where(tloc >= d, bcum - bd, 0.0)), 0.0)
        for h in range(HG_H):
            sl = slice(h * HG_D, (h + 1) * HG_D)
            o_intra[h] = o_intra[h] + jnp.sum(w[:, sl], axis=-1, keepdims=True) * vd[:, sl]

    qb = q * jnp.exp(bcum)
    kh = k * jnp.exp(b_last - bcum)
    per_tile = SUBLANE // ts
    row8 = lax.broadcasted_iota(jnp.int32, (SUBLANE, HG_D), 0) // ts
    tiles = []
    for j in range(rows // SUBLANE):
        r8 = slice(j * SUBLANE, (j + 1) * SUBLANE)
        heads = []
        for h in range(HG_H):
            sl = slice(h * HG_D, (h + 1) * HG_D)
            o_inter = jnp.zeros((SUBLANE, HG_D), F32)
            for u in range(per_tile):
                seq = j * per_tile + u
                mine = row8 == u
                s0 = s0_ref[seq, h]
                o_inter = o_inter + jnp.dot(jnp.where(mine, qb[r8, sl], 0.0).astype(BF16), s0.astype(BF16),
                                            preferred_element_type=F32)
                upd = lax.dot_general(jnp.where(mine, kh[r8, sl], 0.0).astype(BF16), v[r8, sl].astype(BF16),
                                      (((0,), (0,)), ((), ())), preferred_element_type=F32)
                r_last = j * SUBLANE + u * ts + ts - 1
                dec = jnp.exp(bcum[r_last:r_last + 1, sl])
                s1_ref[seq, h] = jnp.broadcast_to(dec, (HG_D, HG_D)).T * s0 + upd
            heads.append(o_inter + o_intra[h][r8, :])
        tiles.append(jnp.concatenate(heads, axis=1))
    o = jnp.concatenate(tiles, axis=0)
    o_ref[...] = _hgrn_out(o, hg_ref[...], ng_ref[...]).astype(BF16)


def _hgrn_sample(z, lb, ng, s0, bs, ts, nb):
    rows = nb * ts
    col = lambda c0: pl.BlockSpec((rows, HG_W), lambda i: (i, c0 // HG_W))
    st = pl.BlockSpec((nb, HG_H, HG_D, HG_D), lambda i: (i, 0, 0, 0))
    return pl.pallas_call(
        functools.partial(_hgrn_sample_kernel, ts),
        grid=(bs // nb,),
        in_specs=[col(C_HQ), col(C_HF), col(C_HI), col(C_HG), _const_spec((1, HG_W)), _const_spec((1, HG_D)), st],
        out_specs=[pl.BlockSpec((rows, HG_W), lambda i: (i, 0)), st],
        out_shape=[jax.ShapeDtypeStruct((bs * ts, HG_W), BF16),
                   jax.ShapeDtypeStruct((bs, HG_H, HG_D, HG_D), F32)],
        compiler_params=_cparams(("parallel",)),
        name="hgrn_sample",
    )(z, z, z, z, lb, ng, s0)


def _layer_sample(x, p, cache_cmp, cache_sel, cache_win, state, page_table, lw, bs, ts):
    n = bs * ts
    assert SUBLANE % ts == 0 and n % SUBLANE == 0
    z = _norm_matmul(x, lw['g_pre_mix'], lw['w_in'], tm=min(512, n), tn=512)
    kv_c = z[:, C_KVC:C_KVC + 2 * KV_W]
    kv_s = z[:, C_KVS:C_KVS + 2 * KV_W]
    kv_w = z[:, C_KVW:C_KVW + 2 * KV_W]

    rows_last = lambda a: jnp.transpose(a, (0, 2, 3, 4, 1)).reshape(a.shape[0], 2, KV_W, a.shape[1])
    kc, vc = _compress_sample(page_table, rows_last(cache_cmp), lw['cmp_w1'], lw['cmp_w2'], lw['cmp_pos'])

    eye = jnp.eye(KVH, dtype=F32)
    q5 = z[:, C_Q:C_Q + NSA_W].reshape(bs, ts, KVH, GROUP, HD).transpose(0, 2, 3, 1, 4)
    qbd = jnp.einsum('bkgtd,kq->bkgtqd', q5, eye).reshape(bs, KVH * GROUP * ts, KV_W)
    g5 = z[:, C_GN:C_GN + 3 * NSA_HEADS].reshape(bs, ts, KVH, GROUP, 3).transpose(0, 2, 3, 1, 4)
    gl = jnp.pad(g5.reshape(bs, KVH * GROUP * ts, 3), ((0, 0), (0, 0), (0, LANE - 3)))
    pad_rows = lambda a: jnp.pad(a.reshape(bs, ts, 2 * KV_W), ((0, 0), (0, SUBLANE - ts), (0, 0)))
    o_rows = _nsa_sample(page_table, qbd, gl, kc, vc, pad_rows(kv_s), rows_last(cache_win), pad_rows(kv_w),
                         rows_last(cache_sel), ts)
    o6 = o_rows.reshape(bs, KVH, GROUP, ts, KVH, HD)
    o_nsa = jnp.stack([o6[:, kvh, :, :, kvh, :] for kvh in range(KVH)], axis=1)
    o_nsa = o_nsa.transpose(0, 3, 1, 2, 4).reshape(n, NSA_W).astype(BF16)

    o_hg, st = _hgrn_sample(z, lw['hg_lb'], lw['hg_norm'], state, bs, ts, nb=min(8, bs))
    h1 = _merge(o_nsa, o_hg, z, x, lw['w_bn'], lw['w_bh'], lw['w_out'], lw['g_post_mix'], tm=min(512, n))
    h2 = _mlp(h1, p, lw['w_up'], lw['w_down'], lw['w_gate'], lw['w_proj'],
              lw['g_pre_mlp'], lw['g_post_mlp'], lw['g_ple'], tm=min(512, n))
    kv6 = lambda a: a.reshape(bs, ts, 2, KVH, HD)
    win_buf = jnp.concatenate([cache_win, kv6(kv_w)], axis=1)[:, ts:]
    return h2, kv6(kv_c), kv6(kv_s), win_buf, st


def kernel(x_prompt, x_sample, cache_cmp_kv, cache_sel_kv, cache_win_kv, state_hgrn, page_table, p_prompt,
           p_sample, w_in, cmp_k_w1, cmp_k_w2, cmp_v_w1, cmp_v_w2, cmp_pos, hg_lb_logits, hg_norm, w_branch_nsa,
           w_branch_hgrn, w_out, norm_pre_mix, norm_post_mix, norm_pre_mlp, norm_post_mlp, w_mlp_up, w_mlp_down,
           norm_ple, w_ple_gate, w_ple_proj):
    depth = w_in.shape[0]
    b, t, d = x_prompt.shape
    bs, ts, _ = x_sample.shape
    lb_all = jnp.cumsum(jax.nn.softmax(hg_lb_logits.astype(F32), axis=0), axis=0)
    h_p = x_prompt.reshape(b * t, d)
    h_s = x_sample.reshape(bs * ts, d)
    outs = [[] for _ in range(8)]
    for i in range(depth):
        lw = _prep_layer(i, lb_all, w_in, cmp_k_w1, cmp_k_w2, cmp_v_w1, cmp_v_w2, cmp_pos, hg_norm, w_branch_nsa,
                         w_branch_hgrn, w_out, norm_pre_mix, norm_post_mix, norm_pre_mlp, norm_post_mlp, w_mlp_up,
                         w_mlp_down, norm_ple, w_ple_gate, w_ple_proj)
        h_p, *res_p = _layer_prompt(h_p, p_prompt[i].reshape(b * t, -1), lw, b, t)
        h_s, *res_s = _layer_sample(h_s, p_sample[i].reshape(bs * ts, -1), cache_cmp_kv[i], cache_sel_kv[i],
                                    cache_win_kv[i], state_hgrn[i], page_table, lw, bs, ts)
        for lst, v in zip(outs, res_p + res_s):
            lst.append(v)
    return (h_p.reshape(b, t, d), h_s.reshape(bs, ts, d)) + tuple(jnp.stack(lst, axis=0) for lst in outs)
```

```python
import functools
import math

import numpy as np
import jax
import jax.numpy as jnp
from jax import lax
from jax.experimental import pallas as pl
from jax.experimental.pallas import tpu as pltpu

F32 = jnp.float32
BF16 = jnp.bfloat16

D_MODEL = 1024
NSA_HEADS = 8
KVH = 2
GROUP = NSA_HEADS // KVH
HD = 64
NSA_W = NSA_HEADS * HD
KV_W = KVH * HD
CMP_BLOCK = 32
CMP_STRIDE = 16
CMP_HIDDEN = 2 * HD
SEL_BLOCK = 64
SEL_TOP = 16
SEL_LOCAL = 2
WINDOW = 512
HG_W = 512
HG_H = 4
HG_D = 128
HG_CHUNK = 64
D_FF = 4 * D_MODEL
PLE_DIM = 256
RMS_EPS = 1e-6
NEG = -1e30
M_INIT = -1e20
SCALE = HD ** -0.5
HG_SAFE_EXP = 60.0

LANE = 128
SUBLANE = 8
VMEM_LIMIT = 48 * 1024 * 1024

C_GM = 0
C_Q = 2048
C_HQ = 2560
C_HF = 3072
C_HI = 3584
C_HG = 4096
C_KVC = 4608
C_KVS = 4864
C_KVW = 5120
C_GN = 5376
Z_COLS = 5632
QB = 128
HI = lax.Precision.HIGHEST


def _cparams(sem, vmem=VMEM_LIMIT):
    return pltpu.CompilerParams(dimension_semantics=sem, vmem_limit_bytes=vmem)


def _rms(x, g):
    return x * lax.rsqrt(jnp.mean(x * x, axis=-1, keepdims=True) + RMS_EPS) * g


def _sigmoid(x):
    return 1.0 / (1.0 + jnp.exp(-x))


def _gelu_tanh(x):
    return 0.5 * x * (1.0 + jnp.tanh(math.sqrt(2.0 / math.pi) * (x + 0.044715 * (x * x * x))))


def _const_spec(shape):
    nd = len(shape)
    return pl.BlockSpec(shape, lambda *_: (0,) * nd)


def _norm_matmul_kernel(x_ref, g_ref, w_ref, o_ref, xn_ref):
    @pl.when(pl.program_id(1) == 0)
    def _():
        xn_ref[...] = _rms(x_ref[...], g_ref[...]).astype(BF16)

    o_ref[...] = jnp.dot(xn_ref[...], w_ref[...], preferred_element_type=F32)


def _norm_matmul(x, g, w, tm, tn):
    n, d = x.shape
    c = w.shape[1]
    return pl.pallas_call(
        _norm_matmul_kernel,
        grid=(n // tm, c // tn),
        in_specs=[pl.BlockSpec((tm, d), lambda i, j: (i, 0)),
                  pl.BlockSpec((1, d), lambda i, j: (0, 0)),
                  pl.BlockSpec((d, tn), lambda i, j: (0, j))],
        out_specs=pl.BlockSpec((tm, tn), lambda i, j: (i, j)),
        out_shape=jax.ShapeDtypeStruct((n, c), F32),
        scratch_shapes=[pltpu.VMEM((tm, d), BF16)],
        compiler_params=_cparams(("parallel", "arbitrary")),
        name="norm_in_proj",
    )(x, g, w)


def _compress_math(load_rows, n_seg, w1_ref, w2_ref, pos_ref, kv):
    x = jnp.concatenate([load_rows(s) for s in range(CMP_STRIDE)], axis=1).astype(BF16)
    h0 = jnp.dot(x, w1_ref[kv, 0], preferred_element_type=F32)
    h1 = jnp.dot(x, w1_ref[kv, 1], preferred_element_type=F32)
    posb = (jnp.dot(pos_ref[0], w1_ref[kv, 0], preferred_element_type=F32)
            + jnp.dot(pos_ref[1], w1_ref[kv, 1], preferred_element_type=F32))
    hid = h0 + pltpu.roll(h1, n_seg - 1, axis=0) + posb[0:1]
    return jnp.dot(_gelu_tanh(hid).astype(BF16), w2_ref[kv], preferred_element_type=F32)


def _compress_prompt_kernel(xk_ref, xv_ref, w1_ref, w2_ref, pos_ref, kc_ref, vc_ref):
    n_seg = xk_ref.shape[0] // CMP_STRIDE
    for kv, x_ref, out_ref in ((0, xk_ref, kc_ref), (1, xv_ref, vc_ref)):
        load = lambda s, x_ref=x_ref: x_ref[pl.ds(s, n_seg, stride=CMP_STRIDE), :]
        out_ref[...] = _compress_math(load, n_seg, w1_ref, w2_ref, pos_ref, kv).astype(BF16)


def _compress_prompt(z, w1, w2, pos, b, t):
    n_seg = t // CMP_STRIDE
    return pl.pallas_call(
        _compress_prompt_kernel,
        grid=(b,),
        in_specs=[pl.BlockSpec((t, KV_W), lambda i: (i, C_KVC // KV_W)),
                  pl.BlockSpec((t, KV_W), lambda i: (i, C_KVC // KV_W + 1)),
                  _const_spec(w1.shape), _const_spec(w2.shape), _const_spec(pos.shape)],
        out_specs=[pl.BlockSpec((None, n_seg, KV_W), lambda i: (i, 0, 0)),
                   pl.BlockSpec((None, n_seg, KV_W), lambda i: (i, 0, 0))],
        out_shape=[jax.ShapeDtypeStruct((b, n_seg, KV_W), BF16),
                   jax.ShapeDtypeStruct((b, n_seg, KV_W), BF16)],
        compiler_params=_cparams(("parallel",)),
        name="compress_prompt",
    )(z, z, w1, w2, pos)


def _nsa_prompt_kernel(q_ref, gn_ref, kc_ref, vct_ref, ks_ref, vst_ref, kw_ref, vwt_ref,
                       bias0_ref, slopes_ref, covert_ref, o_ref,
                       m_ref, l_ref, acc_ref, sel_ref):
    i = pl.program_id(1)
    t0 = i * QB
    n_seg = kc_ref.shape[0]
    n_sel = covert_ref.shape[0]
    nl = NSA_HEADS * QB
    half = GROUP * QB
    slopes = slopes_ref[...]

    q = q_ref[...] * SCALE
    zero = jnp.zeros((HD, QB), F32)
    cols = []
    for kvh in range(KVH):
        for p in range(GROUP // 2):
            c0 = (kvh * (GROUP // 2) + p) * LANE
            blk = q[:, c0:c0 + LANE].T
            for hh in range(2):
                piece = blk[hh * HD:(hh + 1) * HD, :]
                cols.append(jnp.concatenate([piece, zero] if kvh == 0 else [zero, piece], axis=0))
    qbdt = jnp.concatenate(cols, axis=1).astype(BF16)

    lane = lax.broadcasted_iota(jnp.int32, (QB, QB), 1)
    sub = lax.broadcasted_iota(jnp.int32, (QB, QB), 0)
    dl = (lane - sub).astype(F32)

    sc = jnp.dot(kc_ref[...], qbdt, preferred_element_type=F32)
    n_i = lax.broadcasted_iota(jnp.int32, (n_seg, QB), 0)
    t_i = lax.broadcasted_iota(jnp.int32, (n_seg, QB), 1)
    d1 = (t0 + t_i - (n_i * CMP_STRIDE + (CMP_BLOCK - 1))).astype(F32)
    dc = jnp.concatenate([d1] * NSA_HEADS, axis=1)
    okc = dc >= 0.0
    sc = jnp.where(okc, sc - slopes * dc, NEG)
    mc = jnp.max(sc, axis=0, keepdims=True)
    pc = jnp.where(okc, jnp.exp(sc - mc), 0.0)
    lc = jnp.sum(pc, axis=0, keepdims=True)
    pc = pc * (1.0 / jnp.where(lc > 0.0, lc, 1.0))
    pcb = pc.astype(BF16)
    ocmp = [jnp.dot(vct_ref[kvh * HD:(kvh + 1) * HD, :], pcb[:, kvh * half:(kvh + 1) * half],
                    preferred_element_type=F32) for kvh in range(KVH)]

    jblk = lax.broadcasted_iota(jnp.int32, (n_sel, QB), 0)
    qblk = (t0 + lax.broadcasted_iota(jnp.int32, (n_sel, QB), 1)) // SEL_BLOCK
    back = qblk - jblk
    visible = back >= 0
    forced = (jblk == 0) | (visible & (back < SEL_LOCAL))
    for kvh in range(KVH):
        psum = pc[:, kvh * half:kvh * half + QB]
        for g in range(1, GROUP):
            psum = psum + pc[:, kvh * half + g * QB:kvh * half + (g + 1) * QB]
        imp = jnp.dot(covert_ref[...], psum, precision=HI, preferred_element_type=F32)
        score = jnp.where(forced, 1e9, jnp.where(visible, imp, -1e9))
        rank = jnp.zeros((n_sel, QB), F32)
        for jp in range(n_sel):
            row = score[jp:jp + 1, :]
            ge = jnp.where(row >= score, 1.0, 0.0)
            gt = jnp.where(row > score, 1.0, 0.0)
            rank = rank + jnp.where(jblk > jp, ge, gt)
        sel_ref[kvh] = jnp.where(visible, jnp.where(rank < float(min(SEL_TOP, n_sel)), 1.0, 0.0), 0.0)

    def sweep(k_ref, vt_ref, lo, hi, tile_bias):
        m_ref[...] = jnp.full((1, nl), M_INIT, F32)
        l_ref[...] = jnp.zeros((1, nl), F32)
        acc_ref[...] = jnp.zeros((KVH, HD, half), F32)

        def body(jt, carry):
            off = (t0 - jt * QB).astype(F32)
            s = jnp.dot(k_ref[jt], qbdt, preferred_element_type=F32)
            s = s - bias0_ref[...] - slopes * off + tile_bias(jt, off)
            m_old = m_ref[...]
            m_new = jnp.maximum(m_old, jnp.max(s, axis=0, keepdims=True))
            alpha = jnp.exp(m_old - m_new)
            p = jnp.exp(s - m_new)
            l_ref[...] = alpha * l_ref[...] + jnp.sum(p, axis=0, keepdims=True)
            m_ref[...] = m_new
            pb = p.astype(BF16)
            vt = vt_ref[jt]
            for kvh in range(KVH):
                pv = jnp.dot(vt[kvh * HD:(kvh + 1) * HD, :], pb[:, kvh * half:(kvh + 1) * half],
                             preferred_element_type=F32)
                acc_ref[kvh] = alpha[:, kvh * half:(kvh + 1) * half] * acc_ref[kvh] + pv
            return carry

        lax.fori_loop(lo, hi, body, 0)
        inv = 1.0 / l_ref[...]
        return [acc_ref[kvh] * inv[:, kvh * half:(kvh + 1) * half] for kvh in range(KVH)]

    def sel_bias(jt, off):
        causal = dl >= -off
        parts = []
        for kvh in range(KVH):
            rows = sel_ref[kvh, pl.ds(jt * (QB // SEL_BLOCK), QB // SEL_BLOCK), :]
            blockmask = jnp.concatenate(
                [jnp.broadcast_to(rows[r:r + 1, :], (SEL_BLOCK, QB)) for r in range(QB // SEL_BLOCK)], axis=0)
            nb = jnp.where(causal, (blockmask - 1.0) * (-NEG), NEG)
            parts += [nb] * GROUP
        return jnp.concatenate(parts, axis=1)

    def win_bias(jt, off):
        nb = jnp.where(dl >= -off, jnp.where(dl < float(WINDOW) - off, 0.0, NEG), NEG)
        return jnp.concatenate([nb] * NSA_HEADS, axis=1)

    osel = sweep(ks_ref, vst_ref, 0, i + 1, sel_bias)
    owin = sweep(kw_ref, vwt_ref, jnp.maximum(i - WINDOW // QB, 0), i + 1, win_bias)

    gt_ = _sigmoid(gn_ref[...]).T
    for kvh in range(KVH):
        for p in range(GROUP // 2):
            pieces = []
            for hh in range(2):
                g = 2 * p + hh
                h = kvh * GROUP + g
                cs = slice(g * QB, (g + 1) * QB)
                pieces.append(gt_[3 * h:3 * h + 1, :] * ocmp[kvh][:, cs]
                              + gt_[3 * h + 1:3 * h + 2, :] * osel[kvh][:, cs]
                              + gt_[3 * h + 2:3 * h + 3, :] * owin[kvh][:, cs])
            c0 = (kvh * (GROUP // 2) + p) * LANE
            o_ref[:, c0:c0 + LANE] = jnp.concatenate(pieces, axis=0).T.astype(BF16)


PB = 256
N_AUG = 16
LOG2E = math.log2(math.e)


def _split3(x):
    hi = x.astype(BF16).astype(F32)
    r = x - hi
    mid = r.astype(BF16).astype(F32)
    return hi, mid, r - mid


def _nsa_prompt256_kernel(q_ref, gn_ref, kc_ref, vct_ref, zks_ref, zvs_ref, zkw_ref, zvw_ref,
                          augc_ref, slopes_ref, covert_ref, o_ref,
                          qa_ref, m_ref, l_ref, acc_ref, sel_ref, score_ref, ks_ref, vst_ref, kw_ref, vwt_ref,
                          sa_ref, sb_ref):
    i = pl.program_id(1)
    t0 = i * PB

    @pl.when(i == 0)
    def _():
        for zk, zv, k_dst, vt_dst in ((zks_ref, zvs_ref, ks_ref, vst_ref), (zkw_ref, zvw_ref, kw_ref, vwt_ref)):
            for j in range(k_dst.shape[0]):
                k_dst[j] = zk[j * PB:(j + 1) * PB, :].astype(BF16)
                vt_dst[j] = zv[j * PB:(j + 1) * PB, :].T.astype(BF16)
    n_seg = kc_ref.shape[0]
    n_sel = covert_ref.shape[0]
    nl = NSA_HEADS * PB
    half = GROUP * PB
    slopes2 = slopes_ref[...] * LOG2E
    tlane = (lax.broadcasted_iota(jnp.int32, (1, nl), 1) % PB).astype(F32)
    blocks_per_tile = PB // SEL_BLOCK

    q = q_ref[...] * (SCALE * LOG2E)
    zero = jnp.zeros((HD, PB), F32)
    cols = []
    for kvh in range(KVH):
        for p in range(GROUP // 2):
            c0 = (kvh * (GROUP // 2) + p) * LANE
            blk = q[:, c0:c0 + LANE].T
            for hh in range(2):
                piece = blk[hh * HD:(hh + 1) * HD, :]
                cols.append(jnp.concatenate([piece, zero] if kvh == 0 else [zero, piece], axis=0))
    qbdt = jnp.concatenate(cols, axis=1).astype(BF16)
    qa_ref[0:KV_W, :] = qbdt
    qa_ref[KV_W + N_AUG:2 * KV_W, :] = jnp.zeros((KV_W - N_AUG, nl), BF16)

    lane = lax.broadcasted_iota(jnp.int32, (PB, PB), 1)
    sub = lax.broadcasted_iota(jnp.int32, (PB, PB), 0)
    causal_bias = jnp.where(lane >= sub, 0.0, NEG)
    edge_bias = jnp.where(lane < sub, 0.0, NEG)

    sc = jnp.dot(kc_ref[...], qbdt, preferred_element_type=F32)
    n_i = lax.broadcasted_iota(jnp.int32, (n_seg, PB), 0)
    t_i = lax.broadcasted_iota(jnp.int32, (n_seg, PB), 1)
    d1 = (t0 + t_i - (n_i * CMP_STRIDE + (CMP_BLOCK - 1))).astype(F32)
    dc = jnp.concatenate([d1] * NSA_HEADS, axis=1)
    okc = dc >= 0.0
    sc = jnp.where(okc, sc - slopes2 * dc, NEG)
    mc = jnp.max(sc, axis=0, keepdims=True)
    pc = jnp.where(okc, jnp.exp2(sc - mc), 0.0)
    lc = jnp.sum(pc, axis=0, keepdims=True)
    pc = pc * (1.0 / jnp.where(lc > 0.0, lc, 1.0))
    pcb = pc.astype(BF16)
    ocmp = [jnp.dot(vct_ref[kvh * HD:(kvh + 1) * HD, :], pcb[:, kvh * half:(kvh + 1) * half],
                    preferred_element_type=F32) for kvh in range(KVH)]

    jblk = lax.broadcasted_iota(jnp.int32, (n_sel, PB), 0)
    qblk = (t0 + lax.broadcasted_iota(jnp.int32, (n_sel, PB), 1)) // SEL_BLOCK
    back = qblk - jblk
    visible = back >= 0
    forced = (jblk == 0) | (visible & (back < SEL_LOCAL))
    n_groups = jnp.minimum((t0 + PB - 1) // SEL_BLOCK // SUBLANE + 1, n_sel // SUBLANE)
    for kvh in range(KVH):
        psum = pc[:, kvh * half:kvh * half + PB]
        for g in range(1, GROUP):
            psum = psum + pc[:, kvh * half + g * PB:kvh * half + (g + 1) * PB]
        imp = jnp.dot(covert_ref[...], psum, precision=HI, preferred_element_type=F32)
        score = jnp.where(forced, 1e9, jnp.where(visible, imp, -1e9))
        score_ref[...] = score

        def rank_group(gi, rank):
            rows8 = score_ref[pl.ds(pl.multiple_of(gi * SUBLANE, SUBLANE), SUBLANE), :]
            for u in range(SUBLANE):
                row = rows8[u:u + 1, :]
                ge = jnp.where(row >= score, 1.0, 0.0)
                gt = jnp.where(row > score, 1.0, 0.0)
                rank = rank + jnp.where(jblk > gi * SUBLANE + u, ge, gt)
            return rank

        rank = lax.fori_loop(0, n_groups, rank_group, jnp.zeros((n_sel, PB), F32))
        sel_ref[kvh] = jnp.where(visible, jnp.where(rank < float(min(SEL_TOP, n_sel)), 0.0, NEG), NEG)

    sl3 = _split3(slopes2)

    def scores(k_ref, jt, use_sel, dst_ref):
        off = (t0 - jt * PB).astype(F32)
        c3 = _split3(-slopes2 * (tlane + off))
        rows = list(sl3) + list(c3)
        if use_sel:
            tiles_per_group = SUBLANE // blocks_per_tile
            base = pl.multiple_of((jt // tiles_per_group) * SUBLANE, SUBLANE)
            which = jt % tiles_per_group
            per_kvh = []
            for kvh in range(KVH):
                rows8 = sel_ref[kvh, pl.ds(base, SUBLANE), :]
                mine = rows8[0:blocks_per_tile, :]
                for w in range(1, tiles_per_group):
                    mine = jnp.where(which == w, rows8[w * blocks_per_tile:(w + 1) * blocks_per_tile, :], mine)
                per_kvh.append(mine)
            rows.append(jnp.concatenate([per_kvh[kvh] for kvh in range(KVH) for _ in range(GROUP)], axis=1))
        rows.append(jnp.zeros((N_AUG - sum(r.shape[0] for r in rows), nl), F32))
        qa_ref[KV_W:KV_W + N_AUG, :] = jnp.concatenate(rows, axis=0).astype(BF16)
        ka = jnp.concatenate([k_ref[jt], augc_ref[...]], axis=1)
        s = jnp.dot(ka, qa_ref[...], preferred_element_type=F32)
        dst_ref[0:PB, :] = s
        dst_ref[PB:PB + 1, :] = jnp.max(s, axis=0, keepdims=True)

    def softmax_pv(vt_ref, jt, src_ref, extra):
        s = src_ref[0:PB, :]
        if extra is not None:
            s = s + jnp.concatenate([extra] * NSA_HEADS, axis=1)
            s_max = jnp.max(s, axis=0, keepdims=True)
        else:
            s_max = src_ref[PB:PB + 1, :]
        m_old = m_ref[...]
        m_new = jnp.maximum(m_old, s_max)
        alpha = jnp.exp2(m_old - m_new)
        p = jnp.exp2(s - m_new)
        l_ref[...] = alpha * l_ref[...] + jnp.sum(p, axis=0, keepdims=True)
        m_ref[...] = m_new
        pb = p.astype(BF16)
        vt = vt_ref[jt]
        for kvh in range(KVH):
            pv = jnp.dot(vt[kvh * HD:(kvh + 1) * HD, :], pb[:, kvh * half:(kvh + 1) * half],
                         preferred_element_type=F32)
            acc_ref[kvh] = alpha[:, kvh * half:(kvh + 1) * half] * acc_ref[kvh] + pv

    def reset():
        m_ref[...] = jnp.full((1, nl), M_INIT, F32)
        l_ref[...] = jnp.zeros((1, nl), F32)
        acc_ref[...] = jnp.zeros((KVH, HD, half), F32)

    def result():
        inv = 1.0 / l_ref[...]
        return [acc_ref[kvh] * inv[:, kvh * half:(kvh + 1) * half] for kvh in range(KVH)]

    def step(k_ref, vt_ref, jt, use_sel, extra):
        scores(k_ref, jt, use_sel, sa_ref)
        softmax_pv(vt_ref, jt, sa_ref, extra)

    reset()
    n_pairs = (i + 2) // 2
    last = pl.num_programs(1) - 1
    odd_i = (i % 2) == 1
    scores(ks_ref, 0, True, sa_ref)

    def pair(u, carry):
        scores(ks_ref, 2 * u + 1, True, sb_ref)
        softmax_pv(vst_ref, 2 * u, sa_ref, None)
        scores(ks_ref, 2 * u + 2, True, sa_ref)
        softmax_pv(vst_ref, 2 * u + 1, sb_ref, None)
        return carry

    lax.fori_loop(0, n_pairs - 1, pair, 0)
    j0 = 2 * n_pairs - 2
    j1 = jnp.minimum(j0 + 1, last)
    scores(ks_ref, j1, True, sb_ref)
    softmax_pv(vst_ref, j0, sa_ref, jnp.where(odd_i, 0.0, causal_bias))
    softmax_pv(vst_ref, j1, sb_ref, jnp.where(odd_i, causal_bias, NEG))
    osel = result()

    reset()

    @pl.when(i >= WINDOW // PB)
    def _():
        step(kw_ref, vwt_ref, i - WINDOW // PB, False, edge_bias)

    @pl.when(i >= 1)
    def _():
        step(kw_ref, vwt_ref, i - 1, False, None)

    step(kw_ref, vwt_ref, i, False, causal_bias)
    owin = result()

    gt_ = _sigmoid(gn_ref[...]).T
    for kvh in range(KVH):
        for p in range(GROUP // 2):
            pieces = []
            for hh in range(2):
                g = 2 * p + hh
                h = kvh * GROUP + g
                cs = slice(g * PB, (g + 1) * PB)
                pieces.append(gt_[3 * h:3 * h + 1, :] * ocmp[kvh][:, cs]
                              + gt_[3 * h + 1:3 * h + 2, :] * osel[kvh][:, cs]
                              + gt_[3 * h + 2:3 * h + 3, :] * owin[kvh][:, cs])
            c0 = (kvh * (GROUP // 2) + p) * LANE
            o_ref[:, c0:c0 + LANE] = jnp.concatenate(pieces, axis=0).T.astype(BF16)


def _aug_key_columns():
    a = np.zeros((PB, KV_W), np.float32)
    s = np.arange(PB)
    a[:, 0:3] = s[:, None]
    a[:, 3:6] = 1.0
    for r in range(PB // SEL_BLOCK):
        a[:, 6 + r] = (s // SEL_BLOCK == r)
    return a


def _nsa_prompt256(z, kc, vct, b, t):
    nq = t // PB
    n_seg = t // CMP_STRIDE
    n_sel = t // SEL_BLOCK
    nl = NSA_HEADS * PB
    assert 6 + PB // SEL_BLOCK <= N_AUG and n_sel % SUBLANE == 0 and WINDOW == 2 * PB
    slopes = _alibi_slopes_lanes(PB)
    covert = _cover_matrix(n_seg, n_sel).T.copy()
    kv_col = lambda c0: pl.BlockSpec((t, KV_W), lambda bi, i: (bi, c0 // KV_W))
    return pl.pallas_call(
        _nsa_prompt256_kernel,
        grid=(b, nq),
        in_specs=[pl.BlockSpec((PB, NSA_W), lambda bi, i: (bi * nq + i, C_Q // NSA_W)),
                  pl.BlockSpec((PB, LANE), lambda bi, i: (bi * nq + i, C_GN // LANE)),
                  pl.BlockSpec((None, n_seg, KV_W), lambda bi, i: (bi, 0, 0)),
                  pl.BlockSpec((None, KV_W, n_seg), lambda bi, i: (bi, 0, 0)),
                  kv_col(C_KVS), kv_col(C_KVS + KV_W), kv_col(C_KVW), kv_col(C_KVW + KV_W),
                  _const_spec((PB, KV_W)), _const_spec((1, nl)), _const_spec((n_sel, n_seg))],
        out_specs=pl.BlockSpec((PB, NSA_W), lambda bi, i: (bi * nq + i, 0)),
        out_shape=jax.ShapeDtypeStruct((b * t, NSA_W), BF16),
        scratch_shapes=[pltpu.VMEM((2 * KV_W, nl), BF16), pltpu.VMEM((1, nl), F32), pltpu.VMEM((1, nl), F32),
                        pltpu.VMEM((KVH, HD, GROUP * PB), F32), pltpu.VMEM((KVH, n_sel, PB), F32),
                        pltpu.VMEM((n_sel, PB), F32),
                        pltpu.VMEM((nq, PB, KV_W), BF16), pltpu.VMEM((nq, KV_W, PB), BF16),
                        pltpu.VMEM((nq, PB, KV_W), BF16), pltpu.VMEM((nq, KV_W, PB), BF16),
                        pltpu.VMEM((PB + SUBLANE, nl), F32), pltpu.VMEM((PB + SUBLANE, nl), F32)],
        compiler_params=_cparams(("parallel", "arbitrary")),
        name="nsa_prompt",
    )(z, z, kc, vct, z, z, z, z, jnp.asarray(_aug_key_columns(), BF16), jnp.asarray(slopes),
      jnp.asarray(covert))


def _alibi_slopes_lanes(width):
    h = np.arange(1, NSA_HEADS + 1, dtype=np.float32)
    return np.repeat(np.exp2(-8.0 * h / NSA_HEADS), width)[None, :].astype(np.float32)


def _cover_matrix(n_cmp_rows, n_sel_rows):
    n = np.arange(n_cmp_rows)[:, None] * CMP_STRIDE
    j = np.arange(n_sel_rows)[None, :] * SEL_BLOCK
    return ((n < j + SEL_BLOCK) & (n + CMP_BLOCK > j)).astype(np.float32)


def _nsa_prompt(z, kc, vct, ks, vst, kw, vwt, b, t):
    nq = t // QB
    n_seg = t // CMP_STRIDE
    n_sel = t // SEL_BLOCK
    nl = NSA_HEADS * QB
    slopes = _alibi_slopes_lanes(QB)
    dl = (np.arange(QB)[None, :] - np.arange(QB)[:, None]).astype(np.float32)
    bias0 = np.tile(dl, (1, NSA_HEADS)) * slopes
    covert = _cover_matrix(n_seg, n_sel).T.copy()
    tile4 = lambda: pl.BlockSpec((None, nq, QB, QB), lambda bi, i: (bi, 0, 0, 0))
    return pl.pallas_call(
        _nsa_prompt_kernel,
        grid=(b, nq),
        in_specs=[pl.BlockSpec((QB, NSA_W), lambda bi, i: (bi * nq + i, C_Q // NSA_W)),
                  pl.BlockSpec((QB, LANE), lambda bi, i: (bi * nq + i, C_GN // LANE)),
                  pl.BlockSpec((None, n_seg, KV_W), lambda bi, i: (bi, 0, 0)),
                  pl.BlockSpec((None, KV_W, n_seg), lambda bi, i: (bi, 0, 0)),
                  tile4(), tile4(), tile4(), tile4(),
                  _const_spec((QB, nl)), _const_spec((1, nl)), _const_spec((n_sel, n_seg))],
        out_specs=pl.BlockSpec((QB, NSA_W), lambda bi, i: (bi * nq + i, 0)),
        out_shape=jax.ShapeDtypeStruct((b * t, NSA_W), BF16),
        scratch_shapes=[pltpu.VMEM((1, nl), F32), pltpu.VMEM((1, nl), F32),
                        pltpu.VMEM((KVH, HD, GROUP * QB), F32), pltpu.VMEM((KVH, n_sel, QB), F32)],
        compiler_params=_cparams(("parallel", "arbitrary")),
        name="nsa_prompt",
    )(z, z, kc, vct, ks, vst, kw, vwt, jnp.asarray(bias0), jnp.asarray(slopes), jnp.asarray(covert))


def _hgrn_gates(pre, lb):
    log_f = jnp.log(lb + (1.0 - lb) * _sigmoid(pre))
    k = (1.0 - lb) * _sigmoid(-pre)
    return log_f, k


def _hgrn_out(o, gate, ng):
    outs = []
    for h in range(HG_H):
        sl = slice(h * HG_D, (h + 1) * HG_D)
        g = gate[:, sl]
        outs.append(_rms(o[:, sl], ng) * (g * _sigmoid(g)))
    return jnp.concatenate(outs, axis=1)


def _hgrn_prompt_kernel(hq_ref, hf_ref, hi_ref, hg_ref, lb_ref, ng_ref, tri_ref, o_ref, st_ref,
                        s_ref, oraw_ref):
    ci = pl.program_id(1)
    tc = hq_ref.shape[0]
    c = HG_CHUNK

    @pl.when(ci == 0)
    def _():
        s_ref[...] = jnp.zeros(s_ref.shape, F32)

    lb = lb_ref[...]
    tril = (lax.broadcasted_iota(jnp.int32, (c, c), 0) >= lax.broadcasted_iota(jnp.int32, (c, c), 1))
    row8 = lax.broadcasted_iota(jnp.int32, (SUBLANE, HG_D), 0)

    n_chunks = tc // c
    q_all = hq_ref[...]
    v_all = hi_ref[...]
    log_f, k_all = _hgrn_gates(hf_ref[...], lb)
    chunks = []
    worst = None
    for g in range(n_chunks):
        rows = slice(g * c, (g + 1) * c)
        bcum = jnp.dot(tri_ref[...], log_f[rows], precision=HI, preferred_element_type=F32)
        e = bcum - bcum[c // 2 - 1:c // 2, :]
        chunks.append((rows, bcum, e))
        worst = jnp.abs(e) if worst is None else jnp.maximum(worst, jnp.abs(e))
    safe = jnp.max(worst) < HG_SAFE_EXP

    @pl.when(safe)
    def _():
        for rows, bcum, e in chunks:
            q, k, v = q_all[rows], k_all[rows], v_all[rows]
            b_last = bcum[c - 1:c, :]
            qt = (q * jnp.exp(e)).astype(BF16)
            kt = (k * jnp.exp(-e)).astype(BF16)
            qb = (q * jnp.exp(bcum)).astype(BF16)
            kh = (k * jnp.exp(b_last - bcum)).astype(BF16)
            dec = jnp.exp(b_last)
            vb = v.astype(BF16)
            for h in range(HG_H):
                sl = slice(h * HG_D, (h + 1) * HG_D)
                a = lax.dot_general(qt[:, sl], kt[:, sl], (((1,), (1,)), ((), ())), preferred_element_type=F32)
                a = jnp.where(tril, a, 0.0).astype(BF16)
                st = s_ref[h]
                o = (lax.dot_general(qb[:, sl], st.astype(BF16), (((1,), (1,)), ((), ())),
                                     preferred_element_type=F32)
                     + jnp.dot(a, vb[:, sl], preferred_element_type=F32))
                oraw_ref[rows, sl] = o
                s_ref[h] = st * dec[:, sl] + lax.dot_general(vb[:, sl], kh[:, sl], (((0,), (0,)), ((), ())),
                                                             preferred_element_type=F32)

    @pl.when(jnp.logical_not(safe))
    def _():
        for h in range(HG_H):
            sl = slice(h * HG_D, (h + 1) * HG_D)

            def tile(ti, carry, sl=sl, h=h):
                r = pl.multiple_of(ti * SUBLANE, SUBLANE)
                q8 = hq_ref[pl.ds(r, SUBLANE), sl]
                v8 = hi_ref[pl.ds(r, SUBLANE), sl]
                lf8, k8 = _hgrn_gates(hf_ref[pl.ds(r, SUBLANE), sl], lb[:, sl])
                f8 = jnp.exp(lf8)
                st = s_ref[h]
                rows_out = []
                for u in range(SUBLANE):
                    vu = jnp.where(row8 == 0, jnp.broadcast_to(v8[u:u + 1, :], (SUBLANE, HG_D)), 0.0)
                    ku = jnp.broadcast_to(k8[u:u + 1, :], (SUBLANE, HG_D))
                    qu = jnp.broadcast_to(q8[u:u + 1, :], (SUBLANE, HG_D))
                    st = st * f8[u:u + 1, :] + lax.dot_general(
                        vu, ku, (((0,), (0,)), ((), ())), precision=HI, preferred_element_type=F32)
                    ou = lax.dot_general(qu, st, (((1,), (1,)), ((), ())), precision=HI,
                                         preferred_element_type=F32)
                    rows_out.append(ou[0:1, :])
                s_ref[h] = st
                oraw_ref[pl.ds(r, SUBLANE), sl] = jnp.concatenate(rows_out, axis=0)
                return carry

            lax.fori_loop(0, tc // SUBLANE, tile, 0)

    o_ref[...] = _hgrn_out(oraw_ref[...], hg_ref[...], ng_ref[...]).astype(BF16)

    @pl.when(ci == pl.num_programs(1) - 1)
    def _():
        for h in range(HG_H):
            st_ref[h] = s_ref[h].T


def _hgrn_prompt(z, lb, ng, b, t, tc):
    nc = t // tc
    col = lambda c0: pl.BlockSpec((tc, HG_W), lambda bi, ci: (bi * nc + ci, c0 // HG_W))
    tri = np.tril(np.ones((HG_CHUNK, HG_CHUNK), np.float32))
    return pl.pallas_call(
        _hgrn_prompt_kernel,
        grid=(b, nc),
        in_specs=[col(C_HQ), col(C_HF), col(C_HI), col(C_HG),
                  _const_spec((1, HG_W)), _const_spec((1, HG_D)), _const_spec((HG_CHUNK, HG_CHUNK))],
        out_specs=[pl.BlockSpec((tc, HG_W), lambda bi, ci: (bi * nc + ci, 0)),
                   pl.BlockSpec((None, HG_H, HG_D, HG_D), lambda bi, ci: (bi, 0, 0, 0))],
        out_shape=[jax.ShapeDtypeStruct((b * t, HG_W), BF16),
                   jax.ShapeDtypeStruct((b, HG_H, HG_D, HG_D), F32)],
        scratch_shapes=[pltpu.VMEM((HG_H, HG_D, HG_D), F32), pltpu.VMEM((tc, HG_W), F32)],
        compiler_params=_cparams(("parallel", "arbitrary")),
        name="hgrn_prompt",
    )(z, z, z, z, lb, ng, jnp.asarray(tri))


def _merge_kernel(on_ref, oh_ref, ga_ref, gb_ref, x_ref, wn_ref, wh_ref, wo_ref, g_ref, o_ref):
    y = (_sigmoid(ga_ref[...]) * jnp.dot(on_ref[...], wn_ref[...], preferred_element_type=F32)
         + _sigmoid(gb_ref[...]) * jnp.dot(oh_ref[...], wh_ref[...], preferred_element_type=F32))
    mix = jnp.dot(y.astype(BF16), wo_ref[...], preferred_element_type=F32)
    o_ref[...] = x_ref[...] + _rms(mix, g_ref[...])


def _resident(shape):
    nd = len(shape)
    return pl.BlockSpec(shape, lambda *_: (0,) * nd, pipeline_mode=pl.Buffered(1))


def _merge(o_nsa, o_hg, z, x, wn, wh, wo, g, tm):
    n = x.shape[0]
    row = lambda w, cb: pl.BlockSpec((tm, w), lambda i: (i, cb))
    return pl.pallas_call(
        _merge_kernel,
        grid=(n // tm,),
        in_specs=[row(NSA_W, 0), row(HG_W, 0), row(D_MODEL, C_GM // D_MODEL), row(D_MODEL, C_GM // D_MODEL + 1),
                  row(D_MODEL, 0), _resident(wn.shape), _resident(wh.shape), _resident(wo.shape),
                  _resident((1, D_MODEL))],
        out_specs=row(D_MODEL, 0),
        out_shape=jax.ShapeDtypeStruct((n, D_MODEL), F32),
        compiler_params=_cparams(("parallel",)),
        name="merge_out_proj",
    )(o_nsa, o_hg, z, z, x, wn, wh, wo, g)


def _mlp_kernel(h_ref, p_ref, wu_ref, wd_ref, wg_ref, wp_ref, g1_ref, g2_ref, g3_ref, o_ref):
    h = h_ref[...]
    xn = _rms(h, g1_ref[...]).astype(BF16)
    ffn = jnp.zeros(h.shape, F32)
    step = D_MODEL
    for c0 in range(0, D_FF, step):
        up = jnp.dot(xn, wu_ref[:, c0:c0 + step], preferred_element_type=F32)
        act = jnp.square(jnp.maximum(up, 0.0)).astype(BF16)
        ffn = ffn + jnp.dot(act, wd_ref[c0:c0 + step, :], preferred_element_type=F32)
    h = h + _rms(ffn, g2_ref[...])
    gate = _sigmoid(jnp.dot(_rms(h, g3_ref[...]).astype(BF16), wg_ref[...], preferred_element_type=F32))
    o_ref[...] = h + gate * jnp.dot(p_ref[...].astype(BF16), wp_ref[...], preferred_element_type=F32)


def _mlp(h, p, wu, wd, wg, wp, g1, g2, g3, tm):
    n = h.shape[0]
    row = lambda w: pl.BlockSpec((tm, w), lambda i: (i, 0))
    gain = _resident((1, D_MODEL))
    return pl.pallas_call(
        _mlp_kernel,
        grid=(n // tm,),
        in_specs=[row(D_MODEL), row(PLE_DIM), _resident(wu.shape), _resident(wd.shape), _resident(wg.shape),
                  _resident(wp.shape), gain, gain, gain],
        out_specs=row(D_MODEL),
        out_shape=jax.ShapeDtypeStruct((n, D_MODEL), F32),
        compiler_params=_cparams(("parallel",)),
        name="mlp_ple",
    )(h, p, wu, wd, wg, wp, g1, g2, g3)


def _prep_w_in(w):
    sizes = (NSA_W, 2 * KV_W, 2 * KV_W, 2 * KV_W, 3 * NSA_HEADS, HG_W, HG_W, HG_W, HG_W)
    q, kvc, kvs, kvw, gn, hq, hf, hi, hg, gm = jnp.split(w, [int(v) for v in np.cumsum(sizes)], axis=1)
    pad = jnp.zeros((w.shape[0], Z_COLS - C_GN - 3 * NSA_HEADS), w.dtype)
    return jnp.concatenate([gm, q, hq, hf, hi, hg, kvc, kvs, kvw, gn, pad], axis=1).astype(BF16)


def _prep_compress(w1k, w2k, w1v, w2v, pos):
    eye = jnp.eye(KVH, dtype=F32)

    def big1(w1):
        t = jnp.einsum('rsdh,kq->rskdqh', w1, eye)
        return t.reshape(CMP_BLOCK // CMP_STRIDE, CMP_STRIDE * KV_W, KVH * CMP_HIDDEN)

    def big2(w2):
        return jnp.einsum('hd,kq->khqd', w2, eye).reshape(KVH * CMP_HIDDEN, KV_W)

    w1 = jnp.stack([big1(w1k), big1(w1v)]).astype(BF16)
    w2 = jnp.stack([big2(w2k), big2(w2v)]).astype(BF16)
    posb = jnp.broadcast_to(pos[:, :, None, :], pos.shape[:2] + (KVH, HD)).reshape(pos.shape[0], 1, -1)
    posb = jnp.broadcast_to(posb, (pos.shape[0], SUBLANE, posb.shape[-1])).astype(BF16)
    return w1, w2, posb


def _kv_rows_t_kernel(ck_ref, cv_ref, sk_ref, sv_ref, wk_ref, wv_ref, ct_ref, st_ref, wt_ref):
    rows = ck_ref.shape[0]
    for src, dst, part in ((ck_ref, ct_ref, 0), (cv_ref, ct_ref, 1), (sk_ref, st_ref, 0), (sv_ref, st_ref, 1),
                           (wk_ref, wt_ref, 0), (wv_ref, wt_ref, 1)):
        for j in range(rows // LANE):
            dst[part * KV_W:(part + 1) * KV_W, j * LANE:(j + 1) * LANE] = src[j * LANE:(j + 1) * LANE, :].T


def _kv_rows_t(z, b, t, rows):
    nt = t // rows
    col = lambda c0: pl.BlockSpec((rows, KV_W), lambda bi, j: (bi * nt + j, c0 // KV_W))
    full = pl.BlockSpec((None, 2 * KV_W, rows), lambda bi, j: (bi, 0, j))
    tail = pl.BlockSpec((None, 2 * KV_W, rows), lambda bi, j: (bi, 0, 0))
    return pl.pallas_call(
        _kv_rows_t_kernel,
        grid=(b, nt),
        in_specs=[col(C_KVC), col(C_KVC + KV_W), col(C_KVS), col(C_KVS + KV_W), col(C_KVW), col(C_KVW + KV_W)],
        out_specs=[full, full, tail],
        out_shape=[jax.ShapeDtypeStruct((b, 2 * KV_W, t), F32), jax.ShapeDtypeStruct((b, 2 * KV_W, t), F32),
                   jax.ShapeDtypeStruct((b, 2 * KV_W, rows), F32)],
        compiler_params=_cparams(("parallel", "arbitrary")),
        name="kv_rows_t",
    )(z, z, z, z, z, z)


def _prep_layer(i, lb_all, w_in, cmp_k_w1, cmp_k_w2, cmp_v_w1, cmp_v_w2, cmp_pos, hg_norm, w_branch_nsa,
                w_branch_hgrn, w_out, norm_pre_mix, norm_post_mix, norm_pre_mlp, norm_post_mlp, w_mlp_up,
                w_mlp_down, norm_ple, w_ple_gate, w_ple_proj):
    w1, w2, posb = _prep_compress(cmp_k_w1[i], cmp_k_w2[i], cmp_v_w1[i], cmp_v_w2[i], cmp_pos[i])
    row = lambda a: a[i].reshape(1, -1).astype(F32)
    return {
        'w_in': _prep_w_in(w_in[i]), 'cmp_w1': w1, 'cmp_w2': w2, 'cmp_pos': posb,
        'hg_lb': lb_all[i].reshape(1, HG_W), 'hg_norm': row(hg_norm),
        'w_bn': w_branch_nsa[i].astype(BF16), 'w_bh': w_branch_hgrn[i].astype(BF16), 'w_out': w_out[i].astype(BF16),
        'w_up': w_mlp_up[i].astype(BF16), 'w_down': w_mlp_down[i].astype(BF16),
        'w_gate': w_ple_gate[i].astype(BF16), 'w_proj': w_ple_proj[i].astype(BF16),
        'g_pre_mix': row(norm_pre_mix), 'g_post_mix': row(norm_post_mix), 'g_pre_mlp': row(norm_pre_mlp),
        'g_post_mlp': row(norm_post_mlp), 'g_ple': row(norm_ple),
    }


def _key_tiles(rows, b, t):
    r = rows.reshape(b, t // PB, PB, 2 * KV_W)
    return r[..., :KV_W].astype(BF16), jnp.swapaxes(r[..., KV_W:], 2, 3).astype(BF16)


def _layer_prompt(x, p, lw, b, t):
    n = b * t
    z = _norm_matmul(x, lw['g_pre_mix'], lw['w_in'], tm=min(1024, n), tn=Z_COLS // 4)
    wb = min(WINDOW, t)
    kv_c, kv_s, kv_w = _kv_rows_t(z, b, t, wb)
    kc, vc = _compress_prompt(z, lw['cmp_w1'], lw['cmp_w2'], lw['cmp_pos'], b, t)
    vct = jnp.swapaxes(vc, 1, 2)
    o_nsa = _nsa_prompt256(z, kc, vct, b, t)
    o_hg, st = _hgrn_prompt(z, lw['hg_lb'], lw['hg_norm'], b, t, tc=min(256, t))
    h1 = _merge(o_nsa, o_hg, z, x, lw['w_bn'], lw['w_bh'], lw['w_out'], lw['g_post_mix'], tm=min(512, n))
    h2 = _mlp(h1, p, lw['w_up'], lw['w_down'], lw['w_gate'], lw['w_proj'],
              lw['g_pre_mlp'], lw['g_post_mlp'], lw['g_ple'], tm=min(512, n))
    kv6 = lambda a: a.reshape(b, 2, KVH, HD, a.shape[-1]).transpose(0, 4, 1, 2, 3)
    return h2, kv6(kv_c), kv6(kv_s), kv6(kv_w), st


SEG_PITCH = 24


def _page_fetch(pt_ref, cache_ref, buf_ref, sem):
    b = pl.program_id(0)
    n_pages = pt_ref.shape[1]
    slot = b % 2

    def copy(seq, sl, p):
        return pltpu.make_async_copy(cache_ref.at[pt_ref[seq, p]], buf_ref.at[sl, p], sem.at[sl])

    def start(seq, sl):
        lax.fori_loop(0, n_pages, lambda p, c: (copy(seq, sl, p).start(), c)[1], 0)

    @pl.when(b == 0)
    def _():
        start(0, 0)

    @pl.when(b + 1 < pl.num_programs(0))
    def _():
        start(b + 1, 1 - slot)

    def wait():
        lax.fori_loop(0, n_pages, lambda p, c: (copy(b, slot, p).wait(), c)[1], 0)

    return slot, wait


def _compress_sample_kernel(pt_ref, cache_ref, w1_ref, w2_ref, pos_ref, kc_ref, vc_ref, buf_ref, rows_ref, sem):
    n_pages, page = pt_ref.shape[1], buf_ref.shape[-1]
    n_seg = n_pages * page // CMP_STRIDE
    slot, wait = _page_fetch(pt_ref, cache_ref, buf_ref, sem)
    wait()
    segs_per_page = page // CMP_STRIDE
    for kv, out_ref in ((0, kc_ref), (1, vc_ref)):
        for p in range(n_pages):
            rows = buf_ref[slot, p, kv].T
            for g in range(segs_per_page):
                r0 = (p * segs_per_page + g) * SEG_PITCH
                rows_ref[kv, r0:r0 + CMP_STRIDE, :] = rows[g * CMP_STRIDE:(g + 1) * CMP_STRIDE, :]
        load = lambda s, kv=kv: rows_ref[kv, pl.ds(s, n_seg, stride=SEG_PITCH), :]
        out_ref[...] = _compress_math(load, n_seg, w1_ref, w2_ref, pos_ref, kv).astype(BF16)


def _compress_sample(page_table, cache_t, w1, w2, pos):
    bs, n_pages = page_table.shape
    page = cache_t.shape[-1]
    n_seg = n_pages * page // CMP_STRIDE
    const = lambda shape: pl.BlockSpec(shape, lambda i, pt: (0,) * len(shape), pipeline_mode=pl.Buffered(1))
    out = pl.BlockSpec((None, n_seg, KV_W), lambda i, pt: (i, 0, 0))
    return pl.pallas_call(
        _compress_sample_kernel,
        grid_spec=pltpu.PrefetchScalarGridSpec(
            num_scalar_prefetch=1, grid=(bs,),
            in_specs=[pl.BlockSpec(memory_space=pl.ANY), const(w1.shape), const(w2.shape), const(pos.shape)],
            out_specs=[out, out],
            scratch_shapes=[pltpu.VMEM((2, n_pages, 2, KV_W, page), F32),
                            pltpu.VMEM((2, n_seg * SEG_PITCH, KV_W), F32), pltpu.SemaphoreType.DMA((2,))]),
        out_shape=[jax.ShapeDtypeStruct((bs, n_seg, KV_W), BF16)] * 2,
        compiler_params=_cparams(("arbitrary",)),
        name="compress_sample",
    )(page_table, cache_t, w1, w2, pos)


_NT = (((1,), (1,)), ((), ()))


def _nsa_sample_kernel(pt_ref, qbd_ref, gl_ref, kc_ref, vc_ref, snew_ref, cwin_ref, wnew_ref, csel_ref,
                       slope_ref, tq_ref, tq8_ref, cover_ref, gsum_ref, gexp_ref, kaug_ref, o_ref,
                       buf_ref, s_ref, sem):
    n_pages, page = pt_ref.shape[1], buf_ref.shape[-1]
    past = n_pages * page
    n_seg = kc_ref.shape[0]
    npad = cover_ref.shape[1]
    n_sel = past // SEL_BLOCK + 1
    nr = qbd_ref.shape[0]
    wb = cwin_ref.shape[-1]
    slot, wait = _page_fetch(pt_ref, csel_ref, buf_ref, sem)

    qb = (qbd_ref[...] * SCALE).astype(BF16)
    slope = slope_ref[...]
    qpos = tq_ref[...] + float(past)
    zeros_pad = jnp.zeros((QB - snew_ref.shape[0], KV_W), F32)
    t_new = lax.broadcasted_iota(jnp.int32, (1, QB), 1).astype(F32)

    sc = lax.dot_general(qb, kc_ref[...], _NT, preferred_element_type=F32)
    n_i = lax.broadcasted_iota(jnp.int32, (1, n_seg), 1)
    dcmp = qpos - (n_i * CMP_STRIDE + (CMP_BLOCK - 1)).astype(F32)
    okc = dcmp >= 0.0
    sc = jnp.where(okc, sc - slope * dcmp, NEG)
    mc = jnp.max(sc, axis=-1, keepdims=True)
    pc = jnp.where(okc, jnp.exp(sc - mc), 0.0)
    lc = jnp.sum(pc, axis=-1, keepdims=True)
    pc = pc * (1.0 / jnp.where(lc > 0.0, lc, 1.0))
    o_cmp = jnp.dot(pc.astype(BF16), vc_ref[...], preferred_element_type=F32)

    psum = jnp.dot(gsum_ref[...], pc, precision=HI, preferred_element_type=F32)
    imp = jnp.dot(psum, cover_ref[...], precision=HI, preferred_element_type=F32)
    nq8 = gsum_ref.shape[0]
    jblk = lax.broadcasted_iota(jnp.int32, (nq8, npad), 1)
    qblk = (tq8_ref[...].astype(jnp.int32) + past) // SEL_BLOCK
    back = qblk - jblk
    visible = back >= 0
    forced = (jblk == 0) | (visible & (back < SEL_LOCAL))
    score = jnp.where(forced, 1e9, jnp.where(visible, imp, -1e9))
    rank = jnp.zeros((nq8, npad), F32)
    for jp in range(n_sel):
        col = score[:, jp:jp + 1]
        ge = jnp.where(col >= score, 1.0, 0.0)
        gt = jnp.where(col > score, 1.0, 0.0)
        rank = rank + jnp.where(jblk > jp, ge, gt)
    sel8 = jnp.where(visible, jnp.where(rank < float(min(SEL_TOP, n_sel)), 1.0, 0.0), 0.0)
    negb = (jnp.dot(gexp_ref[...], sel8, precision=HI, preferred_element_type=F32) - 1.0) * (-NEG)

    def tile_bias(jt):
        lane = lax.broadcasted_iota(jnp.int32, (nr, QB), 1)
        return jnp.where(lane < SEL_BLOCK, negb[:, 2 * jt:2 * jt + 1], negb[:, 2 * jt + 1:2 * jt + 2])

    wait()
    ppc = 4
    ck = ppc * page
    tiles_per_page = page // QB

    def chunk_t(c, kv):
        return jnp.concatenate([buf_ref[slot, c * ppc + u, kv] for u in range(ppc)], axis=1).astype(BF16)

    blocks_per_chunk = ck // SEL_BLOCK
    lane = lax.broadcasted_iota(jnp.int32, (nr, LANE), 1)
    fixed = jnp.where(lane == 0, 4.0 * slope, jnp.where(lane == 1, slope, 0.0))
    for c in range(n_pages // ppc):
        hi, mid, lo = _split3(slope * (float(c * ck) - qpos))
        b0 = c * blocks_per_chunk
        window = negb[:, (b0 // LANE) * LANE:(b0 // LANE + 1) * LANE]
        moved = pltpu.roll(window, (5 - b0 % LANE) % LANE, axis=1)
        qx = jnp.where(lane == 2, hi, jnp.where(lane == 3, mid, jnp.where(lane == 4, lo, fixed)))
        qx = jnp.where(lane >= 5, jnp.where(lane < 5 + blocks_per_chunk, moved, qx), qx)
        qa = jnp.concatenate([qb, qx.astype(BF16)], axis=1)
        ka = jnp.concatenate([chunk_t(c, 0), kaug_ref[...]], axis=0)
        s_ref[:, c * ck:(c + 1) * ck] = jnp.dot(qa, ka, preferred_element_type=F32)
    knew = jnp.concatenate([snew_ref[:, 0:KV_W], zeros_pad], axis=0).astype(BF16)
    vnew = jnp.concatenate([snew_ref[:, KV_W:2 * KV_W], zeros_pad], axis=0).astype(BF16)
    dnew = tq_ref[...] - t_new
    s = lax.dot_general(qb, knew, _NT, preferred_element_type=F32)
    s_ref[:, past:past + QB] = jnp.where(dnew >= 0.0, s - slope * dnew + tile_bias(past // QB), NEG)
    s_all = s_ref[...]
    ms = jnp.max(s_all, axis=-1, keepdims=True)
    ps = jnp.exp(s_all - ms)
    ls = jnp.sum(ps, axis=-1, keepdims=True)
    psb = ps.astype(BF16)
    o_sel = jnp.dot(psb[:, past:past + QB], vnew, preferred_element_type=F32)
    for c in range(n_pages // ppc):
        o_sel = o_sel + lax.dot_general(psb[:, c * ck:(c + 1) * ck], chunk_t(c, 1), _NT,
                                        preferred_element_type=F32)
    o_sel = o_sel * (1.0 / ls)

    s1 = jnp.dot(qb, cwin_ref[0].astype(BF16), preferred_element_type=F32)
    d1 = float(wb) + tq_ref[...] - lax.broadcasted_iota(jnp.int32, (1, wb), 1).astype(F32)
    ok1 = d1 < float(WINDOW)
    s1 = jnp.where(ok1, s1 - slope * d1, NEG)
    wk = jnp.concatenate([wnew_ref[:, 0:KV_W], zeros_pad], axis=0).astype(BF16)
    wv = jnp.concatenate([wnew_ref[:, KV_W:2 * KV_W], zeros_pad], axis=0).astype(BF16)
    ok2 = dnew >= 0.0
    s2 = jnp.where(ok2, lax.dot_general(qb, wk, _NT, preferred_element_type=F32) - slope * dnew, NEG)
    mw = jnp.maximum(jnp.max(s1, axis=-1, keepdims=True), jnp.max(s2, axis=-1, keepdims=True))
    p1 = jnp.where(ok1, jnp.exp(s1 - mw), 0.0)
    p2 = jnp.where(ok2, jnp.exp(s2 - mw), 0.0)
    lw_ = jnp.sum(p1, axis=-1, keepdims=True) + jnp.sum(p2, axis=-1, keepdims=True)
    o_win = (lax.dot_general(p1.astype(BF16), cwin_ref[1].astype(BF16), _NT, preferred_element_type=F32)
             + jnp.dot(p2.astype(BF16), wv, preferred_element_type=F32)) * (1.0 / lw_)

    sig = _sigmoid(gl_ref[...])
    o_ref[...] = sig[:, 0:1] * o_cmp + sig[:, 1:2] * o_sel + sig[:, 2:3] * o_win


def _nsa_sample(page_table, qbd, gl, kc, vc, snew, cwin, wnew, csel, ts):
    bs, n_pages = page_table.shape
    page = csel.shape[-1]
    past = n_pages * page
    n_seg = kc.shape[1]
    n_sel = past // SEL_BLOCK + 1
    npad = -(-(n_sel + 1) // LANE) * LANE
    nr = KVH * GROUP * ts
    wb = cwin.shape[-1]
    r = np.arange(nr)
    slope = np.exp2(-8.0 * ((r // ts) + 1) / NSA_HEADS).astype(np.float32)[:, None]
    tq = (r % ts).astype(np.float32)[:, None]
    r8 = np.arange(KVH * ts)
    tq8 = (r8 % ts).astype(np.float32)[:, None]
    cover = np.zeros((n_seg, npad), np.float32)
    cover[:, :n_sel] = _cover_matrix(n_seg, n_sel)
    gsum = ((r[None, :] // (GROUP * ts) == r8[:, None] // ts) & (r[None, :] % ts == r8[:, None] % ts)).astype(np.float32)
    ck = 4 * page
    kpos = np.arange(ck)
    kaug = np.zeros((KV_W, ck), np.float32)
    kaug[0] = kpos // 4
    kaug[1] = kpos % 4
    kaug[2:5] = 1.0
    for r in range(ck // SEL_BLOCK):
        kaug[5 + r] = (kpos // SEL_BLOCK == r)
    consts = [slope, tq, tq8, cover, gsum, gsum.T.copy(), jnp.asarray(kaug, BF16)]
    const = lambda shape: pl.BlockSpec(shape, lambda i, pt: (0,) * len(shape), pipeline_mode=pl.Buffered(1))
    per = lambda *s: pl.BlockSpec((None,) + s, lambda i, pt: (i,) + (0,) * len(s))
    return pl.pallas_call(
        _nsa_sample_kernel,
        grid_spec=pltpu.PrefetchScalarGridSpec(
            num_scalar_prefetch=1, grid=(bs,),
            in_specs=[per(nr, KV_W), per(nr, LANE), per(n_seg, KV_W), per(n_seg, KV_W), per(SUBLANE, 2 * KV_W),
                      per(2, KV_W, wb), per(SUBLANE, 2 * KV_W), pl.BlockSpec(memory_space=pl.ANY)]
                     + [const(c.shape) for c in consts],
            out_specs=per(nr, KV_W),
            scratch_shapes=[pltpu.VMEM((2, n_pages, 2, KV_W, page), F32), pltpu.VMEM((nr, past + QB), F32),
                            pltpu.SemaphoreType.DMA((2,))]),
        out_shape=jax.ShapeDtypeStruct((bs, nr, KV_W), F32),
        compiler_params=_cparams(("arbitrary",)),
        name="nsa_sample",
    )(page_table, qbd, gl, kc, vc, snew, cwin, wnew, csel, *[jnp.asarray(c) for c in consts])


def _hgrn_sample_kernel(ts, hq_ref, hf_ref, hi_ref, hg_ref, lb_ref, ng_ref, s0_ref, o_ref, s1_ref):
    rows = hq_ref.shape[0]
    q = hq_ref[...]
    v = hi_ref[...]
    log_f, k = _hgrn_gates(hf_ref[...], lb_ref[...])
    tloc = lax.broadcasted_iota(jnp.int32, (rows, HG_W), 0) % ts
    up = lambda a, d: pltpu.roll(a, d, axis=0)
    down = lambda a, d: pltpu.roll(a, rows - d, axis=0)

    bcum = log_f
    for d in range(1, ts):
        bcum = bcum + jnp.where(tloc >= d, up(log_f, d), 0.0)
    b_last = bcum
    for d in range(1, ts):
        b_last = jnp.where(tloc == ts - 1 - d, down(bcum, d), b_last)

    o_intra = [jnp.zeros((rows, HG_D), F32) for _ in range(HG_H)]
    for d in range(ts):
        kd, bd, vd = (k, bcum, v) if d == 0 else (up(k, d), up(bcum, d), up(v, d))
        w = jnp.where(tloc >= d, q * kd * jnp.exp(jnp.where(tloc >= d, bcum - bd, 0.0)), 0.0)
        for h in range(HG_H):
            sl = slice(h * HG_D, (h + 1) * HG_D)
            o_intra[h] = o_intra[h] + jnp.sum(w[:, sl], axis=-1, keepdims=True) * vd[:, sl]

    qb = q * jnp.exp(bcum)
    kh = k * jnp.exp(b_last - bcum)
    per_tile = SUBLANE // ts
    row8 = lax.broadcasted_iota(jnp.int32, (SUBLANE, HG_D), 0) // ts
    tiles = []
    for j in range(rows // SUBLANE):
        r8 = slice(j * SUBLANE, (j + 1) * SUBLANE)
        heads = []
        for h in range(HG_H):
            sl = slice(h * HG_D, (h + 1) * HG_D)
            o_inter = jnp.zeros((SUBLANE, HG_D), F32)
            for u in range(per_tile):
                seq = j * per_tile + u
                mine = row8 == u
                s0 = s0_ref[seq, h]
                o_inter = o_inter + jnp.dot(jnp.where(mine, qb[r8, sl], 0.0).astype(BF16), s0.astype(BF16),
                                            preferred_element_type=F32)
                upd = lax.dot_general(jnp.where(mine, kh[r8, sl], 0.0).astype(BF16), v[r8, sl].astype(BF16),
                                      (((0,), (0,)), ((), ())), preferred_element_type=F32)
                r_last = j * SUBLANE + u * ts + ts - 1
                dec = jnp.exp(bcum[r_last:r_last + 1, sl])
                s1_ref[seq, h] = jnp.broadcast_to(dec, (HG_D, HG_D)).T * s0 + upd
            heads.append(o_inter + o_intra[h][r8, :])
        tiles.append(jnp.concatenate(heads, axis=1))
    o = jnp.concatenate(tiles, axis=0)
    o_ref[...] = _hgrn_out(o, hg_ref[...], ng_ref[...]).astype(BF16)


def _hgrn_sample(z, lb, ng, s0, bs, ts, nb):
    rows = nb * ts
    col = lambda c0: pl.BlockSpec((rows, HG_W), lambda i: (i, c0 // HG_W))
    st = pl.BlockSpec((nb, HG_H, HG_D, HG_D), lambda i: (i, 0, 0, 0))
    return pl.pallas_call(
        functools.partial(_hgrn_sample_kernel, ts),
        grid=(bs // nb,),
        in_specs=[col(C_HQ), col(C_HF), col(C_HI), col(C_HG), _const_spec((1, HG_W)), _const_spec((1, HG_D)), st],
        out_specs=[pl.BlockSpec((rows, HG_W), lambda i: (i, 0)), st],
        out_shape=[jax.ShapeDtypeStruct((bs * ts, HG_W), BF16),
                   jax.ShapeDtypeStruct((bs, HG_H, HG_D, HG_D), F32)],
        compiler_params=_cparams(("parallel",)),
        name="hgrn_sample",
    )(z, z, z, z, lb, ng, s0)


def _layer_sample(x, p, cache_cmp, cache_sel, cache_win, state, page_table, lw, bs, ts):
    n = bs * ts
    assert SUBLANE % ts == 0 and n % SUBLANE == 0
    z = _norm_matmul(x, lw['g_pre_mix'], lw['w_in'], tm=min(512, n), tn=512)
    kv_c = z[:, C_KVC:C_KVC + 2 * KV_W]
    kv_s = z[:, C_KVS:C_KVS + 2 * KV_W]
    kv_w = z[:, C_KVW:C_KVW + 2 * KV_W]

    rows_last = lambda a: jnp.transpose(a, (0, 2, 3, 4, 1)).reshape(a.shape[0], 2, KV_W, a.shape[1])
    kc, vc = _compress_sample(page_table, rows_last(cache_cmp), lw['cmp_w1'], lw['cmp_w2'], lw['cmp_pos'])

    eye = jnp.eye(KVH, dtype=F32)
    q5 = z[:, C_Q:C_Q + NSA_W].reshape(bs, ts, KVH, GROUP, HD).transpose(0, 2, 3, 1, 4)
    qbd = jnp.einsum('bkgtd,kq->bkgtqd', q5, eye).reshape(bs, KVH * GROUP * ts, KV_W)
    g5 = z[:, C_GN:C_GN + 3 * NSA_HEADS].reshape(bs, ts, KVH, GROUP, 3).transpose(0, 2, 3, 1, 4)
    gl = jnp.pad(g5.reshape(bs, KVH * GROUP * ts, 3), ((0, 0), (0, 0), (0, LANE - 3)))
    pad_rows = lambda a: jnp.pad(a.reshape(bs, ts, 2 * KV_W), ((0, 0), (0, SUBLANE - ts), (0, 0)))
    o_rows = _nsa_sample(page_table, qbd, gl, kc, vc, pad_rows(kv_s), rows_last(cache_win), pad_rows(kv_w),
                         rows_last(cache_sel), ts)
    o6 = o_rows.reshape(bs, KVH, GROUP, ts, KVH, HD)
    o_nsa = jnp.stack([o6[:, kvh, :, :, kvh, :] for kvh in range(KVH)], axis=1)
    o_nsa = o_nsa.transpose(0, 3, 1, 2, 4).reshape(n, NSA_W).astype(BF16)

    o_hg, st = _hgrn_sample(z, lw['hg_lb'], lw['hg_norm'], state, bs, ts, nb=min(8, bs))
    h1 = _merge(o_nsa, o_hg, z, x, lw['w_bn'], lw['w_bh'], lw['w_out'], lw['g_post_mix'], tm=min(512, n))
    h2 = _mlp(h1, p, lw['w_up'], lw['w_down'], lw['w_gate'], lw['w_proj'],
              lw['g_pre_mlp'], lw['g_post_mlp'], lw['g_ple'], tm=min(512, n))
    kv6 = lambda a: a.reshape(bs, ts, 2, KVH, HD)
    win_buf = jnp.concatenate([cache_win, kv6(kv_w)], axis=1)[:, ts:]
    return h2, kv6(kv_c), kv6(kv_s), win_buf, st


def kernel(x_prompt, x_sample, cache_cmp_kv, cache_sel_kv, cache_win_kv, state_hgrn, page_table, p_prompt,
           p_sample, w_in, cmp_k_w1, cmp_k_w2, cmp_v_w1, cmp_v_w2, cmp_pos, hg_lb_logits, hg_norm, w_branch_nsa,
           w_branch_hgrn, w_out, norm_pre_mix, norm_post_mix, norm_pre_mlp, norm_post_mlp, w_mlp_up, w_mlp_down,
           norm_ple, w_ple_gate, w_ple_proj):
    depth = w_in.shape[0]
    b, t, d = x_prompt.shape
    bs, ts, _ = x_sample.shape
    lb_all = jnp.cumsum(jax.nn.softmax(hg_lb_logits.astype(F32), axis=0), axis=0)
    h_p = x_prompt.reshape(b * t, d)
    h_s = x_sample.reshape(bs * ts, d)
    outs = [[] for _ in range(8)]
    for i in range(depth):
        lw = _prep_layer(i, lb_all, w_in, cmp_k_w1, cmp_k_w2, cmp_v_w1, cmp_v_w2, cmp_pos, hg_norm, w_branch_nsa,
                         w_branch_hgrn, w_out, norm_pre_mix, norm_post_mix, norm_pre_mlp, norm_post_mlp, w_mlp_up,
                         w_mlp_down, norm_ple, w_ple_gate, w_ple_proj)
        h_p, *res_p = _layer_prompt(h_p, p_prompt[i].reshape(b * t, -1), lw, b, t)
        h_s, *res_s = _layer_sample(h_s, p_sample[i].reshape(bs * ts, -1), cache_cmp_kv[i], cache_sel_kv[i],
                                    cache_win_kv[i], state_hgrn[i], page_table, lw, bs, ts)
        for lst, v in zip(outs, res_p + res_s):
            lst.append(v)
    return (h_p.reshape(b, t, d), h_s.reshape(bs, ts, d)) + tuple(jnp.stack(lst, axis=0) for lst in outs)
```

```python
import functools
import math

import numpy as np
import jax
import jax.numpy as jnp
from jax import lax
from jax.experimental import pallas as pl
from jax.experimental.pallas import tpu as pltpu

F32 = jnp.float32
BF16 = jnp.bfloat16

D_MODEL = 1024
NSA_HEADS = 8
KVH = 2
GROUP = NSA_HEADS // KVH
HD = 64
NSA_W = NSA_HEADS * HD
KV_W = KVH * HD
CMP_BLOCK = 32
CMP_STRIDE = 16
CMP_HIDDEN = 2 * HD
SEL_BLOCK = 64
SEL_TOP = 16
SEL_LOCAL = 2
WINDOW = 512
HG_W = 512
HG_H = 4
HG_D = 128
HG_CHUNK = 64
D_FF = 4 * D_MODEL
PLE_DIM = 256
RMS_EPS = 1e-6
NEG = -1e30
M_INIT = -1e20
SCALE = HD ** -0.5
HG_SAFE_EXP = 60.0

LANE = 128
SUBLANE = 8
VMEM_LIMIT = 48 * 1024 * 1024

C_GM = 0
C_Q = 2048
C_HQ = 2560
C_HF = 3072
C_HI = 3584
C_HG = 4096
C_KVC = 4608
C_KVS = 4864
C_KVW = 5120
C_GN = 5376
Z_COLS = 5632
QB = 128
HI = lax.Precision.HIGHEST


def _cparams(sem, vmem=VMEM_LIMIT):
    return pltpu.CompilerParams(dimension_semantics=sem, vmem_limit_bytes=vmem)


def _rms(x, g):
    return x * lax.rsqrt(jnp.mean(x * x, axis=-1, keepdims=True) + RMS_EPS) * g


def _sigmoid(x):
    return 1.0 / (1.0 + jnp.exp(-x))


def _gelu_tanh(x):
    return 0.5 * x * (1.0 + jnp.tanh(math.sqrt(2.0 / math.pi) * (x + 0.044715 * (x * x * x))))


def _const_spec(shape):
    nd = len(shape)
    return pl.BlockSpec(shape, lambda *_: (0,) * nd)


def _norm_matmul_kernel(x_ref, g_ref, w_ref, o_ref, xn_ref):
    @pl.when(pl.program_id(1) == 0)
    def _():
        xn_ref[...] = _rms(x_ref[...], g_ref[...]).astype(BF16)

    tn = o_ref.shape[1]
    col = pl.multiple_of(pl.program_id(1) * tn, LANE)
    o_ref[...] = jnp.dot(xn_ref[...], w_ref[:, pl.ds(col, tn)], preferred_element_type=F32)


def _norm_matmul(x, g, w, tm, tn):
    n, d = x.shape
    c = w.shape[1]
    return pl.pallas_call(
        _norm_matmul_kernel,
        grid=(n // tm, c // tn),
        in_specs=[pl.BlockSpec((tm, d), lambda i, j: (i, 0)),
                  pl.BlockSpec((1, d), lambda i, j: (0, 0)),
                  pl.BlockSpec((d, c), lambda i, j: (0, 0), pipeline_mode=pl.Buffered(1))],
        out_specs=pl.BlockSpec((tm, tn), lambda i, j: (i, j)),
        out_shape=jax.ShapeDtypeStruct((n, c), F32),
        scratch_shapes=[pltpu.VMEM((tm, d), BF16)],
        compiler_params=_cparams(("parallel", "arbitrary")),
        name="norm_in_proj",
    )(x, g, w)


def _compress_math(load_rows, n_seg, w1_ref, w2_ref, pos_ref, kv):
    x = jnp.concatenate([load_rows(s) for s in range(CMP_STRIDE)], axis=1).astype(BF16)
    h0 = jnp.dot(x, w1_ref[kv, 0], preferred_element_type=F32)
    h1 = jnp.dot(x, w1_ref[kv, 1], preferred_element_type=F32)
    posb = (jnp.dot(pos_ref[0], w1_ref[kv, 0], preferred_element_type=F32)
            + jnp.dot(pos_ref[1], w1_ref[kv, 1], preferred_element_type=F32))
    hid = h0 + pltpu.roll(h1, n_seg - 1, axis=0) + posb[0:1]
    return jnp.dot(_gelu_tanh(hid).astype(BF16), w2_ref[kv], preferred_element_type=F32)


def _compress_prompt_kernel(xk_ref, xv_ref, w1_ref, w2_ref, pos_ref, kc_ref, vc_ref):
    n_seg = xk_ref.shape[0] // CMP_STRIDE
    for kv, x_ref, out_ref in ((0, xk_ref, kc_ref), (1, xv_ref, vc_ref)):
        load = lambda s, x_ref=x_ref: x_ref[pl.ds(s, n_seg, stride=CMP_STRIDE), :]
        out_ref[...] = _compress_math(load, n_seg, w1_ref, w2_ref, pos_ref, kv).astype(BF16)


def _compress_prompt(z, w1, w2, pos, b, t):
    n_seg = t // CMP_STRIDE
    return pl.pallas_call(
        _compress_prompt_kernel,
        grid=(b,),
        in_specs=[pl.BlockSpec((t, KV_W), lambda i: (i, C_KVC // KV_W)),
                  pl.BlockSpec((t, KV_W), lambda i: (i, C_KVC // KV_W + 1)),
                  _const_spec(w1.shape), _const_spec(w2.shape), _const_spec(pos.shape)],
        out_specs=[pl.BlockSpec((None, n_seg, KV_W), lambda i: (i, 0, 0)),
                   pl.BlockSpec((None, n_seg, KV_W), lambda i: (i, 0, 0))],
        out_shape=[jax.ShapeDtypeStruct((b, n_seg, KV_W), BF16),
                   jax.ShapeDtypeStruct((b, n_seg, KV_W), BF16)],
        compiler_params=_cparams(("parallel",)),
        name="compress_prompt",
    )(z, z, w1, w2, pos)


def _nsa_prompt_kernel(q_ref, gn_ref, kc_ref, vct_ref, ks_ref, vst_ref, kw_ref, vwt_ref,
                       bias0_ref, slopes_ref, covert_ref, o_ref,
                       m_ref, l_ref, acc_ref, sel_ref):
    i = pl.program_id(1)
    t0 = i * QB
    n_seg = kc_ref.shape[0]
    n_sel = covert_ref.shape[0]
    nl = NSA_HEADS * QB
    half = GROUP * QB
    slopes = slopes_ref[...]

    q = q_ref[...] * SCALE
    zero = jnp.zeros((HD, QB), F32)
    cols = []
    for kvh in range(KVH):
        for p in range(GROUP // 2):
            c0 = (kvh * (GROUP // 2) + p) * LANE
            blk = q[:, c0:c0 + LANE].T
            for hh in range(2):
                piece = blk[hh * HD:(hh + 1) * HD, :]
                cols.append(jnp.concatenate([piece, zero] if kvh == 0 else [zero, piece], axis=0))
    qbdt = jnp.concatenate(cols, axis=1).astype(BF16)

    lane = lax.broadcasted_iota(jnp.int32, (QB, QB), 1)
    sub = lax.broadcasted_iota(jnp.int32, (QB, QB), 0)
    dl = (lane - sub).astype(F32)

    sc = jnp.dot(kc_ref[...], qbdt, preferred_element_type=F32)
    n_i = lax.broadcasted_iota(jnp.int32, (n_seg, QB), 0)
    t_i = lax.broadcasted_iota(jnp.int32, (n_seg, QB), 1)
    d1 = (t0 + t_i - (n_i * CMP_STRIDE + (CMP_BLOCK - 1))).astype(F32)
    dc = jnp.concatenate([d1] * NSA_HEADS, axis=1)
    okc = dc >= 0.0
    sc = jnp.where(okc, sc - slopes * dc, NEG)
    mc = jnp.max(sc, axis=0, keepdims=True)
    pc = jnp.where(okc, jnp.exp(sc - mc), 0.0)
    lc = jnp.sum(pc, axis=0, keepdims=True)
    pc = pc * (1.0 / jnp.where(lc > 0.0, lc, 1.0))
    pcb = pc.astype(BF16)
    ocmp = [jnp.dot(vct_ref[kvh * HD:(kvh + 1) * HD, :], pcb[:, kvh * half:(kvh + 1) * half],
                    preferred_element_type=F32) for kvh in range(KVH)]

    jblk = lax.broadcasted_iota(jnp.int32, (n_sel, QB), 0)
    qblk = (t0 + lax.broadcasted_iota(jnp.int32, (n_sel, QB), 1)) // SEL_BLOCK
    back = qblk - jblk
    visible = back >= 0
    forced = (jblk == 0) | (visible & (back < SEL_LOCAL))
    for kvh in range(KVH):
        psum = pc[:, kvh * half:kvh * half + QB]
        for g in range(1, GROUP):
            psum = psum + pc[:, kvh * half + g * QB:kvh * half + (g + 1) * QB]
        imp = jnp.dot(covert_ref[...], psum, precision=HI, preferred_element_type=F32)
        score = jnp.where(forced, 1e9, jnp.where(visible, imp, -1e9))
        rank = jnp.zeros((n_sel, QB), F32)
        for jp in range(n_sel):
            row = score[jp:jp + 1, :]
            ge = jnp.where(row >= score, 1.0, 0.0)
            gt = jnp.where(row > score, 1.0, 0.0)
            rank = rank + jnp.where(jblk > jp, ge, gt)
        sel_ref[kvh] = jnp.where(visible, jnp.where(rank < float(min(SEL_TOP, n_sel)), 1.0, 0.0), 0.0)

    def sweep(k_ref, vt_ref, lo, hi, tile_bias):
        m_ref[...] = jnp.full((1, nl), M_INIT, F32)
        l_ref[...] = jnp.zeros((1, nl), F32)
        acc_ref[...] = jnp.zeros((KVH, HD, half), F32)

        def body(jt, carry):
            off = (t0 - jt * QB).astype(F32)
            s = jnp.dot(k_ref[jt], qbdt, preferred_element_type=F32)
            s = s - bias0_ref[...] - slopes * off + tile_bias(jt, off)
            m_old = m_ref[...]
            m_new = jnp.maximum(m_old, jnp.max(s, axis=0, keepdims=True))
            alpha = jnp.exp(m_old - m_new)
            p = jnp.exp(s - m_new)
            l_ref[...] = alpha * l_ref[...] + jnp.sum(p, axis=0, keepdims=True)
            m_ref[...] = m_new
            pb = p.astype(BF16)
            vt = vt_ref[jt]
            for kvh in range(KVH):
                pv = jnp.dot(vt[kvh * HD:(kvh + 1) * HD, :], pb[:, kvh * half:(kvh + 1) * half],
                             preferred_element_type=F32)
                acc_ref[kvh] = alpha[:, kvh * half:(kvh + 1) * half] * acc_ref[kvh] + pv
            return carry

        lax.fori_loop(lo, hi, body, 0)
        inv = 1.0 / l_ref[...]
        return [acc_ref[kvh] * inv[:, kvh * half:(kvh + 1) * half] for kvh in range(KVH)]

    def sel_bias(jt, off):
        causal = dl >= -off
        parts = []
        for kvh in range(KVH):
            rows = sel_ref[kvh, pl.ds(jt * (QB // SEL_BLOCK), QB // SEL_BLOCK), :]
            blockmask = jnp.concatenate(
                [jnp.broadcast_to(rows[r:r + 1, :], (SEL_BLOCK, QB)) for r in range(QB // SEL_BLOCK)], axis=0)
            nb = jnp.where(causal, (blockmask - 1.0) * (-NEG), NEG)
            parts += [nb] * GROUP
        return jnp.concatenate(parts, axis=1)

    def win_bias(jt, off):
        nb = jnp.where(dl >= -off, jnp.where(dl < float(WINDOW) - off, 0.0, NEG), NEG)
        return jnp.concatenate([nb] * NSA_HEADS, axis=1)

    osel = sweep(ks_ref, vst_ref, 0, i + 1, sel_bias)
    owin = sweep(kw_ref, vwt_ref, jnp.maximum(i - WINDOW // QB, 0), i + 1, win_bias)

    gt_ = _sigmoid(gn_ref[...]).T
    for kvh in range(KVH):
        for p in range(GROUP // 2):
            pieces = []
            for hh in range(2):
                g = 2 * p + hh
                h = kvh * GROUP + g
                cs = slice(g * QB, (g + 1) * QB)
                pieces.append(gt_[3 * h:3 * h + 1, :] * ocmp[kvh][:, cs]
                              + gt_[3 * h + 1:3 * h + 2, :] * osel[kvh][:, cs]
                              + gt_[3 * h + 2:3 * h + 3, :] * owin[kvh][:, cs])
            c0 = (kvh * (GROUP // 2) + p) * LANE
            o_ref[:, c0:c0 + LANE] = jnp.concatenate(pieces, axis=0).T.astype(BF16)


PB = 256
N_AUG = 16
VROWS = HD + 16
LOG2E = math.log2(math.e)


def _split3(x):
    hi = x.astype(BF16).astype(F32)
    r = x - hi
    mid = r.astype(BF16).astype(F32)
    return hi, mid, r - mid


def _nsa_prompt256_kernel(q_ref, gn_ref, kc_ref, vct_ref, zks_ref, zvs_ref, zkw_ref, zvw_ref,
                          augc_ref, slopes_ref, covert_ref, o_ref,
                          qa_ref, m_ref, acc_ref, sel_ref, score_ref, ks_ref, vst_ref, kw_ref, vwt_ref,
                          sa_ref, sb_ref):
    i = pl.program_id(1)
    t0 = i * PB

    @pl.when(i == 0)
    def _():
        ones_rows = jnp.where(lax.broadcasted_iota(jnp.int32, (VROWS - HD, PB), 0) == 0, 1.0, 0.0)
        for zk, zv, k_dst, vt_dst in ((zks_ref, zvs_ref, ks_ref, vst_ref), (zkw_ref, zvw_ref, kw_ref, vwt_ref)):
            for j in range(k_dst.shape[0]):
                k_dst[j] = zk[j * PB:(j + 1) * PB, :].astype(BF16)
                vt = zv[j * PB:(j + 1) * PB, :].T
                vt_dst[j] = jnp.concatenate(
                    [piece for kvh in range(KVH) for piece in (vt[kvh * HD:(kvh + 1) * HD, :], ones_rows)],
                    axis=0).astype(BF16)
    n_seg = kc_ref.shape[0]
    n_sel = covert_ref.shape[0]
    nl = NSA_HEADS * PB
    half = GROUP * PB
    slopes2 = slopes_ref[...] * LOG2E
    tlane = (lax.broadcasted_iota(jnp.int32, (1, nl), 1) % PB).astype(F32)
    blocks_per_tile = PB // SEL_BLOCK

    q = q_ref[...] * (SCALE * LOG2E)
    zero = jnp.zeros((HD, PB), F32)
    cols = []
    for kvh in range(KVH):
        for p in range(GROUP // 2):
            c0 = (kvh * (GROUP // 2) + p) * LANE
            blk = q[:, c0:c0 + LANE].T
            for hh in range(2):
                piece = blk[hh * HD:(hh + 1) * HD, :]
                cols.append(jnp.concatenate([piece, zero] if kvh == 0 else [zero, piece], axis=0))
    qbdt = jnp.concatenate(cols, axis=1).astype(BF16)
    qa_ref[0:KV_W, :] = qbdt
    qa_ref[KV_W + N_AUG:2 * KV_W, :] = jnp.zeros((KV_W - N_AUG, nl), BF16)

    lane = lax.broadcasted_iota(jnp.int32, (PB, PB), 1)
    sub = lax.broadcasted_iota(jnp.int32, (PB, PB), 0)
    causal_bias = jnp.where(lane >= sub, 0.0, NEG)
    edge_bias = jnp.where(lane < sub, 0.0, NEG)

    sc = jnp.dot(kc_ref[...], qbdt, preferred_element_type=F32)
    n_i = lax.broadcasted_iota(jnp.int32, (n_seg, PB), 0)
    t_i = lax.broadcasted_iota(jnp.int32, (n_seg, PB), 1)
    d1 = (t0 + t_i - (n_i * CMP_STRIDE + (CMP_BLOCK - 1))).astype(F32)
    dc = jnp.concatenate([d1] * NSA_HEADS, axis=1)
    okc = dc >= 0.0
    sc = jnp.where(okc, sc - slopes2 * dc, NEG)
    mc = jnp.max(sc, axis=0, keepdims=True)
    pc = jnp.where(okc, jnp.exp2(sc - mc), 0.0)
    lc = jnp.sum(pc, axis=0, keepdims=True)
    pc = pc * (1.0 / jnp.where(lc > 0.0, lc, 1.0))
    pcb = pc.astype(BF16)
    ocmp = [jnp.dot(vct_ref[kvh * HD:(kvh + 1) * HD, :], pcb[:, kvh * half:(kvh + 1) * half],
                    preferred_element_type=F32) for kvh in range(KVH)]

    jblk = lax.broadcasted_iota(jnp.int32, (n_sel, PB), 0)
    qblk = (t0 + lax.broadcasted_iota(jnp.int32, (n_sel, PB), 1)) // SEL_BLOCK
    back = qblk - jblk
    visible = back >= 0
    forced = (jblk == 0) | (visible & (back < SEL_LOCAL))
    n_groups = jnp.minimum((t0 + PB - 1) // SEL_BLOCK // SUBLANE + 1, n_sel // SUBLANE)
    for kvh in range(KVH):
        psum = pc[:, kvh * half:kvh * half + PB]
        for g in range(1, GROUP):
            psum = psum + pc[:, kvh * half + g * PB:kvh * half + (g + 1) * PB]
        imp = jnp.dot(covert_ref[...], psum, precision=HI, preferred_element_type=F32)
        score = jnp.where(forced, 1e9, jnp.where(visible, imp, -1e9))
        score_ref[...] = score

        def rank_group(gi, rank):
            rows8 = score_ref[pl.ds(pl.multiple_of(gi * SUBLANE, SUBLANE), SUBLANE), :]
            for u in range(SUBLANE):
                row = rows8[u:u + 1, :]
                ge = jnp.where(row >= score, 1.0, 0.0)
                gt = jnp.where(row > score, 1.0, 0.0)
                rank = rank + jnp.where(jblk > gi * SUBLANE + u, ge, gt)
            return rank

        rank = lax.fori_loop(0, n_groups, rank_group, jnp.zeros((n_sel, PB), F32))
        sel_ref[kvh] = jnp.where(visible, jnp.where(rank < float(min(SEL_TOP, n_sel)), 0.0, NEG), NEG)

    sl3 = _split3(slopes2)

    def scores(k_ref, jt, use_sel, dst_ref):
        off = (t0 - jt * PB).astype(F32)
        c3 = _split3(-slopes2 * (tlane + off))
        rows = list(sl3) + list(c3)
        if use_sel:
            tiles_per_group = SUBLANE // blocks_per_tile
            base = pl.multiple_of((jt // tiles_per_group) * SUBLANE, SUBLANE)
            which = jt % tiles_per_group
            per_kvh = []
            for kvh in range(KVH):
                rows8 = sel_ref[kvh, pl.ds(base, SUBLANE), :]
                mine = rows8[0:blocks_per_tile, :]
                for w in range(1, tiles_per_group):
                    mine = jnp.where(which == w, rows8[w * blocks_per_tile:(w + 1) * blocks_per_tile, :], mine)
                per_kvh.append(mine)
            rows.append(jnp.concatenate([per_kvh[kvh] for kvh in range(KVH) for _ in range(GROUP)], axis=1))
        rows.append(jnp.zeros((N_AUG - sum(r.shape[0] for r in rows), nl), F32))
        qa_ref[KV_W:KV_W + N_AUG, :] = jnp.concatenate(rows, axis=0).astype(BF16)
        ka = jnp.concatenate([k_ref[jt], augc_ref[...]], axis=1)
        s = jnp.dot(ka, qa_ref[...], preferred_element_type=F32)
        dst_ref[0:PB, :] = s
        dst_ref[PB:PB + 1, :] = jnp.max(s, axis=0, keepdims=True)

    def softmax_pv(vt_ref, jt, src_ref, extra):
        s = src_ref[0:PB, :]
        if extra is not None:
            s = s + jnp.concatenate([extra] * NSA_HEADS, axis=1)
            s_max = jnp.max(s, axis=0, keepdims=True)
        else:
            s_max = src_ref[PB:PB + 1, :]
        m_old = m_ref[...]
        m_new = jnp.maximum(m_old, s_max)
        alpha = jnp.exp2(m_old - m_new)
        p = jnp.exp2(s - m_new)
        m_ref[...] = m_new
        pb = p.astype(BF16)
        vt = vt_ref[jt]
        for kvh in range(KVH):
            pv = jnp.dot(vt[kvh * VROWS:(kvh + 1) * VROWS, :], pb[:, kvh * half:(kvh + 1) * half],
                         preferred_element_type=F32)
            acc_ref[kvh] = alpha[:, kvh * half:(kvh + 1) * half] * acc_ref[kvh] + pv

    def reset():
        m_ref[...] = jnp.full((1, nl), M_INIT, F32)
        acc_ref[...] = jnp.zeros((KVH, VROWS, half), F32)

    def result():
        return [acc_ref[kvh, 0:HD, :] * (1.0 / acc_ref[kvh, HD:HD + 1, :]) for kvh in range(KVH)]

    def step(k_ref, vt_ref, jt, use_sel, extra):
        scores(k_ref, jt, use_sel, sa_ref)
        softmax_pv(vt_ref, jt, sa_ref, extra)

    reset()
    n_pairs = (i + 2) // 2
    last = pl.num_programs(1) - 1
    odd_i = (i % 2) == 1
    scores(ks_ref, 0, True, sa_ref)

    def pair(u, carry):
        scores(ks_ref, 2 * u + 1, True, sb_ref)
        softmax_pv(vst_ref, 2 * u, sa_ref, None)
        scores(ks_ref, 2 * u + 2, True, sa_ref)
        softmax_pv(vst_ref, 2 * u + 1, sb_ref, None)
        return carry

    lax.fori_loop(0, n_pairs - 1, pair, 0)
    j0 = 2 * n_pairs - 2
    j1 = jnp.minimum(j0 + 1, last)
    scores(ks_ref, j1, True, sb_ref)
    softmax_pv(vst_ref, j0, sa_ref, jnp.where(odd_i, 0.0, causal_bias))
    softmax_pv(vst_ref, j1, sb_ref, jnp.where(odd_i, causal_bias, NEG))
    osel = result()

    reset()
    w0 = jnp.maximum(i - 2, 0)
    w1 = jnp.maximum(i - 1, 0)
    scores(kw_ref, w0, False, sa_ref)
    scores(kw_ref, w1, False, sb_ref)
    softmax_pv(vwt_ref, w0, sa_ref, jnp.where(i >= 2, edge_bias, NEG))
    scores(kw_ref, i, False, sa_ref)
    softmax_pv(vwt_ref, w1, sb_ref, jnp.where(i >= 1, 0.0, jnp.full((PB, PB), NEG, F32)))
    softmax_pv(vwt_ref, i, sa_ref, causal_bias)
    owin = result()

    gt_ = _sigmoid(gn_ref[...]).T
    for kvh in range(KVH):
        for p in range(GROUP // 2):
            pieces = []
            for hh in range(2):
                g = 2 * p + hh
                h = kvh * GROUP + g
                cs = slice(g * PB, (g + 1) * PB)
                pieces.append(gt_[3 * h:3 * h + 1, :] * ocmp[kvh][:, cs]
                              + gt_[3 * h + 1:3 * h + 2, :] * osel[kvh][:, cs]
                              + gt_[3 * h + 2:3 * h + 3, :] * owin[kvh][:, cs])
            c0 = (kvh * (GROUP // 2) + p) * LANE
            o_ref[:, c0:c0 + LANE] = jnp.concatenate(pieces, axis=0).T.astype(BF16)


def _aug_key_columns():
    a = np.zeros((PB, KV_W), np.float32)
    s = np.arange(PB)
    a[:, 0:3] = s[:, None]
    a[:, 3:6] = 1.0
    for r in range(PB // SEL_BLOCK):
        a[:, 6 + r] = (s // SEL_BLOCK == r)
    return a


def _nsa_prompt256(z, kc, vct, b, t):
    nq = t // PB
    n_seg = t // CMP_STRIDE
    n_sel = t // SEL_BLOCK
    nl = NSA_HEADS * PB
    assert 6 + PB // SEL_BLOCK <= N_AUG and n_sel % SUBLANE == 0 and WINDOW == 2 * PB
    slopes = _alibi_slopes_lanes(PB)
    covert = _cover_matrix(n_seg, n_sel).T.copy()
    kv_col = lambda c0: pl.BlockSpec((t, KV_W), lambda bi, i: (bi, c0 // KV_W))
    return pl.pallas_call(
        _nsa_prompt256_kernel,
        grid=(b, nq),
        in_specs=[pl.BlockSpec((PB, NSA_W), lambda bi, i: (bi * nq + i, C_Q // NSA_W)),
                  pl.BlockSpec((PB, LANE), lambda bi, i: (bi * nq + i, C_GN // LANE)),
                  pl.BlockSpec((None, n_seg, KV_W), lambda bi, i: (bi, 0, 0)),
                  pl.BlockSpec((None, KV_W, n_seg), lambda bi, i: (bi, 0, 0)),
                  kv_col(C_KVS), kv_col(C_KVS + KV_W), kv_col(C_KVW), kv_col(C_KVW + KV_W),
                  _const_spec((PB, KV_W)), _const_spec((1, nl)), _const_spec((n_sel, n_seg))],
        out_specs=pl.BlockSpec((PB, NSA_W), lambda bi, i: (bi * nq + i, 0)),
        out_shape=jax.ShapeDtypeStruct((b * t, NSA_W), BF16),
        scratch_shapes=[pltpu.VMEM((2 * KV_W, nl), BF16), pltpu.VMEM((1, nl), F32),
                        pltpu.VMEM((KVH, VROWS, GROUP * PB), F32), pltpu.VMEM((KVH, n_sel, PB), F32),
                        pltpu.VMEM((n_sel, PB), F32),
                        pltpu.VMEM((nq, PB, KV_W), BF16), pltpu.VMEM((nq, KVH * VROWS, PB), BF16),
                        pltpu.VMEM((nq, PB, KV_W), BF16), pltpu.VMEM((nq, KVH * VROWS, PB), BF16),
                        pltpu.VMEM((PB + SUBLANE, nl), F32), pltpu.VMEM((PB + SUBLANE, nl), F32)],
        compiler_params=_cparams(("parallel", "arbitrary")),
        name="nsa_prompt",
    )(z, z, kc, vct, z, z, z, z, jnp.asarray(_aug_key_columns(), BF16), jnp.asarray(slopes),
      jnp.asarray(covert))


def _alibi_slopes_lanes(width):
    h = np.arange(1, NSA_HEADS + 1, dtype=np.float32)
    return np.repeat(np.exp2(-8.0 * h / NSA_HEADS), width)[None, :].astype(np.float32)


def _cover_matrix(n_cmp_rows, n_sel_rows):
    n = np.arange(n_cmp_rows)[:, None] * CMP_STRIDE
    j = np.arange(n_sel_rows)[None, :] * SEL_BLOCK
    return ((n < j + SEL_BLOCK) & (n + CMP_BLOCK > j)).astype(np.float32)


def _nsa_prompt(z, kc, vct, ks, vst, kw, vwt, b, t):
    nq = t // QB
    n_seg = t // CMP_STRIDE
    n_sel = t // SEL_BLOCK
    nl = NSA_HEADS * QB
    slopes = _alibi_slopes_lanes(QB)
    dl = (np.arange(QB)[None, :] - np.arange(QB)[:, None]).astype(np.float32)
    bias0 = np.tile(dl, (1, NSA_HEADS)) * slopes
    covert = _cover_matrix(n_seg, n_sel).T.copy()
    tile4 = lambda: pl.BlockSpec((None, nq, QB, QB), lambda bi, i: (bi, 0, 0, 0))
    return pl.pallas_call(
        _nsa_prompt_kernel,
        grid=(b, nq),
        in_specs=[pl.BlockSpec((QB, NSA_W), lambda bi, i: (bi * nq + i, C_Q // NSA_W)),
                  pl.BlockSpec((QB, LANE), lambda bi, i: (bi * nq + i, C_GN // LANE)),
                  pl.BlockSpec((None, n_seg, KV_W), lambda bi, i: (bi, 0, 0)),
                  pl.BlockSpec((None, KV_W, n_seg), lambda bi, i: (bi, 0, 0)),
                  tile4(), tile4(), tile4(), tile4(),
                  _const_spec((QB, nl)), _const_spec((1, nl)), _const_spec((n_sel, n_seg))],
        out_specs=pl.BlockSpec((QB, NSA_W), lambda bi, i: (bi * nq + i, 0)),
        out_shape=jax.ShapeDtypeStruct((b * t, NSA_W), BF16),
        scratch_shapes=[pltpu.VMEM((1, nl), F32), pltpu.VMEM((1, nl), F32),
                        pltpu.VMEM((KVH, HD, GROUP * QB), F32), pltpu.VMEM((KVH, n_sel, QB), F32)],
        compiler_params=_cparams(("parallel", "arbitrary")),
        name="nsa_prompt",
    )(z, z, kc, vct, ks, vst, kw, vwt, jnp.asarray(bias0), jnp.asarray(slopes), jnp.asarray(covert))


def _hgrn_gates(pre, lb):
    log_f = jnp.log(lb + (1.0 - lb) * _sigmoid(pre))
    k = (1.0 - lb) * _sigmoid(-pre)
    return log_f, k


def _hgrn_out(o, gate, ng):
    outs = []
    for h in range(HG_H):
        sl = slice(h * HG_D, (h + 1) * HG_D)
        g = gate[:, sl]
        outs.append(_rms(o[:, sl], ng) * (g * _sigmoid(g)))
    return jnp.concatenate(outs, axis=1)


def _hgrn_prompt_kernel(hq_ref, hf_ref, hi_ref, hg_ref, lb_ref, ng_ref, tri_ref, o_ref, st_ref,
                        s_ref, oraw_ref):
    ci = pl.program_id(1)
    tc = hq_ref.shape[0]
    c = HG_CHUNK

    @pl.when(ci == 0)
    def _():
        s_ref[...] = jnp.zeros(s_ref.shape, F32)

    lb = lb_ref[...]
    tril = (lax.broadcasted_iota(jnp.int32, (c, c), 0) >= lax.broadcasted_iota(jnp.int32, (c, c), 1))
    row8 = lax.broadcasted_iota(jnp.int32, (SUBLANE, HG_D), 0)

    n_chunks = tc // c
    q_all = hq_ref[...]
    v_all = hi_ref[...]
    log_f, k_all = _hgrn_gates(hf_ref[...], lb)
    chunks = []
    worst = None
    for g in range(n_chunks):
        rows = slice(g * c, (g + 1) * c)
        bcum = jnp.dot(tri_ref[...], log_f[rows], precision=HI, preferred_element_type=F32)
        e = bcum - bcum[c // 2 - 1:c // 2, :]
        chunks.append((rows, bcum, e))
        worst = jnp.abs(e) if worst is None else jnp.maximum(worst, jnp.abs(e))
    safe = jnp.max(worst) < HG_SAFE_EXP

    @pl.when(safe)
    def _():
        for rows, bcum, e in chunks:
            q, k, v = q_all[rows], k_all[rows], v_all[rows]
            b_last = bcum[c - 1:c, :]
            qt = (q * jnp.exp(e)).astype(BF16)
            kt = (k * jnp.exp(-e)).astype(BF16)
            qb = (q * jnp.exp(bcum)).astype(BF16)
            kh = (k * jnp.exp(b_last - bcum)).astype(BF16)
            dec = jnp.exp(b_last)
            vb = v.astype(BF16)
            for h in range(HG_H):
                sl = slice(h * HG_D, (h + 1) * HG_D)
                a = lax.dot_general(qt[:, sl], kt[:, sl], (((1,), (1,)), ((), ())), preferred_element_type=F32)
                a = jnp.where(tril, a, 0.0).astype(BF16)
                st = s_ref[h]
                o = (lax.dot_general(qb[:, sl], st.astype(BF16), (((1,), (1,)), ((), ())),
                                     preferred_element_type=F32)
                     + jnp.dot(a, vb[:, sl], preferred_element_type=F32))
                oraw_ref[rows, sl] = o
                s_ref[h] = st * dec[:, sl] + lax.dot_general(vb[:, sl], kh[:, sl], (((0,), (0,)), ((), ())),
                                                             preferred_element_type=F32)

    @pl.when(jnp.logical_not(safe))
    def _():
        for h in range(HG_H):
            sl = slice(h * HG_D, (h + 1) * HG_D)

            def tile(ti, carry, sl=sl, h=h):
                r = pl.multiple_of(ti * SUBLANE, SUBLANE)
                q8 = hq_ref[pl.ds(r, SUBLANE), sl]
                v8 = hi_ref[pl.ds(r, SUBLANE), sl]
                lf8, k8 = _hgrn_gates(hf_ref[pl.ds(r, SUBLANE), sl], lb[:, sl])
                f8 = jnp.exp(lf8)
                st = s_ref[h]
                rows_out = []
                for u in range(SUBLANE):
                    vu = jnp.where(row8 == 0, jnp.broadcast_to(v8[u:u + 1, :], (SUBLANE, HG_D)), 0.0)
                    ku = jnp.broadcast_to(k8[u:u + 1, :], (SUBLANE, HG_D))
                    qu = jnp.broadcast_to(q8[u:u + 1, :], (SUBLANE, HG_D))
                    st = st * f8[u:u + 1, :] + lax.dot_general(
                        vu, ku, (((0,), (0,)), ((), ())), precision=HI, preferred_element_type=F32)
                    ou = lax.dot_general(qu, st, (((1,), (1,)), ((), ())), precision=HI,
                                         preferred_element_type=F32)
                    rows_out.append(ou[0:1, :])
                s_ref[h] = st
                oraw_ref[pl.ds(r, SUBLANE), sl] = jnp.concatenate(rows_out, axis=0)
                return carry

            lax.fori_loop(0, tc // SUBLANE, tile, 0)

    o_ref[...] = _hgrn_out(oraw_ref[...], hg_ref[...], ng_ref[...]).astype(BF16)

    @pl.when(ci == pl.num_programs(1) - 1)
    def _():
        for h in range(HG_H):
            st_ref[h] = s_ref[h].T


def _hgrn_prompt(z, lb, ng, b, t, tc):
    nc = t // tc
    col = lambda c0: pl.BlockSpec((tc, HG_W), lambda bi, ci: (bi * nc + ci, c0 // HG_W))
    tri = np.tril(np.ones((HG_CHUNK, HG_CHUNK), np.float32))
    return pl.pallas_call(
        _hgrn_prompt_kernel,
        grid=(b, nc),
        in_specs=[col(C_HQ), col(C_HF), col(C_HI), col(C_HG),
                  _const_spec((1, HG_W)), _const_spec((1, HG_D)), _const_spec((HG_CHUNK, HG_CHUNK))],
        out_specs=[pl.BlockSpec((tc, HG_W), lambda bi, ci: (bi * nc + ci, 0)),
                   pl.BlockSpec((None, HG_H, HG_D, HG_D), lambda bi, ci: (bi, 0, 0, 0))],
        out_shape=[jax.ShapeDtypeStruct((b * t, HG_W), BF16),
                   jax.ShapeDtypeStruct((b, HG_H, HG_D, HG_D), F32)],
        scratch_shapes=[pltpu.VMEM((HG_H, HG_D, HG_D), F32), pltpu.VMEM((tc, HG_W), F32)],
        compiler_params=_cparams(("parallel", "arbitrary")),
        name="hgrn_prompt",
    )(z, z, z, z, lb, ng, jnp.asarray(tri))


def _merge_kernel(on_ref, oh_ref, ga_ref, gb_ref, x_ref, wn_ref, wh_ref, wo_ref, g_ref, o_ref):
    y = (_sigmoid(ga_ref[...]) * jnp.dot(on_ref[...], wn_ref[...], preferred_element_type=F32)
         + _sigmoid(gb_ref[...]) * jnp.dot(oh_ref[...], wh_ref[...], preferred_element_type=F32))
    mix = jnp.dot(y.astype(BF16), wo_ref[...], preferred_element_type=F32)
    o_ref[...] = x_ref[...] + _rms(mix, g_ref[...])


def _resident(shape):
    nd = len(shape)
    return pl.BlockSpec(shape, lambda *_: (0,) * nd, pipeline_mode=pl.Buffered(1))


def _merge(o_nsa, o_hg, z, x, wn, wh, wo, g, tm):
    n = x.shape[0]
    row = lambda w, cb: pl.BlockSpec((tm, w), lambda i: (i, cb))
    return pl.pallas_call(
        _merge_kernel,
        grid=(n // tm,),
        in_specs=[row(NSA_W, 0), row(HG_W, 0), row(D_MODEL, C_GM // D_MODEL), row(D_MODEL, C_GM // D_MODEL + 1),
                  row(D_MODEL, 0), _resident(wn.shape), _resident(wh.shape), _resident(wo.shape),
                  _resident((1, D_MODEL))],
        out_specs=row(D_MODEL, 0),
        out_shape=jax.ShapeDtypeStruct((n, D_MODEL), F32),
        compiler_params=_cparams(("parallel",)),
        name="merge_out_proj",
    )(o_nsa, o_hg, z, z, x, wn, wh, wo, g)


def _mlp_kernel(h_ref, p_ref, wu_ref, wd_ref, wg_ref, wp_ref, g1_ref, g2_ref, g3_ref, o_ref):
    h = h_ref[...]
    xn = _rms(h, g1_ref[...]).astype(BF16)
    ffn = jnp.zeros(h.shape, F32)
    step = D_MODEL
    for c0 in range(0, D_FF, step):
        up = jnp.dot(xn, wu_ref[:, c0:c0 + step], preferred_element_type=F32)
        act = jnp.square(jnp.maximum(up, 0.0)).astype(BF16)
        ffn = ffn + jnp.dot(act, wd_ref[c0:c0 + step, :], preferred_element_type=F32)
    h = h + _rms(ffn, g2_ref[...])
    gate = _sigmoid(jnp.dot(_rms(h, g3_ref[...]).astype(BF16), wg_ref[...], preferred_element_type=F32))
    o_ref[...] = h + gate * jnp.dot(p_ref[...].astype(BF16), wp_ref[...], preferred_element_type=F32)


def _mlp(h, p, wu, wd, wg, wp, g1, g2, g3, tm):
    n = h.shape[0]
    row = lambda w: pl.BlockSpec((tm, w), lambda i: (i, 0))
    gain = _resident((1, D_MODEL))
    return pl.pallas_call(
        _mlp_kernel,
        grid=(n // tm,),
        in_specs=[row(D_MODEL), row(PLE_DIM), _resident(wu.shape), _resident(wd.shape), _resident(wg.shape),
                  _resident(wp.shape), gain, gain, gain],
        out_specs=row(D_MODEL),
        out_shape=jax.ShapeDtypeStruct((n, D_MODEL), F32),
        compiler_params=_cparams(("parallel",)),
        name="mlp_ple",
    )(h, p, wu, wd, wg, wp, g1, g2, g3)


def _prep_w_in(w):
    sizes = (NSA_W, 2 * KV_W, 2 * KV_W, 2 * KV_W, 3 * NSA_HEADS, HG_W, HG_W, HG_W, HG_W)
    q, kvc, kvs, kvw, gn, hq, hf, hi, hg, gm = jnp.split(w, [int(v) for v in np.cumsum(sizes)], axis=1)
    pad = jnp.zeros((w.shape[0], Z_COLS - C_GN - 3 * NSA_HEADS), w.dtype)
    return jnp.concatenate([gm, q, hq, hf, hi, hg, kvc, kvs, kvw, gn, pad], axis=1).astype(BF16)


def _prep_compress(w1k, w2k, w1v, w2v, pos):
    eye = jnp.eye(KVH, dtype=F32)

    def big1(w1):
        t = jnp.einsum('rsdh,kq->rskdqh', w1, eye)
        return t.reshape(CMP_BLOCK // CMP_STRIDE, CMP_STRIDE * KV_W, KVH * CMP_HIDDEN)

    def big2(w2):
        return jnp.einsum('hd,kq->khqd', w2, eye).reshape(KVH * CMP_HIDDEN, KV_W)

    w1 = jnp.stack([big1(w1k), big1(w1v)]).astype(BF16)
    w2 = jnp.stack([big2(w2k), big2(w2v)]).astype(BF16)
    posb = jnp.broadcast_to(pos[:, :, None, :], pos.shape[:2] + (KVH, HD)).reshape(pos.shape[0], 1, -1)
    posb = jnp.broadcast_to(posb, (pos.shape[0], SUBLANE, posb.shape[-1])).astype(BF16)
    return w1, w2, posb


def _kv_rows_t_kernel(ck_ref, cv_ref, sk_ref, sv_ref, wk_ref, wv_ref, ct_ref, st_ref, wt_ref):
    rows = ck_ref.shape[0]
    for src, dst, part in ((ck_ref, ct_ref, 0), (cv_ref, ct_ref, 1), (sk_ref, st_ref, 0), (sv_ref, st_ref, 1),
                           (wk_ref, wt_ref, 0), (wv_ref, wt_ref, 1)):
        for j in range(rows // LANE):
            dst[part * KV_W:(part + 1) * KV_W, j * LANE:(j + 1) * LANE] = src[j * LANE:(j + 1) * LANE, :].T


def _kv_rows_t(z, b, t, rows):
    nt = t // rows
    col = lambda c0: pl.BlockSpec((rows, KV_W), lambda bi, j: (bi * nt + j, c0 // KV_W))
    full = pl.BlockSpec((None, 2 * KV_W, rows), lambda bi, j: (bi, 0, j))
    tail = pl.BlockSpec((None, 2 * KV_W, rows), lambda bi, j: (bi, 0, 0))
    return pl.pallas_call(
        _kv_rows_t_kernel,
        grid=(b, nt),
        in_specs=[col(C_KVC), col(C_KVC + KV_W), col(C_KVS), col(C_KVS + KV_W), col(C_KVW), col(C_KVW + KV_W)],
        out_specs=[full, full, tail],
        out_shape=[jax.ShapeDtypeStruct((b, 2 * KV_W, t), F32), jax.ShapeDtypeStruct((b, 2 * KV_W, t), F32),
                   jax.ShapeDtypeStruct((b, 2 * KV_W, rows), F32)],
        compiler_params=_cparams(("parallel", "arbitrary")),
        name="kv_rows_t",
    )(z, z, z, z, z, z)


def _prep_layer(i, lb_all, w_in, cmp_k_w1, cmp_k_w2, cmp_v_w1, cmp_v_w2, cmp_pos, hg_norm, w_branch_nsa,
                w_branch_hgrn, w_out, norm_pre_mix, norm_post_mix, norm_pre_mlp, norm_post_mlp, w_mlp_up,
                w_mlp_down, norm_ple, w_ple_gate, w_ple_proj):
    w1, w2, posb = _prep_compress(cmp_k_w1[i], cmp_k_w2[i], cmp_v_w1[i], cmp_v_w2[i], cmp_pos[i])
    row = lambda a: a[i].reshape(1, -1).astype(F32)
    return {
        'w_in': _prep_w_in(w_in[i]), 'cmp_w1': w1, 'cmp_w2': w2, 'cmp_pos': posb,
        'hg_lb': lb_all[i].reshape(1, HG_W), 'hg_norm': row(hg_norm),
        'w_bn': w_branch_nsa[i].astype(BF16), 'w_bh': w_branch_hgrn[i].astype(BF16), 'w_out': w_out[i].astype(BF16),
        'w_up': w_mlp_up[i].astype(BF16), 'w_down': w_mlp_down[i].astype(BF16),
        'w_gate': w_ple_gate[i].astype(BF16), 'w_proj': w_ple_proj[i].astype(BF16),
        'g_pre_mix': row(norm_pre_mix), 'g_post_mix': row(norm_post_mix), 'g_pre_mlp': row(norm_pre_mlp),
        'g_post_mlp': row(norm_post_mlp), 'g_ple': row(norm_ple),
    }


def _key_tiles(rows, b, t):
    r = rows.reshape(b, t // PB, PB, 2 * KV_W)
    return r[..., :KV_W].astype(BF16), jnp.swapaxes(r[..., KV_W:], 2, 3).astype(BF16)


def _layer_prompt(x, p, lw, b, t):
    n = b * t
    z = _norm_matmul(x, lw['g_pre_mix'], lw['w_in'], tm=min(1024, n), tn=Z_COLS // 4)
    wb = min(WINDOW, t)
    kv_c, kv_s, kv_w = _kv_rows_t(z, b, t, wb)
    kc, vc = _compress_prompt(z, lw['cmp_w1'], lw['cmp_w2'], lw['cmp_pos'], b, t)
    vct = jnp.swapaxes(vc, 1, 2)
    o_nsa = _nsa_prompt256(z, kc, vct, b, t)
    o_hg, st = _hgrn_prompt(z, lw['hg_lb'], lw['hg_norm'], b, t, tc=min(256, t))
    h1 = _merge(o_nsa, o_hg, z, x, lw['w_bn'], lw['w_bh'], lw['w_out'], lw['g_post_mix'], tm=min(512, n))
    h2 = _mlp(h1, p, lw['w_up'], lw['w_down'], lw['w_gate'], lw['w_proj'],
              lw['g_pre_mlp'], lw['g_post_mlp'], lw['g_ple'], tm=min(512, n))
    kv6 = lambda a: a.reshape(b, 2, KVH, HD, a.shape[-1]).transpose(0, 4, 1, 2, 3)
    return h2, kv6(kv_c), kv6(kv_s), kv6(kv_w), st


SEG_PITCH = 24


def _page_fetch(pt_ref, cache_ref, buf_ref, sem):
    b = pl.program_id(0)
    n_pages = pt_ref.shape[1]
    slot = b % 2

    def copy(seq, sl, p):
        return pltpu.make_async_copy(cache_ref.at[pt_ref[seq, p]], buf_ref.at[sl, p], sem.at[sl])

    def start(seq, sl):
        lax.fori_loop(0, n_pages, lambda p, c: (copy(seq, sl, p).start(), c)[1], 0)

    @pl.when(b == 0)
    def _():
        start(0, 0)

    @pl.when(b + 1 < pl.num_programs(0))
    def _():
        start(b + 1, 1 - slot)

    def wait():
        lax.fori_loop(0, n_pages, lambda p, c: (copy(b, slot, p).wait(), c)[1], 0)

    return slot, wait


def _compress_sample_kernel(pt_ref, cache_ref, w1_ref, w2_ref, pos_ref, kc_ref, vc_ref, buf_ref, rows_ref, sem):
    n_pages, page = pt_ref.shape[1], buf_ref.shape[-1]
    n_seg = n_pages * page // CMP_STRIDE
    slot, wait = _page_fetch(pt_ref, cache_ref, buf_ref, sem)
    wait()
    segs_per_page = page // CMP_STRIDE
    for kv, out_ref in ((0, kc_ref), (1, vc_ref)):
        for p in range(n_pages):
            rows = buf_ref[slot, p, kv].T
            for g in range(segs_per_page):
                r0 = (p * segs_per_page + g) * SEG_PITCH
                rows_ref[kv, r0:r0 + CMP_STRIDE, :] = rows[g * CMP_STRIDE:(g + 1) * CMP_STRIDE, :]
        load = lambda s, kv=kv: rows_ref[kv, pl.ds(s, n_seg, stride=SEG_PITCH), :]
        out_ref[...] = _compress_math(load, n_seg, w1_ref, w2_ref, pos_ref, kv).astype(BF16)


def _compress_sample(page_table, cache_t, w1, w2, pos):
    bs, n_pages = page_table.shape
    page = cache_t.shape[-1]
    n_seg = n_pages * page // CMP_STRIDE
    const = lambda shape: pl.BlockSpec(shape, lambda i, pt: (0,) * len(shape), pipeline_mode=pl.Buffered(1))
    out = pl.BlockSpec((None, n_seg, KV_W), lambda i, pt: (i, 0, 0))
    return pl.pallas_call(
        _compress_sample_kernel,
        grid_spec=pltpu.PrefetchScalarGridSpec(
            num_scalar_prefetch=1, grid=(bs,),
            in_specs=[pl.BlockSpec(memory_space=pl.ANY), const(w1.shape), const(w2.shape), const(pos.shape)],
            out_specs=[out, out],
            scratch_shapes=[pltpu.VMEM((2, n_pages, 2, KV_W, page), F32),
                            pltpu.VMEM((2, n_seg * SEG_PITCH, KV_W), F32), pltpu.SemaphoreType.DMA((2,))]),
        out_shape=[jax.ShapeDtypeStruct((bs, n_seg, KV_W), BF16)] * 2,
        compiler_params=_cparams(("arbitrary",)),
        name="compress_sample",
    )(page_table, cache_t, w1, w2, pos)


_NT = (((1,), (1,)), ((), ()))


def _nsa_sample_kernel(pt_ref, qbd_ref, gl_ref, kc_ref, vc_ref, snew_ref, cwin_ref, wnew_ref, csel_ref,
                       slope_ref, tq_ref, tq8_ref, cover_ref, gsum_ref, gexp_ref, o_ref,
                       buf_ref, s_ref, sem):
    n_pages, page = pt_ref.shape[1], buf_ref.shape[-1]
    past = n_pages * page
    n_seg = kc_ref.shape[0]
    npad = cover_ref.shape[1]
    n_sel = past // SEL_BLOCK + 1
    nr = qbd_ref.shape[0]
    wb = cwin_ref.shape[-1]
    slot, wait = _page_fetch(pt_ref, csel_ref, buf_ref, sem)

    qb = (qbd_ref[...] * SCALE).astype(BF16)
    slope = slope_ref[...]
    qpos = tq_ref[...] + float(past)
    zeros_pad = jnp.zeros((QB - snew_ref.shape[0], KV_W), F32)
    t_new = lax.broadcasted_iota(jnp.int32, (1, QB), 1).astype(F32)

    sc = lax.dot_general(qb, kc_ref[...], _NT, preferred_element_type=F32)
    n_i = lax.broadcasted_iota(jnp.int32, (1, n_seg), 1)
    dcmp = qpos - (n_i * CMP_STRIDE + (CMP_BLOCK - 1)).astype(F32)
    okc = dcmp >= 0.0
    sc = jnp.where(okc, sc - slope * dcmp, NEG)
    mc = jnp.max(sc, axis=-1, keepdims=True)
    pc = jnp.where(okc, jnp.exp(sc - mc), 0.0)
    lc = jnp.sum(pc, axis=-1, keepdims=True)
    pc = pc * (1.0 / jnp.where(lc > 0.0, lc, 1.0))
    o_cmp = jnp.dot(pc.astype(BF16), vc_ref[...], preferred_element_type=F32)

    psum = jnp.dot(gsum_ref[...], pc, precision=HI, preferred_element_type=F32)
    imp = jnp.dot(psum, cover_ref[...], precision=HI, preferred_element_type=F32)
    nq8 = gsum_ref.shape[0]
    jblk = lax.broadcasted_iota(jnp.int32, (nq8, npad), 1)
    qblk = (tq8_ref[...].astype(jnp.int32) + past) // SEL_BLOCK
    back = qblk - jblk
    visible = back >= 0
    forced = (jblk == 0) | (visible & (back < SEL_LOCAL))
    score = jnp.where(forced, 1e9, jnp.where(visible, imp, -1e9))
    rank = jnp.zeros((nq8, npad), F32)
    for jp in range(n_sel):
        col = score[:, jp:jp + 1]
        ge = jnp.where(col >= score, 1.0, 0.0)
        gt = jnp.where(col > score, 1.0, 0.0)
        rank = rank + jnp.where(jblk > jp, ge, gt)
    sel8 = jnp.where(visible, jnp.where(rank < float(min(SEL_TOP, n_sel)), 1.0, 0.0), 0.0)
    negb = (jnp.dot(gexp_ref[...], sel8, precision=HI, preferred_element_type=F32) - 1.0) * (-NEG)

    def tile_bias(jt):
        lane = lax.broadcasted_iota(jnp.int32, (nr, QB), 1)
        return jnp.where(lane < SEL_BLOCK, negb[:, 2 * jt:2 * jt + 1], negb[:, 2 * jt + 1:2 * jt + 2])

    wait()
    ppc = 4
    ck = ppc * page
    tiles_per_page = page // QB

    def chunk_t(c, kv):
        return jnp.concatenate([buf_ref[slot, c * ppc + u, kv] for u in range(ppc)], axis=1).astype(BF16)

    for c in range(n_pages // ppc):
        s = jnp.dot(qb, chunk_t(c, 0), preferred_element_type=F32)
        kpos = (lax.broadcasted_iota(jnp.int32, (1, ck), 1) + c * ck).astype(F32)
        bias = jnp.concatenate([tile_bias(c * ppc * tiles_per_page + u) for u in range(ppc * tiles_per_page)],
                               axis=1)
        s_ref[:, c * ck:(c + 1) * ck] = s - slope * (qpos - kpos) + bias
    knew = jnp.concatenate([snew_ref[:, 0:KV_W], zeros_pad], axis=0).astype(BF16)
    vnew = jnp.concatenate([snew_ref[:, KV_W:2 * KV_W], zeros_pad], axis=0).astype(BF16)
    dnew = tq_ref[...] - t_new
    s = lax.dot_general(qb, knew, _NT, preferred_element_type=F32)
    s_ref[:, past:past + QB] = jnp.where(dnew >= 0.0, s - slope * dnew + tile_bias(past // QB), NEG)
    s_all = s_ref[...]
    ms = jnp.max(s_all, axis=-1, keepdims=True)
    ps = jnp.exp(s_all - ms)
    ls = jnp.sum(ps, axis=-1, keepdims=True)
    psb = ps.astype(BF16)
    o_sel = jnp.dot(psb[:, past:past + QB], vnew, preferred_element_type=F32)
    for c in range(n_pages // ppc):
        o_sel = o_sel + lax.dot_general(psb[:, c * ck:(c + 1) * ck], chunk_t(c, 1), _NT,
                                        preferred_element_type=F32)
    o_sel = o_sel * (1.0 / ls)

    s1 = jnp.dot(qb, cwin_ref[0].astype(BF16), preferred_element_type=F32)
    d1 = float(wb) + tq_ref[...] - lax.broadcasted_iota(jnp.int32, (1, wb), 1).astype(F32)
    ok1 = d1 < float(WINDOW)
    s1 = jnp.where(ok1, s1 - slope * d1, NEG)
    wk = jnp.concatenate([wnew_ref[:, 0:KV_W], zeros_pad], axis=0).astype(BF16)
    wv = jnp.concatenate([wnew_ref[:, KV_W:2 * KV_W], zeros_pad], axis=0).astype(BF16)
    ok2 = dnew >= 0.0
    s2 = jnp.where(ok2, lax.dot_general(qb, wk, _NT, preferred_element_type=F32) - slope * dnew, NEG)
    mw = jnp.maximum(jnp.max(s1, axis=-1, keepdims=True), jnp.max(s2, axis=-1, keepdims=True))
    p1 = jnp.where(ok1, jnp.exp(s1 - mw), 0.0)
    p2 = jnp.where(ok2, jnp.exp(s2 - mw), 0.0)
    lw_ = jnp.sum(p1, axis=-1, keepdims=True) + jnp.sum(p2, axis=-1, keepdims=True)
    o_win = (lax.dot_general(p1.astype(BF16), cwin_ref[1].astype(BF16), _NT, preferred_element_type=F32)
             + jnp.dot(p2.astype(BF16), wv, preferred_element_type=F32)) * (1.0 / lw_)

    sig = _sigmoid(gl_ref[...])
    o_ref[...] = sig[:, 0:1] * o_cmp + sig[:, 1:2] * o_sel + sig[:, 2:3] * o_win


def _nsa_sample(page_table, qbd, gl, kc, vc, snew, cwin, wnew, csel, ts):
    bs, n_pages = page_table.shape
    page = csel.shape[-1]
    past = n_pages * page
    n_seg = kc.shape[1]
    n_sel = past // SEL_BLOCK + 1
    npad = -(-(n_sel + 1) // LANE) * LANE
    nr = KVH * GROUP * ts
    wb = cwin.shape[-1]
    r = np.arange(nr)
    slope = np.exp2(-8.0 * ((r // ts) + 1) / NSA_HEADS).astype(np.float32)[:, None]
    tq = (r % ts).astype(np.float32)[:, None]
    r8 = np.arange(KVH * ts)
    tq8 = (r8 % ts).astype(np.float32)[:, None]
    cover = np.zeros((n_seg, npad), np.float32)
    cover[:, :n_sel] = _cover_matrix(n_seg, n_sel)
    gsum = ((r[None, :] // (GROUP * ts) == r8[:, None] // ts) & (r[None, :] % ts == r8[:, None] % ts)).astype(np.float32)
    consts = [slope, tq, tq8, cover, gsum, gsum.T.copy()]
    const = lambda shape: pl.BlockSpec(shape, lambda i, pt: (0,) * len(shape), pipeline_mode=pl.Buffered(1))
    per = lambda *s: pl.BlockSpec((None,) + s, lambda i, pt: (i,) + (0,) * len(s))
    return pl.pallas_call(
        _nsa_sample_kernel,
        grid_spec=pltpu.PrefetchScalarGridSpec(
            num_scalar_prefetch=1, grid=(bs,),
            in_specs=[per(nr, KV_W), per(nr, LANE), per(n_seg, KV_W), per(n_seg, KV_W), per(SUBLANE, 2 * KV_W),
                      per(2, KV_W, wb), per(SUBLANE, 2 * KV_W), pl.BlockSpec(memory_space=pl.ANY)]
                     + [const(c.shape) for c in consts],
            out_specs=per(nr, KV_W),
            scratch_shapes=[pltpu.VMEM((2, n_pages, 2, KV_W, page), F32), pltpu.VMEM((nr, past + QB), F32),
                            pltpu.SemaphoreType.DMA((2,))]),
        out_shape=jax.ShapeDtypeStruct((bs, nr, KV_W), F32),
        compiler_params=_cparams(("arbitrary",)),
        name="nsa_sample",
    )(page_table, qbd, gl, kc, vc, snew, cwin, wnew, csel, *[jnp.asarray(c) for c in consts])


def _hgrn_sample_kernel(ts, hq_ref, hf_ref, hi_ref, hg_ref, lb_ref, ng_ref, s0_ref, o_ref, s1_ref):
    rows = hq_ref.shape[0]
    q = hq_ref[...]
    v = hi_ref[...]
    log_f, k = _hgrn_gates(hf_ref[...], lb_ref[...])
    tloc = lax.broadcasted_iota(jnp.int32, (rows, HG_W), 0) % ts
    up = lambda a, d: pltpu.roll(a, d, axis=0)
    down = lambda a, d: pltpu.roll(a, rows - d, axis=0)

    bcum = log_f
    for d in range(1, ts):
        bcum = bcum + jnp.where(tloc >= d, up(log_f, d), 0.0)
    b_last = bcum
    for d in range(1, ts):
        b_last = jnp.where(tloc == ts - 1 - d, down(bcum, d), b_last)

    o_intra = [jnp.zeros((rows, HG_D), F32) for _ in range(HG_H)]
    for d in range(ts):
        kd, bd, vd = (k, bcum, v) if d == 0 else (up(k, d), up(bcum, d), up(v, d))
        w = jnp.where(tloc >= d, q * kd * jnp.exp(jnp.where(tloc >= d, bcum - bd, 0.0)), 0.0)
        for h in range(HG_H):
            sl = slice(h * HG_D, (h + 1) * HG_D)
            o_intra[h] = o_intra[h] + jnp.sum(w[:, sl], axis=-1, keepdims=True) * vd[:, sl]

    qb = q * jnp.exp(bcum)
    kh = k * jnp.exp(b_last - bcum)
    per_tile = SUBLANE // ts
    row8 = lax.broadcasted_iota(jnp.int32, (SUBLANE, HG_D), 0) // ts
    tiles = []
    for j in range(rows // SUBLANE):
        r8 = slice(j * SUBLANE, (j + 1) * SUBLANE)
        heads = []
        for h in range(HG_H):
            sl = slice(h * HG_D, (h + 1) * HG_D)
            o_inter = jnp.zeros((SUBLANE, HG_D), F32)
            for u in range(per_tile):
                seq = j * per_tile + u
                mine = row8 == u
                s0 = s0_ref[seq, h]
                o_inter = o_inter + jnp.dot(jnp.where(mine, qb[r8, sl], 0.0).astype(BF16), s0.astype(BF16),
                                            preferred_element_type=F32)
                upd = lax.dot_general(jnp.where(mine, kh[r8, sl], 0.0).astype(BF16), v[r8, sl].astype(BF16),
                                      (((0,), (0,)), ((), ())), preferred_element_type=F32)
                r_last = j * SUBLANE + u * ts + ts - 1
                dec = jnp.exp(bcum[r_last:r_last + 1, sl])
                s1_ref[seq, h] = jnp.broadcast_to(dec, (HG_D, HG_D)).T * s0 + upd
            heads.append(o_inter + o_intra[h][r8, :])
        tiles.append(jnp.concatenate(heads, axis=1))
    o = jnp.concatenate(tiles, axis=0)
    o_ref[...] = _hgrn_out(o, hg_ref[...], ng_ref[...]).astype(BF16)


def _hgrn_sample(z, lb, ng, s0, bs, ts, nb):
    rows = nb * ts
    col = lambda c0: pl.BlockSpec((rows, HG_W), lambda i: (i, c0 // HG_W))
    st = pl.BlockSpec((nb, HG_H, HG_D, HG_D), lambda i: (i, 0, 0, 0))
    return pl.pallas_call(
        functools.partial(_hgrn_sample_kernel, ts),
        grid=(bs // nb,),
        in_specs=[col(C_HQ), col(C_HF), col(C_HI), col(C_HG), _const_spec((1, HG_W)), _const_spec((1, HG_D)), st],
        out_specs=[pl.BlockSpec((rows, HG_W), lambda i: (i, 0)), st],
        out_shape=[jax.ShapeDtypeStruct((bs * ts, HG_W), BF16),
                   jax.ShapeDtypeStruct((bs, HG_H, HG_D, HG_D), F32)],
        compiler_params=_cparams(("parallel",)),
        name="hgrn_sample",
    )(z, z, z, z, lb, ng, s0)


def _layer_sample(x, p, cache_cmp, cache_sel, cache_win, state, page_table, lw, bs, ts):
    n = bs * ts
    assert SUBLANE % ts == 0 and n % SUBLANE == 0
    z = _norm_matmul(x, lw['g_pre_mix'], lw['w_in'], tm=min(512, n), tn=512)
    kv_c = z[:, C_KVC:C_KVC + 2 * KV_W]
    kv_s = z[:, C_KVS:C_KVS + 2 * KV_W]
    kv_w = z[:, C_KVW:C_KVW + 2 * KV_W]

    rows_last = lambda a: jnp.transpose(a, (0, 2, 3, 4, 1)).reshape(a.shape[0], 2, KV_W, a.shape[1])
    kc, vc = _compress_sample(page_table, rows_last(cache_cmp), lw['cmp_w1'], lw['cmp_w2'], lw['cmp_pos'])

    eye = jnp.eye(KVH, dtype=F32)
    q5 = z[:, C_Q:C_Q + NSA_W].reshape(bs, ts, KVH, GROUP, HD).transpose(0, 2, 3, 1, 4)
    qbd = jnp.einsum('bkgtd,kq->bkgtqd', q5, eye).reshape(bs, KVH * GROUP * ts, KV_W)
    g5 = z[:, C_GN:C_GN + 3 * NSA_HEADS].reshape(bs, ts, KVH, GROUP, 3).transpose(0, 2, 3, 1, 4)
    gl = jnp.pad(g5.reshape(bs, KVH * GROUP * ts, 3), ((0, 0), (0, 0), (0, LANE - 3)))
    pad_rows = lambda a: jnp.pad(a.reshape(bs, ts, 2 * KV_W), ((0, 0), (0, SUBLANE - ts), (0, 0)))
    o_rows = _nsa_sample(page_table, qbd, gl, kc, vc, pad_rows(kv_s), rows_last(cache_win), pad_rows(kv_w),
                         rows_last(cache_sel), ts)
    o6 = o_rows.reshape(bs, KVH, GROUP, ts, KVH, HD)
    o_nsa = jnp.stack([o6[:, kvh, :, :, kvh, :] for kvh in range(KVH)], axis=1)
    o_nsa = o_nsa.transpose(0, 3, 1, 2, 4).reshape(n, NSA_W).astype(BF16)

    o_hg, st = _hgrn_sample(z, lw['hg_lb'], lw['hg_norm'], state, bs, ts, nb=min(8, bs))
    h1 = _merge(o_nsa, o_hg, z, x, lw['w_bn'], lw['w_bh'], lw['w_out'], lw['g_post_mix'], tm=min(512, n))
    h2 = _mlp(h1, p, lw['w_up'], lw['w_down'], lw['w_gate'], lw['w_proj'],
              lw['g_pre_mlp'], lw['g_post_mlp'], lw['g_ple'], tm=min(512, n))
    kv6 = lambda a: a.reshape(bs, ts, 2, KVH, HD)
    win_buf = jnp.concatenate([cache_win, kv6(kv_w)], axis=1)[:, ts:]
    return h2, kv6(kv_c), kv6(kv_s), win_buf, st


def kernel(x_prompt, x_sample, cache_cmp_kv, cache_sel_kv, cache_win_kv, state_hgrn, page_table, p_prompt,
           p_sample, w_in, cmp_k_w1, cmp_k_w2, cmp_v_w1, cmp_v_w2, cmp_pos, hg_lb_logits, hg_norm, w_branch_nsa,
           w_branch_hgrn, w_out, norm_pre_mix, norm_post_mix, norm_pre_mlp, norm_post_mlp, w_mlp_up, w_mlp_down,
           norm_ple, w_ple_gate, w_ple_proj):
    depth = w_in.shape[0]
    b, t, d = x_prompt.shape
    bs, ts, _ = x_sample.shape
    lb_all = jnp.cumsum(jax.nn.softmax(hg_lb_logits.astype(F32), axis=0), axis=0)
    h_p = x_prompt.reshape(b * t, d)
    h_s = x_sample.reshape(bs * ts, d)
    outs = [[] for _ in range(8)]
    for i in range(depth):
        lw = _prep_layer(i, lb_all, w_in, cmp_k_w1, cmp_k_w2, cmp_v_w1, cmp_v_w2, cmp_pos, hg_norm, w_branch_nsa,
                         w_branch_hgrn, w_out, norm_pre_mix, norm_post_mix, norm_pre_mlp, norm_post_mlp, w_mlp_up,
                         w_mlp_down, norm_ple, w_ple_gate, w_ple_proj)
        h_p, *res_p = _layer_prompt(h_p, p_prompt[i].reshape(b * t, -1), lw, b, t)
        h_s, *res_s = _layer_sample(h_s, p_sample[i].reshape(bs * ts, -1), cache_cmp_kv[i], cache_sel_kv[i],
                                    cache_win_kv[i], state_hgrn[i], page_table, lw, bs, ts)
        for lst, v in zip(outs, res_p + res_s):
            lst.append(v)
    return (h_p.reshape(b, t, d), h_s.reshape(bs, ts, d)) + tuple(jnp.stack(lst, axis=0) for lst in outs)
```

```python
import functools
import math

import numpy as np
import jax
import jax.numpy as jnp
from jax import lax
from jax.experimental import pallas as pl
from jax.experimental.pallas import tpu as pltpu

F32 = jnp.float32
BF16 = jnp.bfloat16

D_MODEL = 1024
NSA_HEADS = 8
KVH = 2
GROUP = NSA_HEADS // KVH
HD = 64
NSA_W = NSA_HEADS * HD
KV_W = KVH * HD
CMP_BLOCK = 32
CMP_STRIDE = 16
CMP_HIDDEN = 2 * HD
SEL_BLOCK = 64
SEL_TOP = 16
SEL_LOCAL = 2
WINDOW = 512
HG_W = 512
HG_H = 4
HG_D = 128
HG_CHUNK = 64
D_FF = 4 * D_MODEL
PLE_DIM = 256
RMS_EPS = 1e-6
NEG = -1e30
M_INIT = -1e20
SCALE = HD ** -0.5
HG_SAFE_EXP = 60.0

LANE = 128
SUBLANE = 8
VMEM_LIMIT = 48 * 1024 * 1024

C_GM = 0
C_Q = 2048
C_HQ = 2560
C_HF = 3072
C_HI = 3584
C_HG = 4096
C_KVC = 4608
C_KVS = 4864
C_KVW = 5120
C_GN = 5376
Z_COLS = 5632
QB = 128
HI = lax.Precision.HIGHEST


def _cparams(sem, vmem=VMEM_LIMIT):
    return pltpu.CompilerParams(dimension_semantics=sem, vmem_limit_bytes=vmem)


def _rms(x, g):
    return x * lax.rsqrt(jnp.mean(x * x, axis=-1, keepdims=True) + RMS_EPS) * g


def _sigmoid(x):
    return 1.0 / (1.0 + jnp.exp(-x))


def _gelu_tanh(x):
    return 0.5 * x * (1.0 + jnp.tanh(math.sqrt(2.0 / math.pi) * (x + 0.044715 * (x * x * x))))


def _const_spec(shape):
    nd = len(shape)
    return pl.BlockSpec(shape, lambda *_: (0,) * nd)


def _norm_matmul_kernel(x_ref, g_ref, w_ref, o_ref, xn_ref):
    @pl.when(pl.program_id(1) == 0)
    def _():
        xn_ref[...] = _rms(x_ref[...], g_ref[...]).astype(BF16)

    tn = o_ref.shape[1]
    col = pl.multiple_of(pl.program_id(1) * tn, LANE)
    o_ref[...] = jnp.dot(xn_ref[...], w_ref[:, pl.ds(col, tn)], preferred_element_type=F32)


def _norm_matmul(x, g, w, tm, tn):
    n, d = x.shape
    c = w.shape[1]
    return pl.pallas_call(
        _norm_matmul_kernel,
        grid=(n // tm, c // tn),
        in_specs=[pl.BlockSpec((tm, d), lambda i, j: (i, 0)),
                  pl.BlockSpec((1, d), lambda i, j: (0, 0)),
                  pl.BlockSpec((d, c), lambda i, j: (0, 0), pipeline_mode=pl.Buffered(1))],
        out_specs=pl.BlockSpec((tm, tn), lambda i, j: (i, j)),
        out_shape=jax.ShapeDtypeStruct((n, c), F32),
        scratch_shapes=[pltpu.VMEM((tm, d), BF16)],
        compiler_params=_cparams(("parallel", "arbitrary")),
        name="norm_in_proj",
    )(x, g, w)


def _compress_math(load_rows, n_seg, w1_ref, w2_ref, pos_ref, kv):
    x = jnp.concatenate([load_rows(s) for s in range(CMP_STRIDE)], axis=1).astype(BF16)
    h0 = jnp.dot(x, w1_ref[kv, 0], preferred_element_type=F32)
    h1 = jnp.dot(x, w1_ref[kv, 1], preferred_element_type=F32)
    posb = (jnp.dot(pos_ref[0], w1_ref[kv, 0], preferred_element_type=F32)
            + jnp.dot(pos_ref[1], w1_ref[kv, 1], preferred_element_type=F32))
    hid = h0 + pltpu.roll(h1, n_seg - 1, axis=0) + posb[0:1]
    return jnp.dot(_gelu_tanh(hid).astype(BF16), w2_ref[kv], preferred_element_type=F32)


def _compress_prompt_kernel(xk_ref, xv_ref, w1_ref, w2_ref, pos_ref, kc_ref, vc_ref):
    n_seg = xk_ref.shape[0] // CMP_STRIDE
    for kv, x_ref, out_ref in ((0, xk_ref, kc_ref), (1, xv_ref, vc_ref)):
        load = lambda s, x_ref=x_ref: x_ref[pl.ds(s, n_seg, stride=CMP_STRIDE), :]
        out_ref[...] = _compress_math(load, n_seg, w1_ref, w2_ref, pos_ref, kv).astype(BF16)


def _compress_prompt(z, w1, w2, pos, b, t):
    n_seg = t // CMP_STRIDE
    return pl.pallas_call(
        _compress_prompt_kernel,
        grid=(b,),
        in_specs=[pl.BlockSpec((t, KV_W), lambda i: (i, C_KVC // KV_W)),
                  pl.BlockSpec((t, KV_W), lambda i: (i, C_KVC // KV_W + 1)),
                  _const_spec(w1.shape), _const_spec(w2.shape), _const_spec(pos.shape)],
        out_specs=[pl.BlockSpec((None, n_seg, KV_W), lambda i: (i, 0, 0)),
                   pl.BlockSpec((None, n_seg, KV_W), lambda i: (i, 0, 0))],
        out_shape=[jax.ShapeDtypeStruct((b, n_seg, KV_W), BF16),
                   jax.ShapeDtypeStruct((b, n_seg, KV_W), BF16)],
        compiler_params=_cparams(("parallel",)),
        name="compress_prompt",
    )(z, z, w1, w2, pos)


PB = 256
N_AUG = 16
VROWS = HD + 16
LOG2E = math.log2(math.e)


def _split3(x):
    hi = x.astype(BF16).astype(F32)
    r = x - hi
    mid = r.astype(BF16).astype(F32)
    return hi, mid, r - mid


def _nsa_prompt256_kernel(q_ref, gn_ref, kc_ref, vct_ref, zks_ref, zvs_ref, zkw_ref, zvw_ref,
                          augc_ref, slopes_ref, covert_ref, o_ref,
                          qa_ref, m_ref, acc_ref, sel_ref, score_ref, ks_ref, vst_ref, kw_ref, vwt_ref,
                          sa_ref, sb_ref):
    i = pl.program_id(1)
    t0 = i * PB

    @pl.when(i == 0)
    def _():
        ones_rows = jnp.where(lax.broadcasted_iota(jnp.int32, (VROWS - HD, PB), 0) == 0, 1.0, 0.0)
        for zk, zv, k_dst, vt_dst in ((zks_ref, zvs_ref, ks_ref, vst_ref), (zkw_ref, zvw_ref, kw_ref, vwt_ref)):
            for j in range(k_dst.shape[0]):
                k_dst[j] = zk[j * PB:(j + 1) * PB, :].astype(BF16)
                vt = zv[j * PB:(j + 1) * PB, :].T
                vt_dst[j] = jnp.concatenate(
                    [piece for kvh in range(KVH) for piece in (vt[kvh * HD:(kvh + 1) * HD, :], ones_rows)],
                    axis=0).astype(BF16)
    n_seg = kc_ref.shape[0]
    n_sel = covert_ref.shape[0]
    nl = NSA_HEADS * PB
    half = GROUP * PB
    slopes2 = slopes_ref[...] * LOG2E
    tlane = (lax.broadcasted_iota(jnp.int32, (1, nl), 1) % PB).astype(F32)
    blocks_per_tile = PB // SEL_BLOCK

    q = q_ref[...] * (SCALE * LOG2E)
    zero = jnp.zeros((HD, PB), F32)
    cols = []
    for kvh in range(KVH):
        for p in range(GROUP // 2):
            c0 = (kvh * (GROUP // 2) + p) * LANE
            blk = q[:, c0:c0 + LANE].T
            for hh in range(2):
                piece = blk[hh * HD:(hh + 1) * HD, :]
                cols.append(jnp.concatenate([piece, zero] if kvh == 0 else [zero, piece], axis=0))
    qbdt = jnp.concatenate(cols, axis=1).astype(BF16)
    qa_ref[0:KV_W, :] = qbdt
    qa_ref[KV_W + N_AUG:2 * KV_W, :] = jnp.zeros((KV_W - N_AUG, nl), BF16)

    lane = lax.broadcasted_iota(jnp.int32, (PB, PB), 1)
    sub = lax.broadcasted_iota(jnp.int32, (PB, PB), 0)
    causal_bias = jnp.where(lane >= sub, 0.0, NEG)
    edge_bias = jnp.where(lane < sub, 0.0, NEG)

    sc = jnp.dot(kc_ref[...], qbdt, preferred_element_type=F32)
    n_i = lax.broadcasted_iota(jnp.int32, (n_seg, PB), 0)
    t_i = lax.broadcasted_iota(jnp.int32, (n_seg, PB), 1)
    d1 = (t0 + t_i - (n_i * CMP_STRIDE + (CMP_BLOCK - 1))).astype(F32)
    dc = jnp.concatenate([d1] * NSA_HEADS, axis=1)
    okc = dc >= 0.0
    sc = jnp.where(okc, sc - slopes2 * dc, NEG)
    mc = jnp.max(sc, axis=0, keepdims=True)
    pc = jnp.where(okc, jnp.exp2(sc - mc), 0.0)
    lc = jnp.sum(pc, axis=0, keepdims=True)
    pc = pc * (1.0 / jnp.where(lc > 0.0, lc, 1.0))
    pcb = pc.astype(BF16)
    ocmp = [jnp.dot(vct_ref[kvh * HD:(kvh + 1) * HD, :], pcb[:, kvh * half:(kvh + 1) * half],
                    preferred_element_type=F32) for kvh in range(KVH)]

    jblk = lax.broadcasted_iota(jnp.int32, (n_sel, PB), 0)
    qblk = (t0 + lax.broadcasted_iota(jnp.int32, (n_sel, PB), 1)) // SEL_BLOCK
    back = qblk - jblk
    visible = back >= 0
    forced = (jblk == 0) | (visible & (back < SEL_LOCAL))
    n_groups = jnp.minimum((t0 + PB - 1) // SEL_BLOCK // SUBLANE + 1, n_sel // SUBLANE)
    for kvh in range(KVH):
        psum = pc[:, kvh * half:kvh * half + PB]
        for g in range(1, GROUP):
            psum = psum + pc[:, kvh * half + g * PB:kvh * half + (g + 1) * PB]
        imp = jnp.dot(covert_ref[...], psum, precision=HI, preferred_element_type=F32)
        score = jnp.where(forced, 1e9, jnp.where(visible, imp, -1e9))
        score_ref[...] = score

        def rank_group(gi, rank):
            rows8 = score_ref[pl.ds(pl.multiple_of(gi * SUBLANE, SUBLANE), SUBLANE), :]
            for u in range(SUBLANE):
                row = rows8[u:u + 1, :]
                ge = jnp.where(row >= score, 1.0, 0.0)
                gt = jnp.where(row > score, 1.0, 0.0)
                rank = rank + jnp.where(jblk > gi * SUBLANE + u, ge, gt)
            return rank

        rank = lax.fori_loop(0, n_groups, rank_group, jnp.zeros((n_sel, PB), F32))
        sel_ref[kvh] = jnp.where(visible, jnp.where(rank < float(min(SEL_TOP, n_sel)), 0.0, NEG), NEG)

    sl3 = _split3(slopes2)

    def scores(k_ref, jt, use_sel, dst_ref):
        off = (t0 - jt * PB).astype(F32)
        c3 = _split3(-slopes2 * (tlane + off))
        rows = list(sl3) + list(c3)
        if use_sel:
            tiles_per_group = SUBLANE // blocks_per_tile
            base = pl.multiple_of((jt // tiles_per_group) * SUBLANE, SUBLANE)
            which = jt % tiles_per_group
            per_kvh = []
            for kvh in range(KVH):
                rows8 = sel_ref[kvh, pl.ds(base, SUBLANE), :]
                mine = rows8[0:blocks_per_tile, :]
                for w in range(1, tiles_per_group):
                    mine = jnp.where(which == w, rows8[w * blocks_per_tile:(w + 1) * blocks_per_tile, :], mine)
                per_kvh.append(mine)
            rows.append(jnp.concatenate([per_kvh[kvh] for kvh in range(KVH) for _ in range(GROUP)], axis=1))
        rows.append(jnp.zeros((N_AUG - sum(r.shape[0] for r in rows), nl), F32))
        qa_ref[KV_W:KV_W + N_AUG, :] = jnp.concatenate(rows, axis=0).astype(BF16)
        ka = jnp.concatenate([k_ref[jt], augc_ref[...]], axis=1)
        s = jnp.dot(ka, qa_ref[...], preferred_element_type=F32)
        dst_ref[0:PB, :] = s
        dst_ref[PB:PB + 1, :] = jnp.max(s, axis=0, keepdims=True)

    def softmax_pv(vt_ref, jt, src_ref, extra):
        s = src_ref[0:PB, :]
        if extra is not None:
            s = s + jnp.concatenate([extra] * NSA_HEADS, axis=1)
            s_max = jnp.max(s, axis=0, keepdims=True)
        else:
            s_max = src_ref[PB:PB + 1, :]
        m_old = m_ref[...]
        m_new = jnp.maximum(m_old, s_max)
        alpha = jnp.exp2(m_old - m_new)
        p = jnp.exp2(s - m_new)
        m_ref[...] = m_new
        pb = p.astype(BF16)
        vt = vt_ref[jt]
        for kvh in range(KVH):
            pv = jnp.dot(vt[kvh * VROWS:(kvh + 1) * VROWS, :], pb[:, kvh * half:(kvh + 1) * half],
                         preferred_element_type=F32)
            acc_ref[kvh] = alpha[:, kvh * half:(kvh + 1) * half] * acc_ref[kvh] + pv

    def reset():
        m_ref[...] = jnp.full((1, nl), M_INIT, F32)
        acc_ref[...] = jnp.zeros((KVH, VROWS, half), F32)

    def result():
        return [acc_ref[kvh, 0:HD, :] * (1.0 / acc_ref[kvh, HD:HD + 1, :]) for kvh in range(KVH)]

    reset()
    n_pairs = (i + 2) // 2
    last = pl.num_programs(1) - 1
    odd_i = (i % 2) == 1
    scores(ks_ref, 0, True, sa_ref)

    def pair(u, carry):
        scores(ks_ref, 2 * u + 1, True, sb_ref)
        softmax_pv(vst_ref, 2 * u, sa_ref, None)
        scores(ks_ref, 2 * u + 2, True, sa_ref)
        softmax_pv(vst_ref, 2 * u + 1, sb_ref, None)
        return carry

    lax.fori_loop(0, n_pairs - 1, pair, 0)
    j0 = 2 * n_pairs - 2
    j1 = jnp.minimum(j0 + 1, last)
    scores(ks_ref, j1, True, sb_ref)
    softmax_pv(vst_ref, j0, sa_ref, jnp.where(odd_i, 0.0, causal_bias))
    softmax_pv(vst_ref, j1, sb_ref, jnp.where(odd_i, causal_bias, NEG))
    osel = result()

    reset()
    w0 = jnp.maximum(i - 2, 0)
    w1 = jnp.maximum(i - 1, 0)
    scores(kw_ref, w0, False, sa_ref)
    scores(kw_ref, w1, False, sb_ref)
    softmax_pv(vwt_ref, w0, sa_ref, jnp.where(i >= 2, edge_bias, NEG))
    scores(kw_ref, i, False, sa_ref)
    softmax_pv(vwt_ref, w1, sb_ref, jnp.where(i >= 1, 0.0, jnp.full((PB, PB), NEG, F32)))
    softmax_pv(vwt_ref, i, sa_ref, causal_bias)
    owin = result()

    gt_ = _sigmoid(gn_ref[...]).T
    for kvh in range(KVH):
        for p in range(GROUP // 2):
            pieces = []
            for hh in range(2):
                g = 2 * p + hh
                h = kvh * GROUP + g
                cs = slice(g * PB, (g + 1) * PB)
                pieces.append(gt_[3 * h:3 * h + 1, :] * ocmp[kvh][:, cs]
                              + gt_[3 * h + 1:3 * h + 2, :] * osel[kvh][:, cs]
                              + gt_[3 * h + 2:3 * h + 3, :] * owin[kvh][:, cs])
            c0 = (kvh * (GROUP // 2) + p) * LANE
            o_ref[:, c0:c0 + LANE] = jnp.concatenate(pieces, axis=0).T.astype(BF16)


def _aug_key_columns():
    a = np.zeros((PB, KV_W), np.float32)
    s = np.arange(PB)
    a[:, 0:3] = s[:, None]
    a[:, 3:6] = 1.0
    for r in range(PB // SEL_BLOCK):
        a[:, 6 + r] = (s // SEL_BLOCK == r)
    return a


def _nsa_prompt256(z, kc, vct, b, t):
    nq = t // PB
    n_seg = t // CMP_STRIDE
    n_sel = t // SEL_BLOCK
    nl = NSA_HEADS * PB
    assert 6 + PB // SEL_BLOCK <= N_AUG and n_sel % SUBLANE == 0 and WINDOW == 2 * PB
    slopes = _alibi_slopes_lanes(PB)
    covert = _cover_matrix(n_seg, n_sel).T.copy()
    kv_col = lambda c0: pl.BlockSpec((t, KV_W), lambda bi, i: (bi, c0 // KV_W))
    return pl.pallas_call(
        _nsa_prompt256_kernel,
        grid=(b, nq),
        in_specs=[pl.BlockSpec((PB, NSA_W), lambda bi, i: (bi * nq + i, C_Q // NSA_W)),
                  pl.BlockSpec((PB, LANE), lambda bi, i: (bi * nq + i, C_GN // LANE)),
                  pl.BlockSpec((None, n_seg, KV_W), lambda bi, i: (bi, 0, 0)),
                  pl.BlockSpec((None, KV_W, n_seg), lambda bi, i: (bi, 0, 0)),
                  kv_col(C_KVS), kv_col(C_KVS + KV_W), kv_col(C_KVW), kv_col(C_KVW + KV_W),
                  _const_spec((PB, KV_W)), _const_spec((1, nl)), _const_spec((n_sel, n_seg))],
        out_specs=pl.BlockSpec((PB, NSA_W), lambda bi, i: (bi * nq + i, 0)),
        out_shape=jax.ShapeDtypeStruct((b * t, NSA_W), BF16),
        scratch_shapes=[pltpu.VMEM((2 * KV_W, nl), BF16), pltpu.VMEM((1, nl), F32),
                        pltpu.VMEM((KVH, VROWS, GROUP * PB), F32), pltpu.VMEM((KVH, n_sel, PB), F32),
                        pltpu.VMEM((n_sel, PB), F32),
                        pltpu.VMEM((nq, PB, KV_W), BF16), pltpu.VMEM((nq, KVH * VROWS, PB), BF16),
                        pltpu.VMEM((nq, PB, KV_W), BF16), pltpu.VMEM((nq, KVH * VROWS, PB), BF16),
                        pltpu.VMEM((PB + SUBLANE, nl), F32), pltpu.VMEM((PB + SUBLANE, nl), F32)],
        compiler_params=_cparams(("parallel", "arbitrary")),
        name="nsa_prompt",
    )(z, z, kc, vct, z, z, z, z, jnp.asarray(_aug_key_columns(), BF16), jnp.asarray(slopes),
      jnp.asarray(covert))


def _alibi_slopes_lanes(width):
    h = np.arange(1, NSA_HEADS + 1, dtype=np.float32)
    return np.repeat(np.exp2(-8.0 * h / NSA_HEADS), width)[None, :].astype(np.float32)


def _cover_matrix(n_cmp_rows, n_sel_rows):
    n = np.arange(n_cmp_rows)[:, None] * CMP_STRIDE
    j = np.arange(n_sel_rows)[None, :] * SEL_BLOCK
    return ((n < j + SEL_BLOCK) & (n + CMP_BLOCK > j)).astype(np.float32)


def _hgrn_gates(pre, lb):
    e = jnp.exp(-pre)
    r = 1.0 / (1.0 + e)
    log_f = jnp.log(lb + (1.0 - lb) * r)
    k = (1.0 - lb) * jnp.where(pre > 0.0, e * r, 1.0 - r)
    return log_f, k


def _hgrn_out(o, gate, ng):
    outs = []
    for h in range(HG_H):
        sl = slice(h * HG_D, (h + 1) * HG_D)
        g = gate[:, sl]
        outs.append(_rms(o[:, sl], ng) * (g * _sigmoid(g)))
    return jnp.concatenate(outs, axis=1)


def _hgrn_prompt_kernel(hq_ref, hf_ref, hi_ref, hg_ref, lb_ref, ng_ref, tri_ref, o_ref, st_ref,
                        s_ref, oraw_ref):
    ci = pl.program_id(1)
    tc = hq_ref.shape[0]
    c = HG_CHUNK

    @pl.when(ci == 0)
    def _():
        s_ref[...] = jnp.zeros(s_ref.shape, F32)

    lb = lb_ref[...]
    tril = (lax.broadcasted_iota(jnp.int32, (c, c), 0) >= lax.broadcasted_iota(jnp.int32, (c, c), 1))
    row8 = lax.broadcasted_iota(jnp.int32, (SUBLANE, HG_D), 0)

    n_chunks = tc // c
    q_all = hq_ref[...]
    v_all = hi_ref[...]
    log_f, k_all = _hgrn_gates(hf_ref[...], lb)
    chunks = []
    worst = None
    for g in range(n_chunks):
        rows = slice(g * c, (g + 1) * c)
        bcum = jnp.dot(tri_ref[...], log_f[rows], precision=HI, preferred_element_type=F32)
        e = bcum - bcum[c // 2 - 1:c // 2, :]
        chunks.append((rows, bcum, e))
        worst = jnp.abs(e) if worst is None else jnp.maximum(worst, jnp.abs(e))
    safe = jnp.max(worst) < HG_SAFE_EXP

    @pl.when(safe)
    def _():
        for rows, bcum, e in chunks:
            q, k, v = q_all[rows], k_all[rows], v_all[rows]
            b_last = bcum[c - 1:c, :]
            qt = (q * jnp.exp(e)).astype(BF16)
            kt = (k * jnp.exp(-e)).astype(BF16)
            qb = (q * jnp.exp(bcum)).astype(BF16)
            kh = (k * jnp.exp(b_last - bcum)).astype(BF16)
            dec = jnp.exp(b_last)
            vb = v.astype(BF16)
            for h in range(HG_H):
                sl = slice(h * HG_D, (h + 1) * HG_D)
                a = lax.dot_general(qt[:, sl], kt[:, sl], (((1,), (1,)), ((), ())), preferred_element_type=F32)
                a = jnp.where(tril, a, 0.0).astype(BF16)
                st = s_ref[h]
                o = (lax.dot_general(qb[:, sl], st.astype(BF16), (((1,), (1,)), ((), ())),
                                     preferred_element_type=F32)
                     + jnp.dot(a, vb[:, sl], preferred_element_type=F32))
                oraw_ref[rows, sl] = o
                s_ref[h] = st * dec[:, sl] + lax.dot_general(vb[:, sl], kh[:, sl], (((0,), (0,)), ((), ())),
                                                             preferred_element_type=F32)

    @pl.when(jnp.logical_not(safe))
    def _():
        for h in range(HG_H):
            sl = slice(h * HG_D, (h + 1) * HG_D)

            def tile(ti, carry, sl=sl, h=h):
                r = pl.multiple_of(ti * SUBLANE, SUBLANE)
                q8 = hq_ref[pl.ds(r, SUBLANE), sl]
                v8 = hi_ref[pl.ds(r, SUBLANE), sl]
                lf8, k8 = _hgrn_gates(hf_ref[pl.ds(r, SUBLANE), sl], lb[:, sl])
                f8 = jnp.exp(lf8)
                st = s_ref[h]
                rows_out = []
                for u in range(SUBLANE):
                    vu = jnp.where(row8 == 0, jnp.broadcast_to(v8[u:u + 1, :], (SUBLANE, HG_D)), 0.0)
                    ku = jnp.broadcast_to(k8[u:u + 1, :], (SUBLANE, HG_D))
                    qu = jnp.broadcast_to(q8[u:u + 1, :], (SUBLANE, HG_D))
                    st = st * f8[u:u + 1, :] + lax.dot_general(
                        vu, ku, (((0,), (0,)), ((), ())), precision=HI, preferred_element_type=F32)
                    ou = lax.dot_general(qu, st, (((1,), (1,)), ((), ())), precision=HI,
                                         preferred_element_type=F32)
                    rows_out.append(ou[0:1, :])
                s_ref[h] = st
                oraw_ref[pl.ds(r, SUBLANE), sl] = jnp.concatenate(rows_out, axis=0)
                return carry

            lax.fori_loop(0, tc // SUBLANE, tile, 0)

    o_ref[...] = _hgrn_out(oraw_ref[...], hg_ref[...], ng_ref[...]).astype(BF16)

    @pl.when(ci == pl.num_programs(1) - 1)
    def _():
        for h in range(HG_H):
            st_ref[h] = s_ref[h].T


def _hgrn_prompt(z, lb, ng, b, t, tc):
    nc = t // tc
    col = lambda c0: pl.BlockSpec((tc, HG_W), lambda bi, ci: (bi * nc + ci, c0 // HG_W))
    tri = np.tril(np.ones((HG_CHUNK, HG_CHUNK), np.float32))
    return pl.pallas_call(
        _hgrn_prompt_kernel,
        grid=(b, nc),
        in_specs=[col(C_HQ), col(C_HF), col(C_HI), col(C_HG),
                  _const_spec((1, HG_W)), _const_spec((1, HG_D)), _const_spec((HG_CHUNK, HG_CHUNK))],
        out_specs=[pl.BlockSpec((tc, HG_W), lambda bi, ci: (bi * nc + ci, 0)),
                   pl.BlockSpec((None, HG_H, HG_D, HG_D), lambda bi, ci: (bi, 0, 0, 0))],
        out_shape=[jax.ShapeDtypeStruct((b * t, HG_W), BF16),
                   jax.ShapeDtypeStruct((b, HG_H, HG_D, HG_D), F32)],
        scratch_shapes=[pltpu.VMEM((HG_H, HG_D, HG_D), F32), pltpu.VMEM((tc, HG_W), F32)],
        compiler_params=_cparams(("parallel", "arbitrary")),
        name="hgrn_prompt",
    )(z, z, z, z, lb, ng, jnp.asarray(tri))


def _merge_kernel(on_ref, oh_ref, ga_ref, gb_ref, x_ref, wn_ref, wh_ref, wo_ref, g_ref, o_ref):
    y = (_sigmoid(ga_ref[...]) * jnp.dot(on_ref[...], wn_ref[...], preferred_element_type=F32)
         + _sigmoid(gb_ref[...]) * jnp.dot(oh_ref[...], wh_ref[...], preferred_element_type=F32))
    mix = jnp.dot(y.astype(BF16), wo_ref[...], preferred_element_type=F32)
    o_ref[...] = x_ref[...] + _rms(mix, g_ref[...])


def _resident(shape):
    nd = len(shape)
    return pl.BlockSpec(shape, lambda *_: (0,) * nd, pipeline_mode=pl.Buffered(1))


def _merge(o_nsa, o_hg, z, x, wn, wh, wo, g, tm):
    n = x.shape[0]
    row = lambda w, cb: pl.BlockSpec((tm, w), lambda i: (i, cb))
    return pl.pallas_call(
        _merge_kernel,
        grid=(n // tm,),
        in_specs=[row(NSA_W, 0), row(HG_W, 0), row(D_MODEL, C_GM // D_MODEL), row(D_MODEL, C_GM // D_MODEL + 1),
                  row(D_MODEL, 0), _resident(wn.shape), _resident(wh.shape), _resident(wo.shape),
                  _resident((1, D_MODEL))],
        out_specs=row(D_MODEL, 0),
        out_shape=jax.ShapeDtypeStruct((n, D_MODEL), F32),
        compiler_params=_cparams(("parallel",)),
        name="merge_out_proj",
    )(o_nsa, o_hg, z, z, x, wn, wh, wo, g)


def _mlp_kernel(h_ref, p_ref, wu_ref, wd_ref, wg_ref, wp_ref, g1_ref, g2_ref, g3_ref, o_ref):
    h = h_ref[...]
    xn = _rms(h, g1_ref[...]).astype(BF16)
    ffn = jnp.zeros(h.shape, F32)
    step = D_MODEL // 2
    for c0 in range(0, D_FF, step):
        up = jnp.dot(xn, wu_ref[:, c0:c0 + step], preferred_element_type=F32)
        act = jnp.square(jnp.maximum(up, 0.0)).astype(BF16)
        ffn = ffn + jnp.dot(act, wd_ref[c0:c0 + step, :], preferred_element_type=F32)
    h = h + _rms(ffn, g2_ref[...])
    gate = _sigmoid(jnp.dot(_rms(h, g3_ref[...]).astype(BF16), wg_ref[...], preferred_element_type=F32))
    o_ref[...] = h + gate * jnp.dot(p_ref[...].astype(BF16), wp_ref[...], preferred_element_type=F32)


def _mlp(h, p, wu, wd, wg, wp, g1, g2, g3, tm):
    n = h.shape[0]
    row = lambda w: pl.BlockSpec((tm, w), lambda i: (i, 0))
    gain = _resident((1, D_MODEL))
    return pl.pallas_call(
        _mlp_kernel,
        grid=(n // tm,),
        in_specs=[row(D_MODEL), row(PLE_DIM), _resident(wu.shape), _resident(wd.shape), _resident(wg.shape),
                  _resident(wp.shape), gain, gain, gain],
        out_specs=row(D_MODEL),
        out_shape=jax.ShapeDtypeStruct((n, D_MODEL), F32),
        compiler_params=_cparams(("parallel",)),
        name="mlp_ple",
    )(h, p, wu, wd, wg, wp, g1, g2, g3)


def _prep_w_in(w):
    sizes = (NSA_W, 2 * KV_W, 2 * KV_W, 2 * KV_W, 3 * NSA_HEADS, HG_W, HG_W, HG_W, HG_W)
    q, kvc, kvs, kvw, gn, hq, hf, hi, hg, gm = jnp.split(w, [int(v) for v in np.cumsum(sizes)], axis=1)
    pad = jnp.zeros((w.shape[0], Z_COLS - C_GN - 3 * NSA_HEADS), w.dtype)
    return jnp.concatenate([gm, q, hq, hf, hi, hg, kvc, kvs, kvw, gn, pad], axis=1).astype(BF16)


def _prep_compress(w1k, w2k, w1v, w2v, pos):
    eye = jnp.eye(KVH, dtype=F32)

    def big1(w1):
        t = jnp.einsum('rsdh,kq->rskdqh', w1, eye)
        return t.reshape(CMP_BLOCK // CMP_STRIDE, CMP_STRIDE * KV_W, KVH * CMP_HIDDEN)

    def big2(w2):
        return jnp.einsum('hd,kq->khqd', w2, eye).reshape(KVH * CMP_HIDDEN, KV_W)

    w1 = jnp.stack([big1(w1k), big1(w1v)]).astype(BF16)
    w2 = jnp.stack([big2(w2k), big2(w2v)]).astype(BF16)
    posb = jnp.broadcast_to(pos[:, :, None, :], pos.shape[:2] + (KVH, HD)).reshape(pos.shape[0], 1, -1)
    posb = jnp.broadcast_to(posb, (pos.shape[0], SUBLANE, posb.shape[-1])).astype(BF16)
    return w1, w2, posb


def _kv_rows_t_kernel(ck_ref, cv_ref, sk_ref, sv_ref, wk_ref, wv_ref, ct_ref, st_ref, wt_ref):
    rows = ck_ref.shape[0]
    for src, dst, part in ((ck_ref, ct_ref, 0), (cv_ref, ct_ref, 1), (sk_ref, st_ref, 0), (sv_ref, st_ref, 1),
                           (wk_ref, wt_ref, 0), (wv_ref, wt_ref, 1)):
        for j in range(rows // LANE):
            dst[part * KV_W:(part + 1) * KV_W, j * LANE:(j + 1) * LANE] = src[j * LANE:(j + 1) * LANE, :].T


def _kv_rows_t(z, b, t, rows):
    nt = t // rows
    col = lambda c0: pl.BlockSpec((rows, KV_W), lambda bi, j: (bi * nt + j, c0 // KV_W))
    full = pl.BlockSpec((None, 2 * KV_W, rows), lambda bi, j: (bi, 0, j))
    tail = pl.BlockSpec((None, 2 * KV_W, rows), lambda bi, j: (bi, 0, 0))
    return pl.pallas_call(
        _kv_rows_t_kernel,
        grid=(b, nt),
        in_specs=[col(C_KVC), col(C_KVC + KV_W), col(C_KVS), col(C_KVS + KV_W), col(C_KVW), col(C_KVW + KV_W)],
        out_specs=[full, full, tail],
        out_shape=[jax.ShapeDtypeStruct((b, 2 * KV_W, t), F32), jax.ShapeDtypeStruct((b, 2 * KV_W, t), F32),
                   jax.ShapeDtypeStruct((b, 2 * KV_W, rows), F32)],
        compiler_params=_cparams(("parallel", "arbitrary")),
        name="kv_rows_t",
    )(z, z, z, z, z, z)


def _prep_layer(i, lb_all, w_in, cmp_k_w1, cmp_k_w2, cmp_v_w1, cmp_v_w2, cmp_pos, hg_norm, w_branch_nsa,
                w_branch_hgrn, w_out, norm_pre_mix, norm_post_mix, norm_pre_mlp, norm_post_mlp, w_mlp_up,
                w_mlp_down, norm_ple, w_ple_gate, w_ple_proj):
    w1, w2, posb = _prep_compress(cmp_k_w1[i], cmp_k_w2[i], cmp_v_w1[i], cmp_v_w2[i], cmp_pos[i])
    row = lambda a: a[i].reshape(1, -1).astype(F32)
    return {
        'w_in': _prep_w_in(w_in[i]), 'cmp_w1': w1, 'cmp_w2': w2, 'cmp_pos': posb,
        'hg_lb': lb_all[i].reshape(1, HG_W), 'hg_norm': row(hg_norm),
        'w_bn': w_branch_nsa[i].astype(BF16), 'w_bh': w_branch_hgrn[i].astype(BF16), 'w_out': w_out[i].astype(BF16),
        'w_up': w_mlp_up[i].astype(BF16), 'w_down': w_mlp_down[i].astype(BF16),
        'w_gate': w_ple_gate[i].astype(BF16), 'w_proj': w_ple_proj[i].astype(BF16),
        'g_pre_mix': row(norm_pre_mix), 'g_post_mix': row(norm_post_mix), 'g_pre_mlp': row(norm_pre_mlp),
        'g_post_mlp': row(norm_post_mlp), 'g_ple': row(norm_ple),
    }


def _layer_prompt(x, p, lw, b, t):
    n = b * t
    z = _norm_matmul(x, lw['g_pre_mix'], lw['w_in'], tm=min(1024, n), tn=Z_COLS // 4)
    wb = min(WINDOW, t)
    kv_c, kv_s, kv_w = _kv_rows_t(z, b, t, wb)
    kc, vc = _compress_prompt(z, lw['cmp_w1'], lw['cmp_w2'], lw['cmp_pos'], b, t)
    vct = jnp.swapaxes(vc, 1, 2)
    o_nsa = _nsa_prompt256(z, kc, vct, b, t)
    o_hg, st = _hgrn_prompt(z, lw['hg_lb'], lw['hg_norm'], b, t, tc=min(256, t))
    h1 = _merge(o_nsa, o_hg, z, x, lw['w_bn'], lw['w_bh'], lw['w_out'], lw['g_post_mix'], tm=min(512, n))
    h2 = _mlp(h1, p, lw['w_up'], lw['w_down'], lw['w_gate'], lw['w_proj'],
              lw['g_pre_mlp'], lw['g_post_mlp'], lw['g_ple'], tm=min(1024, n))
    kv6 = lambda a: a.reshape(b, 2, KVH, HD, a.shape[-1]).transpose(0, 4, 1, 2, 3)
    return h2, kv6(kv_c), kv6(kv_s), kv6(kv_w), st


SEG_PITCH = 24


def _page_fetch(pt_ref, cache_ref, buf_ref, sem):
    b = pl.program_id(0)
    n_pages = pt_ref.shape[1]
    slot = b % 2

    def copy(seq, sl, p):
        return pltpu.make_async_copy(cache_ref.at[pt_ref[seq, p]], buf_ref.at[sl, p], sem.at[sl])

    def start(seq, sl):
        lax.fori_loop(0, n_pages, lambda p, c: (copy(seq, sl, p).start(), c)[1], 0)

    @pl.when(b == 0)
    def _():
        start(0, 0)

    @pl.when(b + 1 < pl.num_programs(0))
    def _():
        start(b + 1, 1 - slot)

    def wait():
        lax.fori_loop(0, n_pages, lambda p, c: (copy(b, slot, p).wait(), c)[1], 0)

    return slot, wait


def _compress_sample_kernel(pt_ref, cache_ref, w1_ref, w2_ref, pos_ref, kc_ref, vc_ref, buf_ref, rows_ref, sem):
    n_pages, page = pt_ref.shape[1], buf_ref.shape[-1]
    n_seg = n_pages * page // CMP_STRIDE
    slot, wait = _page_fetch(pt_ref, cache_ref, buf_ref, sem)
    wait()
    segs_per_page = page // CMP_STRIDE
    for kv, out_ref in ((0, kc_ref), (1, vc_ref)):
        for p in range(n_pages):
            rows = buf_ref[slot, p, kv].T
            for g in range(segs_per_page):
                r0 = (p * segs_per_page + g) * SEG_PITCH
                rows_ref[kv, r0:r0 + CMP_STRIDE, :] = rows[g * CMP_STRIDE:(g + 1) * CMP_STRIDE, :]
        load = lambda s, kv=kv: rows_ref[kv, pl.ds(s, n_seg, stride=SEG_PITCH), :]
        out_ref[...] = _compress_math(load, n_seg, w1_ref, w2_ref, pos_ref, kv).astype(BF16)


def _compress_sample(page_table, cache_t, w1, w2, pos):
    bs, n_pages = page_table.shape
    page = cache_t.shape[-1]
    n_seg = n_pages * page // CMP_STRIDE
    const = lambda shape: pl.BlockSpec(shape, lambda i, pt: (0,) * len(shape), pipeline_mode=pl.Buffered(1))
    out = pl.BlockSpec((None, n_seg, KV_W), lambda i, pt: (i, 0, 0))
    return pl.pallas_call(
        _compress_sample_kernel,
        grid_spec=pltpu.PrefetchScalarGridSpec(
            num_scalar_prefetch=1, grid=(bs,),
            in_specs=[pl.BlockSpec(memory_space=pl.ANY), const(w1.shape), const(w2.shape), const(pos.shape)],
            out_specs=[out, out],
            scratch_shapes=[pltpu.VMEM((2, n_pages, 2, KV_W, page), F32),
                            pltpu.VMEM((2, n_seg * SEG_PITCH, KV_W), F32), pltpu.SemaphoreType.DMA((2,))]),
        out_shape=[jax.ShapeDtypeStruct((bs, n_seg, KV_W), BF16)] * 2,
        compiler_params=_cparams(("arbitrary",)),
        name="compress_sample",
    )(page_table, cache_t, w1, w2, pos)


_NT = (((1,), (1,)), ((), ()))


def _nsa_sample_kernel(pt_ref, qbd_ref, gl_ref, kc_ref, vc_ref, snew_ref, cwin_ref, wnew_ref, csel_ref,
                       slope_ref, tq_ref, tq8_ref, cover_ref, gsum_ref, gexp_ref, o_ref,
                       buf_ref, s_ref, sem):
    n_pages, page = pt_ref.shape[1], buf_ref.shape[-1]
    past = n_pages * page
    n_seg = kc_ref.shape[0]
    npad = cover_ref.shape[1]
    n_sel = past // SEL_BLOCK + 1
    nr = qbd_ref.shape[0]
    wb = cwin_ref.shape[-1]
    slot, wait = _page_fetch(pt_ref, csel_ref, buf_ref, sem)

    qb = (qbd_ref[...] * SCALE).astype(BF16)
    slope = slope_ref[...]
    qpos = tq_ref[...] + float(past)
    zeros_pad = jnp.zeros((QB - snew_ref.shape[0], KV_W), F32)
    t_new = lax.broadcasted_iota(jnp.int32, (1, QB), 1).astype(F32)

    sc = lax.dot_general(qb, kc_ref[...], _NT, preferred_element_type=F32)
    n_i = lax.broadcasted_iota(jnp.int32, (1, n_seg), 1)
    dcmp = qpos - (n_i * CMP_STRIDE + (CMP_BLOCK - 1)).astype(F32)
    okc = dcmp >= 0.0
    sc = jnp.where(okc, sc - slope * dcmp, NEG)
    mc = jnp.max(sc, axis=-1, keepdims=True)
    pc = jnp.where(okc, jnp.exp(sc - mc), 0.0)
    lc = jnp.sum(pc, axis=-1, keepdims=True)
    pc = pc * (1.0 / jnp.where(lc > 0.0, lc, 1.0))
    o_cmp = jnp.dot(pc.astype(BF16), vc_ref[...], preferred_element_type=F32)

    psum = jnp.dot(gsum_ref[...], pc, precision=HI, preferred_element_type=F32)
    imp = jnp.dot(psum, cover_ref[...], precision=HI, preferred_element_type=F32)
    nq8 = gsum_ref.shape[0]
    jblk = lax.broadcasted_iota(jnp.int32, (nq8, npad), 1)
    qblk = (tq8_ref[...].astype(jnp.int32) + past) // SEL_BLOCK
    back = qblk - jblk
    visible = back >= 0
    forced = (jblk == 0) | (visible & (back < SEL_LOCAL))
    score = jnp.where(forced, 1e9, jnp.where(visible, imp, -1e9))
    rank = jnp.zeros((nq8, npad), F32)
    for jp in range(n_sel):
        col = score[:, jp:jp + 1]
        ge = jnp.where(col >= score, 1.0, 0.0)
        gt = jnp.where(col > score, 1.0, 0.0)
        rank = rank + jnp.where(jblk > jp, ge, gt)
    sel8 = jnp.where(visible, jnp.where(rank < float(min(SEL_TOP, n_sel)), 1.0, 0.0), 0.0)
    negb = (jnp.dot(gexp_ref[...], sel8, precision=HI, preferred_element_type=F32) - 1.0) * (-NEG)

    def tile_bias(jt):
        lane = lax.broadcasted_iota(jnp.int32, (nr, QB), 1)
        return jnp.where(lane < SEL_BLOCK, negb[:, 2 * jt:2 * jt + 1], negb[:, 2 * jt + 1:2 * jt + 2])

    wait()
    ppc = 4
    ck = ppc * page
    tiles_per_page = page // QB

    def chunk_t(c, kv):
        return jnp.concatenate([buf_ref[slot, c * ppc + u, kv] for u in range(ppc)], axis=1).astype(BF16)

    for c in range(n_pages // ppc):
        s = jnp.dot(qb, chunk_t(c, 0), preferred_element_type=F32)
        kpos = (lax.broadcasted_iota(jnp.int32, (1, ck), 1) + c * ck).astype(F32)
        bias = jnp.concatenate([tile_bias(c * ppc * tiles_per_page + u) for u in range(ppc * tiles_per_page)],
                               axis=1)
        s_ref[:, c * ck:(c + 1) * ck] = s - slope * (qpos - kpos) + bias
    knew = jnp.concatenate([snew_ref[:, 0:KV_W], zeros_pad], axis=0).astype(BF16)
    vnew = jnp.concatenate([snew_ref[:, KV_W:2 * KV_W], zeros_pad], axis=0).astype(BF16)
    dnew = tq_ref[...] - t_new
    s = lax.dot_general(qb, knew, _NT, preferred_element_type=F32)
    s_ref[:, past:past + QB] = jnp.where(dnew >= 0.0, s - slope * dnew + tile_bias(past // QB), NEG)
    s_all = s_ref[...]
    ms = jnp.max(s_all, axis=-1, keepdims=True)
    ps = jnp.exp(s_all - ms)
    ls = jnp.sum(ps, axis=-1, keepdims=True)
    psb = ps.astype(BF16)
    o_sel = jnp.dot(psb[:, past:past + QB], vnew, preferred_element_type=F32)
    for c in range(n_pages // ppc):
        o_sel = o_sel + lax.dot_general(psb[:, c * ck:(c + 1) * ck], chunk_t(c, 1), _NT,
                                        preferred_element_type=F32)
    o_sel = o_sel * (1.0 / ls)

    s1 = jnp.dot(qb, cwin_ref[0].astype(BF16), preferred_element_type=F32)
    d1 = float(wb) + tq_ref[...] - lax.broadcasted_iota(jnp.int32, (1, wb), 1).astype(F32)
    ok1 = d1 < float(WINDOW)
    s1 = jnp.where(ok1, s1 - slope * d1, NEG)
    wk = jnp.concatenate([wnew_ref[:, 0:KV_W], zeros_pad], axis=0).astype(BF16)
    wv = jnp.concatenate([wnew_ref[:, KV_W:2 * KV_W], zeros_pad], axis=0).astype(BF16)
    ok2 = dnew >= 0.0
    s2 = jnp.where(ok2, lax.dot_general(qb, wk, _NT, preferred_element_type=F32) - slope * dnew, NEG)
    mw = jnp.maximum(jnp.max(s1, axis=-1, keepdims=True), jnp.max(s2, axis=-1, keepdims=True))
    p1 = jnp.where(ok1, jnp.exp(s1 - mw), 0.0)
    p2 = jnp.where(ok2, jnp.exp(s2 - mw), 0.0)
    lw_ = jnp.sum(p1, axis=-1, keepdims=True) + jnp.sum(p2, axis=-1, keepdims=True)
    o_win = (lax.dot_general(p1.astype(BF16), cwin_ref[1].astype(BF16), _NT, preferred_element_type=F32)
             + jnp.dot(p2.astype(BF16), wv, preferred_element_type=F32)) * (1.0 / lw_)

    sig = _sigmoid(gl_ref[...])
    o_ref[...] = sig[:, 0:1] * o_cmp + sig[:, 1:2] * o_sel + sig[:, 2:3] * o_win


def _nsa_sample(page_table, qbd, gl, kc, vc, snew, cwin, wnew, csel, ts):
    bs, n_pages = page_table.shape
    page = csel.shape[-1]
    past = n_pages * page
    n_seg = kc.shape[1]
    n_sel = past // SEL_BLOCK + 1
    npad = -(-(n_sel + 1) // LANE) * LANE
    nr = KVH * GROUP * ts
    wb = cwin.shape[-1]
    r = np.arange(nr)
    slope = np.exp2(-8.0 * ((r // ts) + 1) / NSA_HEADS).astype(np.float32)[:, None]
    tq = (r % ts).astype(np.float32)[:, None]
    r8 = np.arange(KVH * ts)
    tq8 = (r8 % ts).astype(np.float32)[:, None]
    cover = np.zeros((n_seg, npad), np.float32)
    cover[:, :n_sel] = _cover_matrix(n_seg, n_sel)
    gsum = ((r[None, :] // (GROUP * ts) == r8[:, None] // ts) & (r[None, :] % ts == r8[:, None] % ts)).astype(np.float32)
    consts = [slope, tq, tq8, cover, gsum, gsum.T.copy()]
    const = lambda shape: pl.BlockSpec(shape, lambda i, pt: (0,) * len(shape), pipeline_mode=pl.Buffered(1))
    per = lambda *s: pl.BlockSpec((None,) + s, lambda i, pt: (i,) + (0,) * len(s))
    return pl.pallas_call(
        _nsa_sample_kernel,
        grid_spec=pltpu.PrefetchScalarGridSpec(
            num_scalar_prefetch=1, grid=(bs,),
            in_specs=[per(nr, KV_W), per(nr, LANE), per(n_seg, KV_W), per(n_seg, KV_W), per(SUBLANE, 2 * KV_W),
                      per(2, KV_W, wb), per(SUBLANE, 2 * KV_W), pl.BlockSpec(memory_space=pl.ANY)]
                     + [const(c.shape) for c in consts],
            out_specs=per(nr, KV_W),
            scratch_shapes=[pltpu.VMEM((2, n_pages, 2, KV_W, page), F32), pltpu.VMEM((nr, past + QB), F32),
                            pltpu.SemaphoreType.DMA((2,))]),
        out_shape=jax.ShapeDtypeStruct((bs, nr, KV_W), F32),
        compiler_params=_cparams(("arbitrary",)),
        name="nsa_sample",
    )(page_table, qbd, gl, kc, vc, snew, cwin, wnew, csel, *[jnp.asarray(c) for c in consts])


def _hgrn_sample_kernel(ts, hq_ref, hf_ref, hi_ref, hg_ref, lb_ref, ng_ref, s0_ref, o_ref, s1_ref):
    rows = hq_ref.shape[0]
    q = hq_ref[...]
    v = hi_ref[...]
    log_f, k = _hgrn_gates(hf_ref[...], lb_ref[...])
    tloc = lax.broadcasted_iota(jnp.int32, (rows, HG_W), 0) % ts
    up = lambda a, d: pltpu.roll(a, d, axis=0)
    down = lambda a, d: pltpu.roll(a, rows - d, axis=0)

    bcum = log_f
    for d in range(1, ts):
        bcum = bcum + jnp.where(tloc >= d, up(log_f, d), 0.0)
    b_last = bcum
    for d in range(1, ts):
        b_last = jnp.where(tloc == ts - 1 - d, down(bcum, d), b_last)

    o_intra = [jnp.zeros((rows, HG_D), F32) for _ in range(HG_H)]
    for d in range(ts):
        kd, bd, vd = (k, bcum, v) if d == 0 else (up(k, d), up(bcum, d), up(v, d))
        w = jnp.where(tloc >= d, q * kd * jnp.exp(jnp.where(tloc >= d, bcum - bd, 0.0)), 0.0)
        for h in range(HG_H):
            sl = slice(h * HG_D, (h + 1) * HG_D)
            o_intra[h] = o_intra[h] + jnp.sum(w[:, sl], axis=-1, keepdims=True) * vd[:, sl]

    qb = q * jnp.exp(bcum)
    kh = k * jnp.exp(b_last - bcum)
    per_tile = SUBLANE // ts
    row8 = lax.broadcasted_iota(jnp.int32, (SUBLANE, HG_D), 0) // ts
    tiles = []
    for j in range(rows // SUBLANE):
        r8 = slice(j * SUBLANE, (j + 1) * SUBLANE)
        heads = []
        for h in range(HG_H):
            sl = slice(h * HG_D, (h + 1) * HG_D)
            o_inter = jnp.zeros((SUBLANE, HG_D), F32)
            for u in range(per_tile):
                seq = j * per_tile + u
                mine = row8 == u
                s0 = s0_ref[seq, h]
                o_inter = o_inter + jnp.dot(jnp.where(mine, qb[r8, sl], 0.0).astype(BF16), s0.astype(BF16),
                                            preferred_element_type=F32)
                upd = lax.dot_general(jnp.where(mine, kh[r8, sl], 0.0).astype(BF16), v[r8, sl].astype(BF16),
                                      (((0,), (0,)), ((), ())), preferred_element_type=F32)
                r_last = j * SUBLANE + u * ts + ts - 1
                dec = jnp.exp(bcum[r_last:r_last + 1, sl])
                s1_ref[seq, h] = jnp.broadcast_to(dec, (HG_D, HG_D)).T * s0 + upd
            heads.append(o_inter + o_intra[h][r8, :])
        tiles.append(jnp.concatenate(heads, axis=1))
    o = jnp.concatenate(tiles, axis=0)
    o_ref[...] = _hgrn_out(o, hg_ref[...], ng_ref[...]).astype(BF16)


def _hgrn_sample(z, lb, ng, s0, bs, ts, nb):
    rows = nb * ts
    col = lambda c0: pl.BlockSpec((rows, HG_W), lambda i: (i, c0 // HG_W))
    st = pl.BlockSpec((nb, HG_H, HG_D, HG_D), lambda i: (i, 0, 0, 0))
    return pl.pallas_call(
        functools.partial(_hgrn_sample_kernel, ts),
        grid=(bs // nb,),
        in_specs=[col(C_HQ), col(C_HF), col(C_HI), col(C_HG), _const_spec((1, HG_W)), _const_spec((1, HG_D)), st],
        out_specs=[pl.BlockSpec((rows, HG_W), lambda i: (i, 0)), st],
        out_shape=[jax.ShapeDtypeStruct((bs * ts, HG_W), BF16),
                   jax.ShapeDtypeStruct((bs, HG_H, HG_D, HG_D), F32)],
        compiler_params=_cparams(("parallel",)),
        name="hgrn_sample",
    )(z, z, z, z, lb, ng, s0)


def _layer_sample(x, p, cache_cmp, cache_sel, cache_win, state, page_table, lw, bs, ts):
    n = bs * ts
    assert SUBLANE % ts == 0 and n % SUBLANE == 0
    z = _norm_matmul(x, lw['g_pre_mix'], lw['w_in'], tm=min(512, n), tn=512)
    kv_c = z[:, C_KVC:C_KVC + 2 * KV_W]
    kv_s = z[:, C_KVS:C_KVS + 2 * KV_W]
    kv_w = z[:, C_KVW:C_KVW + 2 * KV_W]

    rows_last = lambda a: jnp.transpose(a, (0, 2, 3, 4, 1)).reshape(a.shape[0], 2, KV_W, a.shape[1])
    kc, vc = _compress_sample(page_table, rows_last(cache_cmp), lw['cmp_w1'], lw['cmp_w2'], lw['cmp_pos'])

    eye = jnp.eye(KVH, dtype=F32)
    q5 = z[:, C_Q:C_Q + NSA_W].reshape(bs, ts, KVH, GROUP, HD).transpose(0, 2, 3, 1, 4)
    qbd = jnp.einsum('bkgtd,kq->bkgtqd', q5, eye).reshape(bs, KVH * GROUP * ts, KV_W)
    g5 = z[:, C_GN:C_GN + 3 * NSA_HEADS].reshape(bs, ts, KVH, GROUP, 3).transpose(0, 2, 3, 1, 4)
    gl = jnp.pad(g5.reshape(bs, KVH * GROUP * ts, 3), ((0, 0), (0, 0), (0, LANE - 3)))
    pad_rows = lambda a: jnp.pad(a.reshape(bs, ts, 2 * KV_W), ((0, 0), (0, SUBLANE - ts), (0, 0)))
    o_rows = _nsa_sample(page_table, qbd, gl, kc, vc, pad_rows(kv_s), rows_last(cache_win), pad_rows(kv_w),
                         rows_last(cache_sel), ts)
    o6 = o_rows.reshape(bs, KVH, GROUP, ts, KVH, HD)
    o_nsa = jnp.stack([o6[:, kvh, :, :, kvh, :] for kvh in range(KVH)], axis=1)
    o_nsa = o_nsa.transpose(0, 3, 1, 2, 4).reshape(n, NSA_W).astype(BF16)

    o_hg, st = _hgrn_sample(z, lw['hg_lb'], lw['hg_norm'], state, bs, ts, nb=min(8, bs))
    h1 = _merge(o_nsa, o_hg, z, x, lw['w_bn'], lw['w_bh'], lw['w_out'], lw['g_post_mix'], tm=min(512, n))
    h2 = _mlp(h1, p, lw['w_up'], lw['w_down'], lw['w_gate'], lw['w_proj'],
              lw['g_pre_mlp'], lw['g_post_mlp'], lw['g_ple'], tm=min(512, n))
    kv6 = lambda a: a.reshape(bs, ts, 2, KVH, HD)
    win_buf = jnp.concatenate([cache_win, kv6(kv_w)], axis=1)[:, ts:]
    return h2, kv6(kv_c), kv6(kv_s), win_buf, st


def kernel(x_prompt, x_sample, cache_cmp_kv, cache_sel_kv, cache_win_kv, state_hgrn, page_table, p_prompt,
           p_sample, w_in, cmp_k_w1, cmp_k_w2, cmp_v_w1, cmp_v_w2, cmp_pos, hg_lb_logits, hg_norm, w_branch_nsa,
           w_branch_hgrn, w_out, norm_pre_mix, norm_post_mix, norm_pre_mlp, norm_post_mlp, w_mlp_up, w_mlp_down,
           norm_ple, w_ple_gate, w_ple_proj):
    depth = w_in.shape[0]
    b, t, d = x_prompt.shape
    bs, ts, _ = x_sample.shape
    lb_all = jnp.cumsum(jax.nn.softmax(hg_lb_logits.astype(F32), axis=0), axis=0)
    h_p = x_prompt.reshape(b * t, d)
    h_s = x_sample.reshape(bs * ts, d)
    outs = [[] for _ in range(8)]
    for i in range(depth):
        lw = _prep_layer(i, lb_all, w_in, cmp_k_w1, cmp_k_w2, cmp_v_w1, cmp_v_w2, cmp_pos, hg_norm, w_branch_nsa,
                         w_branch_hgrn, w_out, norm_pre_mix, norm_post_mix, norm_pre_mlp, norm_post_mlp, w_mlp_up,
                         w_mlp_down, norm_ple, w_ple_gate, w_ple_proj)
        h_p, *res_p = _layer_prompt(h_p, p_prompt[i].reshape(b * t, -1), lw, b, t)
        h_s, *res_s = _layer_sample(h_s, p_sample[i].reshape(bs * ts, -1), cache_cmp_kv[i], cache_sel_kv[i],
                                    cache_win_kv[i], state_hgrn[i], page_table, lw, bs, ts)
        for lst, v in zip(outs, res_p + res_s):
            lst.append(v)
    return (h_p.reshape(b, t, d), h_s.reshape(bs, ts, d)) + tuple(jnp.stack(lst, axis=0) for lst in outs)
```

```python
import functools
import math

import numpy as np
import jax
import jax.numpy as jnp
from jax import lax
from jax.experimental import pallas as pl
from jax.experimental.pallas import tpu as pltpu

F32 = jnp.float32
BF16 = jnp.bfloat16

D_MODEL = 1024
NSA_HEADS = 8
KVH = 2
GROUP = NSA_HEADS // KVH
HD = 64
NSA_W = NSA_HEADS * HD
KV_W = KVH * HD
CMP_BLOCK = 32
CMP_STRIDE = 16
CMP_HIDDEN = 2 * HD
SEL_BLOCK = 64
SEL_TOP = 16
SEL_LOCAL = 2
WINDOW = 512
HG_W = 512
HG_H = 4
HG_D = 128
HG_CHUNK = 64
D_FF = 4 * D_MODEL
PLE_DIM = 256
RMS_EPS = 1e-6
NEG = -1e30
M_INIT = -1e20
SCALE = HD ** -0.5
HG_SAFE_EXP = 60.0

LANE = 128
SUBLANE = 8
VMEM_LIMIT = 48 * 1024 * 1024

C_GM = 0
C_Q = 2048
C_HQ = 2560
C_HF = 3072
C_HI = 3584
C_HG = 4096
C_KVC = 4608
C_KVS = 4864
C_KVW = 5120
C_GN = 5376
Z_COLS = 5632
QB = 128
HI = lax.Precision.HIGHEST


def _cparams(sem, vmem=VMEM_LIMIT):
    return pltpu.CompilerParams(dimension_semantics=sem, vmem_limit_bytes=vmem)


def _rms(x, g):
    return x * lax.rsqrt(jnp.mean(x * x, axis=-1, keepdims=True) + RMS_EPS) * g


def _sigmoid(x):
    return 1.0 / (1.0 + jnp.exp(-x))


def _gelu_tanh(x):
    return 0.5 * x * (1.0 + jnp.tanh(math.sqrt(2.0 / math.pi) * (x + 0.044715 * (x * x * x))))


def _const_spec(shape):
    nd = len(shape)
    return pl.BlockSpec(shape, lambda *_: (0,) * nd)


def _norm_matmul_kernel(x_ref, g_ref, w_ref, o_ref, xn_ref):
    @pl.when(pl.program_id(1) == 0)
    def _():
        xn_ref[...] = _rms(x_ref[...], g_ref[...]).astype(BF16)

    tn = o_ref.shape[1]
    col = pl.multiple_of(pl.program_id(1) * tn, LANE)
    o_ref[...] = jnp.dot(xn_ref[...], w_ref[:, pl.ds(col, tn)], preferred_element_type=F32)


def _norm_matmul(x, g, w, tm, tn):
    n, d = x.shape
    c = w.shape[1]
    return pl.pallas_call(
        _norm_matmul_kernel,
        grid=(n // tm, c // tn),
        in_specs=[pl.BlockSpec((tm, d), lambda i, j: (i, 0)),
                  pl.BlockSpec((1, d), lambda i, j: (0, 0)),
                  pl.BlockSpec((d, c), lambda i, j: (0, 0), pipeline_mode=pl.Buffered(1))],
        out_specs=pl.BlockSpec((tm, tn), lambda i, j: (i, j)),
        out_shape=jax.ShapeDtypeStruct((n, c), F32),
        scratch_shapes=[pltpu.VMEM((tm, d), BF16)],
        compiler_params=_cparams(("parallel", "arbitrary")),
        name="norm_in_proj",
    )(x, g, w)


def _compress_math(load_rows, n_seg, w1_ref, w2_ref, pos_ref, kv):
    x = jnp.concatenate([load_rows(s) for s in range(CMP_STRIDE)], axis=1).astype(BF16)
    h0 = jnp.dot(x, w1_ref[kv, 0], preferred_element_type=F32)
    h1 = jnp.dot(x, w1_ref[kv, 1], preferred_element_type=F32)
    posb = (jnp.dot(pos_ref[0], w1_ref[kv, 0], preferred_element_type=F32)
            + jnp.dot(pos_ref[1], w1_ref[kv, 1], preferred_element_type=F32))
    hid = h0 + pltpu.roll(h1, n_seg - 1, axis=0) + posb[0:1]
    return jnp.dot(_gelu_tanh(hid).astype(BF16), w2_ref[kv], preferred_element_type=F32)


def _compress_prompt_kernel(xk_ref, xv_ref, w1_ref, w2_ref, pos_ref, kc_ref, vc_ref):
    n_seg = xk_ref.shape[0] // CMP_STRIDE
    for kv, x_ref, out_ref in ((0, xk_ref, kc_ref), (1, xv_ref, vc_ref)):
        load = lambda s, x_ref=x_ref: x_ref[pl.ds(s, n_seg, stride=CMP_STRIDE), :]
        out_ref[...] = _compress_math(load, n_seg, w1_ref, w2_ref, pos_ref, kv).astype(BF16)


def _compress_prompt(z, w1, w2, pos, b, t):
    n_seg = t // CMP_STRIDE
    return pl.pallas_call(
        _compress_prompt_kernel,
        grid=(b,),
        in_specs=[pl.BlockSpec((t, KV_W), lambda i: (i, C_KVC // KV_W)),
                  pl.BlockSpec((t, KV_W), lambda i: (i, C_KVC // KV_W + 1)),
                  _const_spec(w1.shape), _const_spec(w2.shape), _const_spec(pos.shape)],
        out_specs=[pl.BlockSpec((None, n_seg, KV_W), lambda i: (i, 0, 0)),
                   pl.BlockSpec((None, n_seg, KV_W), lambda i: (i, 0, 0))],
        out_shape=[jax.ShapeDtypeStruct((b, n_seg, KV_W), BF16),
                   jax.ShapeDtypeStruct((b, n_seg, KV_W), BF16)],
        compiler_params=_cparams(("parallel",)),
        name="compress_prompt",
    )(z, z, w1, w2, pos)


PB = 256
N_AUG = 16
VROWS = HD + 16
LOG2E = math.log2(math.e)


def _split3(x):
    hi = x.astype(BF16).astype(F32)
    r = x - hi
    mid = r.astype(BF16).astype(F32)
    return hi, mid, r - mid


def _nsa_prompt256_kernel(q_ref, gn_ref, kc_ref, vct_ref, zks_ref, zvs_ref, zkw_ref, zvw_ref,
                          augc_ref, slopes_ref, covert_ref, o_ref,
                          qa_ref, m_ref, acc_ref, sel_ref, score_ref, ks_ref, vst_ref, kw_ref, vwt_ref,
                          sa_ref, sb_ref):
    i = pl.program_id(1)
    t0 = i * PB

    @pl.when(i == 0)
    def _():
        ones_rows = jnp.where(lax.broadcasted_iota(jnp.int32, (VROWS - HD, PB), 0) == 0, 1.0, 0.0)
        for zk, zv, k_dst, vt_dst in ((zks_ref, zvs_ref, ks_ref, vst_ref), (zkw_ref, zvw_ref, kw_ref, vwt_ref)):
            for j in range(k_dst.shape[0]):
                k_dst[j] = zk[j * PB:(j + 1) * PB, :].astype(BF16)
                vt = zv[j * PB:(j + 1) * PB, :].T
                vt_dst[j] = jnp.concatenate(
                    [piece for kvh in range(KVH) for piece in (vt[kvh * HD:(kvh + 1) * HD, :], ones_rows)],
                    axis=0).astype(BF16)
    n_seg = kc_ref.shape[0]
    n_sel = covert_ref.shape[0]
    nl = NSA_HEADS * PB
    half = GROUP * PB
    slopes2 = slopes_ref[...] * LOG2E
    tlane = (lax.broadcasted_iota(jnp.int32, (1, nl), 1) % PB).astype(F32)
    blocks_per_tile = PB // SEL_BLOCK

    q = q_ref[...] * (SCALE * LOG2E)
    zero = jnp.zeros((HD, PB), F32)
    cols = []
    for kvh in range(KVH):
        for p in range(GROUP // 2):
            c0 = (kvh * (GROUP // 2) + p) * LANE
            blk = q[:, c0:c0 + LANE].T
            for hh in range(2):
                piece = blk[hh * HD:(hh + 1) * HD, :]
                cols.append(jnp.concatenate([piece, zero] if kvh == 0 else [zero, piece], axis=0))
    qbdt = jnp.concatenate(cols, axis=1).astype(BF16)
    qa_ref[0:KV_W, :] = qbdt
    qa_ref[KV_W + N_AUG:2 * KV_W, :] = jnp.zeros((KV_W - N_AUG, nl), BF16)

    lane = lax.broadcasted_iota(jnp.int32, (PB, PB), 1)
    sub = lax.broadcasted_iota(jnp.int32, (PB, PB), 0)
    causal_bias = jnp.where(lane >= sub, 0.0, NEG)
    edge_bias = jnp.where(lane < sub, 0.0, NEG)

    sc = jnp.dot(kc_ref[...], qbdt, preferred_element_type=F32)
    n_i = lax.broadcasted_iota(jnp.int32, (n_seg, PB), 0)
    t_i = lax.broadcasted_iota(jnp.int32, (n_seg, PB), 1)
    d1 = (t0 + t_i - (n_i * CMP_STRIDE + (CMP_BLOCK - 1))).astype(F32)
    dc = jnp.concatenate([d1] * NSA_HEADS, axis=1)
    okc = dc >= 0.0
    sc = jnp.where(okc, sc - slopes2 * dc, NEG)
    mc = jnp.max(sc, axis=0, keepdims=True)
    pc = jnp.where(okc, jnp.exp2(sc - mc), 0.0)
    lc = jnp.sum(pc, axis=0, keepdims=True)
    pc = pc * (1.0 / jnp.where(lc > 0.0, lc, 1.0))
    pcb = pc.astype(BF16)
    ocmp = [jnp.dot(vct_ref[kvh * HD:(kvh + 1) * HD, :], pcb[:, kvh * half:(kvh + 1) * half],
                    preferred_element_type=F32) for kvh in range(KVH)]

    jblk = lax.broadcasted_iota(jnp.int32, (n_sel, PB), 0)
    qblk = (t0 + lax.broadcasted_iota(jnp.int32, (n_sel, PB), 1)) // SEL_BLOCK
    back = qblk - jblk
    visible = back >= 0
    forced = (jblk == 0) | (visible & (back < SEL_LOCAL))
    n_groups = jnp.minimum((t0 + PB - 1) // SEL_BLOCK // SUBLANE + 1, n_sel // SUBLANE)
    for kvh in range(KVH):
        psum = pc[:, kvh * half:kvh * half + PB]
        for g in range(1, GROUP):
            psum = psum + pc[:, kvh * half + g * PB:kvh * half + (g + 1) * PB]
        imp = jnp.dot(covert_ref[...], psum, precision=HI, preferred_element_type=F32)
        score = jnp.where(forced, 1e9, jnp.where(visible, imp, -1e9))
        score_ref[...] = score

        def rank_group(gi, rank):
            rows8 = score_ref[pl.ds(pl.multiple_of(gi * SUBLANE, SUBLANE), SUBLANE), :]
            for u in range(SUBLANE):
                row = rows8[u:u + 1, :]
                ge = jnp.where(row >= score, 1.0, 0.0)
                gt = jnp.where(row > score, 1.0, 0.0)
                rank = rank + jnp.where(jblk > gi * SUBLANE + u, ge, gt)
            return rank

        rank = lax.fori_loop(0, n_groups, rank_group, jnp.zeros((n_sel, PB), F32))
        sel_ref[kvh] = jnp.where(visible, jnp.where(rank < float(min(SEL_TOP, n_sel)), 0.0, NEG), NEG)

    sl3 = _split3(slopes2)

    def scores(k_ref, jt, use_sel, dst_ref):
        off = (t0 - jt * PB).astype(F32)
        c3 = _split3(-slopes2 * (tlane + off))
        rows = list(sl3) + list(c3)
        if use_sel:
            tiles_per_group = SUBLANE // blocks_per_tile
            base = pl.multiple_of((jt // tiles_per_group) * SUBLANE, SUBLANE)
            which = jt % tiles_per_group
            per_kvh = []
            for kvh in range(KVH):
                rows8 = sel_ref[kvh, pl.ds(base, SUBLANE), :]
                mine = rows8[0:blocks_per_tile, :]
                for w in range(1, tiles_per_group):
                    mine = jnp.where(which == w, rows8[w * blocks_per_tile:(w + 1) * blocks_per_tile, :], mine)
                per_kvh.append(mine)
            rows.append(jnp.concatenate([per_kvh[kvh] for kvh in range(KVH) for _ in range(GROUP)], axis=1))
        rows.append(jnp.zeros((N_AUG - sum(r.shape[0] for r in rows), nl), F32))
        qa_ref[KV_W:KV_W + N_AUG, :] = jnp.concatenate(rows, axis=0).astype(BF16)
        ka = jnp.concatenate([k_ref[jt], augc_ref[...]], axis=1)
        s = jnp.dot(ka, qa_ref[...], preferred_element_type=F32)
        dst_ref[0:PB, :] = s
        dst_ref[PB:PB + 1, :] = jnp.max(s, axis=0, keepdims=True)

    def softmax_pv(vt_ref, jt, src_ref, extra):
        s = src_ref[0:PB, :]
        if extra is not None:
            s = s + jnp.concatenate([extra] * NSA_HEADS, axis=1)
            s_max = jnp.max(s, axis=0, keepdims=True)
        else:
            s_max = src_ref[PB:PB + 1, :]
        m_old = m_ref[...]
        m_new = jnp.maximum(m_old, s_max)
        alpha = jnp.exp2(m_old - m_new)
        p = jnp.exp2(s - m_new)
        m_ref[...] = m_new
        pb = p.astype(BF16)
        vt = vt_ref[jt]
        for kvh in range(KVH):
            pv = jnp.dot(vt[kvh * VROWS:(kvh + 1) * VROWS, :], pb[:, kvh * half:(kvh + 1) * half],
                         preferred_element_type=F32)
            acc_ref[kvh] = alpha[:, kvh * half:(kvh + 1) * half] * acc_ref[kvh] + pv

    def reset():
        m_ref[...] = jnp.full((1, nl), M_INIT, F32)
        acc_ref[...] = jnp.zeros((KVH, VROWS, half), F32)

    def result():
        return [acc_ref[kvh, 0:HD, :] * (1.0 / acc_ref[kvh, HD:HD + 1, :]) for kvh in range(KVH)]

    reset()
    n_pairs = (i + 2) // 2
    last = pl.num_programs(1) - 1
    odd_i = (i % 2) == 1
    scores(ks_ref, 0, True, sa_ref)

    def pair(u, carry):
        scores(ks_ref, 2 * u + 1, True, sb_ref)
        softmax_pv(vst_ref, 2 * u, sa_ref, None)
        scores(ks_ref, 2 * u + 2, True, sa_ref)
        softmax_pv(vst_ref, 2 * u + 1, sb_ref, None)
        return carry

    lax.fori_loop(0, n_pairs - 1, pair, 0)
    j0 = 2 * n_pairs - 2
    j1 = jnp.minimum(j0 + 1, last)
    scores(ks_ref, j1, True, sb_ref)
    softmax_pv(vst_ref, j0, sa_ref, jnp.where(odd_i, 0.0, causal_bias))
    softmax_pv(vst_ref, j1, sb_ref, jnp.where(odd_i, causal_bias, NEG))
    osel = result()

    reset()
    w0 = jnp.maximum(i - 2, 0)
    w1 = jnp.maximum(i - 1, 0)
    scores(kw_ref, w0, False, sa_ref)
    scores(kw_ref, w1, False, sb_ref)
    softmax_pv(vwt_ref, w0, sa_ref, jnp.where(i >= 2, edge_bias, NEG))
    scores(kw_ref, i, False, sa_ref)
    softmax_pv(vwt_ref, w1, sb_ref, jnp.where(i >= 1, 0.0, jnp.full((PB, PB), NEG, F32)))
    softmax_pv(vwt_ref, i, sa_ref, causal_bias)
    owin = result()

    gt_ = _sigmoid(gn_ref[...]).T
    for kvh in range(KVH):
        for p in range(GROUP // 2):
            pieces = []
            for hh in range(2):
                g = 2 * p + hh
                h = kvh * GROUP + g
                cs = slice(g * PB, (g + 1) * PB)
                pieces.append(gt_[3 * h:3 * h + 1, :] * ocmp[kvh][:, cs]
                              + gt_[3 * h + 1:3 * h + 2, :] * osel[kvh][:, cs]
                              + gt_[3 * h + 2:3 * h + 3, :] * owin[kvh][:, cs])
            c0 = (kvh * (GROUP // 2) + p) * LANE
            o_ref[:, c0:c0 + LANE] = jnp.concatenate(pieces, axis=0).T.astype(BF16)


def _aug_key_columns():
    a = np.zeros((PB, KV_W), np.float32)
    s = np.arange(PB)
    a[:, 0:3] = s[:, None]
    a[:, 3:6] = 1.0
    for r in range(PB // SEL_BLOCK):
        a[:, 6 + r] = (s // SEL_BLOCK == r)
    return a


def _nsa_prompt256(z, kc, vct, b, t):
    nq = t // PB
    n_seg = t // CMP_STRIDE
    n_sel = t // SEL_BLOCK
    nl = NSA_HEADS * PB
    assert 6 + PB // SEL_BLOCK <= N_AUG and n_sel % SUBLANE == 0 and WINDOW == 2 * PB
    slopes = _alibi_slopes_lanes(PB)
    covert = _cover_matrix(n_seg, n_sel).T.copy()
    kv_col = lambda c0: pl.BlockSpec((t, KV_W), lambda bi, i: (bi, c0 // KV_W))
    return pl.pallas_call(
        _nsa_prompt256_kernel,
        grid=(b, nq),
        in_specs=[pl.BlockSpec((PB, NSA_W), lambda bi, i: (bi * nq + i, C_Q // NSA_W)),
                  pl.BlockSpec((PB, LANE), lambda bi, i: (bi * nq + i, C_GN // LANE)),
                  pl.BlockSpec((None, n_seg, KV_W), lambda bi, i: (bi, 0, 0)),
                  pl.BlockSpec((None, KV_W, n_seg), lambda bi, i: (bi, 0, 0)),
                  kv_col(C_KVS), kv_col(C_KVS + KV_W), kv_col(C_KVW), kv_col(C_KVW + KV_W),
                  _const_spec((PB, KV_W)), _const_spec((1, nl)), _const_spec((n_sel, n_seg))],
        out_specs=pl.BlockSpec((PB, NSA_W), lambda bi, i: (bi * nq + i, 0)),
        out_shape=jax.ShapeDtypeStruct((b * t, NSA_W), BF16),
        scratch_shapes=[pltpu.VMEM((2 * KV_W, nl), BF16), pltpu.VMEM((1, nl), F32),
                        pltpu.VMEM((KVH, VROWS, GROUP * PB), F32), pltpu.VMEM((KVH, n_sel, PB), F32),
                        pltpu.VMEM((n_sel, PB), F32),
                        pltpu.VMEM((nq, PB, KV_W), BF16), pltpu.VMEM((nq, KVH * VROWS, PB), BF16),
                        pltpu.VMEM((nq, PB, KV_W), BF16), pltpu.VMEM((nq, KVH * VROWS, PB), BF16),
                        pltpu.VMEM((PB + SUBLANE, nl), F32), pltpu.VMEM((PB + SUBLANE, nl), F32)],
        compiler_params=_cparams(("parallel", "arbitrary")),
        name="nsa_prompt",
    )(z, z, kc, vct, z, z, z, z, jnp.asarray(_aug_key_columns(), BF16), jnp.asarray(slopes),
      jnp.asarray(covert))


def _alibi_slopes_lanes(width):
    h = np.arange(1, NSA_HEADS + 1, dtype=np.float32)
    return np.repeat(np.exp2(-8.0 * h / NSA_HEADS), width)[None, :].astype(np.float32)


def _cover_matrix(n_cmp_rows, n_sel_rows):
    n = np.arange(n_cmp_rows)[:, None] * CMP_STRIDE
    j = np.arange(n_sel_rows)[None, :] * SEL_BLOCK
    return ((n < j + SEL_BLOCK) & (n + CMP_BLOCK > j)).astype(np.float32)


def _hgrn_gates(pre, lb):
    e = jnp.exp(-pre)
    r = 1.0 / (1.0 + e)
    log_f = jnp.log(lb + (1.0 - lb) * r)
    k = (1.0 - lb) * jnp.where(pre > 0.0, e * r, 1.0 - r)
    return log_f, k


def _hgrn_out(o, gate, ng):
    outs = []
    for h in range(HG_H):
        sl = slice(h * HG_D, (h + 1) * HG_D)
        g = gate[:, sl]
        outs.append(_rms(o[:, sl], ng) * (g * _sigmoid(g)))
    return jnp.concatenate(outs, axis=1)


def _hgrn_prompt_kernel(hq_ref, hf_ref, hi_ref, hg_ref, lb_ref, ng_ref, tri_ref, o_ref, st_ref,
                        s_ref, oraw_ref):
    ci = pl.program_id(1)
    tc = hq_ref.shape[0]
    c = HG_CHUNK

    @pl.when(ci == 0)
    def _():
        s_ref[...] = jnp.zeros(s_ref.shape, F32)

    lb = lb_ref[...]
    tril = (lax.broadcasted_iota(jnp.int32, (c, c), 0) >= lax.broadcasted_iota(jnp.int32, (c, c), 1))
    row8 = lax.broadcasted_iota(jnp.int32, (SUBLANE, HG_D), 0)

    n_chunks = tc // c
    q_all = hq_ref[...]
    v_all = hi_ref[...]
    log_f, k_all = _hgrn_gates(hf_ref[...], lb)
    chunks = []
    worst = None
    for g in range(n_chunks):
        rows = slice(g * c, (g + 1) * c)
        bcum = jnp.dot(tri_ref[...], log_f[rows], precision=HI, preferred_element_type=F32)
        e = bcum - bcum[c // 2 - 1:c // 2, :]
        chunks.append((rows, bcum, e))
        worst = jnp.abs(e) if worst is None else jnp.maximum(worst, jnp.abs(e))
    safe = jnp.max(worst) < HG_SAFE_EXP

    @pl.when(safe)
    def _():
        for rows, bcum, e in chunks:
            q, k, v = q_all[rows], k_all[rows], v_all[rows]
            b_last = bcum[c - 1:c, :]
            qt = (q * jnp.exp(e)).astype(BF16)
            kt = (k * jnp.exp(-e)).astype(BF16)
            qb = (q * jnp.exp(bcum)).astype(BF16)
            kh = (k * jnp.exp(b_last - bcum)).astype(BF16)
            dec = jnp.exp(b_last)
            vb = v.astype(BF16)
            def head(h, rows=rows, qt=qt, kt=kt, qb=qb, kh=kh, dec=dec, vb=vb):
                sl = slice(h * HG_D, (h + 1) * HG_D)
                a = lax.dot_general(qt[:, sl], kt[:, sl], (((1,), (1,)), ((), ())), preferred_element_type=F32)
                st = s_ref[h]
                o = lax.dot_general(qb[:, sl], st.astype(BF16), (((1,), (1,)), ((), ())),
                                    preferred_element_type=F32)
                upd = lax.dot_general(vb[:, sl], kh[:, sl], (((0,), (0,)), ((), ())), preferred_element_type=F32)
                yield
                a = jnp.where(tril, a, 0.0).astype(BF16)
                s_ref[h] = st * dec[:, sl] + upd
                yield
                oraw_ref[rows, sl] = o + jnp.dot(a, vb[:, sl], preferred_element_type=F32)
                yield

            for _ in zip(*[head(h) for h in range(HG_H)]):
                pass

    @pl.when(jnp.logical_not(safe))
    def _():
        for h in range(HG_H):
            sl = slice(h * HG_D, (h + 1) * HG_D)

            def tile(ti, carry, sl=sl, h=h):
                r = pl.multiple_of(ti * SUBLANE, SUBLANE)
                q8 = hq_ref[pl.ds(r, SUBLANE), sl]
                v8 = hi_ref[pl.ds(r, SUBLANE), sl]
                lf8, k8 = _hgrn_gates(hf_ref[pl.ds(r, SUBLANE), sl], lb[:, sl])
                f8 = jnp.exp(lf8)
                st = s_ref[h]
                rows_out = []
                for u in range(SUBLANE):
                    vu = jnp.where(row8 == 0, jnp.broadcast_to(v8[u:u + 1, :], (SUBLANE, HG_D)), 0.0)
                    ku = jnp.broadcast_to(k8[u:u + 1, :], (SUBLANE, HG_D))
                    qu = jnp.broadcast_to(q8[u:u + 1, :], (SUBLANE, HG_D))
                    st = st * f8[u:u + 1, :] + lax.dot_general(
                        vu, ku, (((0,), (0,)), ((), ())), precision=HI, preferred_element_type=F32)
                    ou = lax.dot_general(qu, st, (((1,), (1,)), ((), ())), precision=HI,
                                         preferred_element_type=F32)
                    rows_out.append(ou[0:1, :])
                s_ref[h] = st
                oraw_ref[pl.ds(r, SUBLANE), sl] = jnp.concatenate(rows_out, axis=0)
                return carry

            lax.fori_loop(0, tc // SUBLANE, tile, 0)

    o_ref[...] = _hgrn_out(oraw_ref[...], hg_ref[...], ng_ref[...]).astype(BF16)

    @pl.when(ci == pl.num_programs(1) - 1)
    def _():
        for h in range(HG_H):
            st_ref[h] = s_ref[h].T


def _hgrn_prompt(z, lb, ng, b, t, tc):
    nc = t // tc
    col = lambda c0: pl.BlockSpec((tc, HG_W), lambda bi, ci: (bi * nc + ci, c0 // HG_W))
    tri = np.tril(np.ones((HG_CHUNK, HG_CHUNK), np.float32))
    return pl.pallas_call(
        _hgrn_prompt_kernel,
        grid=(b, nc),
        in_specs=[col(C_HQ), col(C_HF), col(C_HI), col(C_HG),
                  _const_spec((1, HG_W)), _const_spec((1, HG_D)), _const_spec((HG_CHUNK, HG_CHUNK))],
        out_specs=[pl.BlockSpec((tc, HG_W), lambda bi, ci: (bi * nc + ci, 0)),
                   pl.BlockSpec((None, HG_H, HG_D, HG_D), lambda bi, ci: (bi, 0, 0, 0))],
        out_shape=[jax.ShapeDtypeStruct((b * t, HG_W), BF16),
                   jax.ShapeDtypeStruct((b, HG_H, HG_D, HG_D), F32)],
        scratch_shapes=[pltpu.VMEM((HG_H, HG_D, HG_D), F32), pltpu.VMEM((tc, HG_W), F32)],
        compiler_params=_cparams(("parallel", "arbitrary")),
        name="hgrn_prompt",
    )(z, z, z, z, lb, ng, jnp.asarray(tri))


def _merge_kernel(on_ref, oh_ref, ga_ref, gb_ref, x_ref, wn_ref, wh_ref, wo_ref, g_ref, o_ref):
    y = (_sigmoid(ga_ref[...]) * jnp.dot(on_ref[...], wn_ref[...], preferred_element_type=F32)
         + _sigmoid(gb_ref[...]) * jnp.dot(oh_ref[...], wh_ref[...], preferred_element_type=F32))
    mix = jnp.dot(y.astype(BF16), wo_ref[...], preferred_element_type=F32)
    o_ref[...] = x_ref[...] + _rms(mix, g_ref[...])


def _resident(shape):
    nd = len(shape)
    return pl.BlockSpec(shape, lambda *_: (0,) * nd, pipeline_mode=pl.Buffered(1))


def _merge(o_nsa, o_hg, z, x, wn, wh, wo, g, tm):
    n = x.shape[0]
    row = lambda w, cb: pl.BlockSpec((tm, w), lambda i: (i, cb))
    return pl.pallas_call(
        _merge_kernel,
        grid=(n // tm,),
        in_specs=[row(NSA_W, 0), row(HG_W, 0), row(D_MODEL, C_GM // D_MODEL), row(D_MODEL, C_GM // D_MODEL + 1),
                  row(D_MODEL, 0), _resident(wn.shape), _resident(wh.shape), _resident(wo.shape),
                  _resident((1, D_MODEL))],
        out_specs=row(D_MODEL, 0),
        out_shape=jax.ShapeDtypeStruct((n, D_MODEL), F32),
        compiler_params=_cparams(("parallel",)),
        name="merge_out_proj",
    )(o_nsa, o_hg, z, z, x, wn, wh, wo, g)


def _mlp_kernel(h_ref, p_ref, wu_ref, wd_ref, wg_ref, wp_ref, g1_ref, g2_ref, g3_ref, o_ref):
    h = h_ref[...]
    xn = _rms(h, g1_ref[...]).astype(BF16)
    ffn = jnp.zeros(h.shape, F32)
    step = D_MODEL // 2
    for c0 in range(0, D_FF, step):
        up = jnp.dot(xn, wu_ref[:, c0:c0 + step], preferred_element_type=F32)
        act = jnp.square(jnp.maximum(up, 0.0)).astype(BF16)
        ffn = ffn + jnp.dot(act, wd_ref[c0:c0 + step, :], preferred_element_type=F32)
    h = h + _rms(ffn, g2_ref[...])
    gate = _sigmoid(jnp.dot(_rms(h, g3_ref[...]).astype(BF16), wg_ref[...], preferred_element_type=F32))
    o_ref[...] = h + gate * jnp.dot(p_ref[...].astype(BF16), wp_ref[...], preferred_element_type=F32)


def _mlp(h, p, wu, wd, wg, wp, g1, g2, g3, tm):
    n = h.shape[0]
    row = lambda w: pl.BlockSpec((tm, w), lambda i: (i, 0))
    gain = _resident((1, D_MODEL))
    return pl.pallas_call(
        _mlp_kernel,
        grid=(n // tm,),
        in_specs=[row(D_MODEL), row(PLE_DIM), _resident(wu.shape), _resident(wd.shape), _resident(wg.shape),
                  _resident(wp.shape), gain, gain, gain],
        out_specs=row(D_MODEL),
        out_shape=jax.ShapeDtypeStruct((n, D_MODEL), F32),
        compiler_params=_cparams(("parallel",)),
        name="mlp_ple",
    )(h, p, wu, wd, wg, wp, g1, g2, g3)


def _prep_w_in(w):
    sizes = (NSA_W, 2 * KV_W, 2 * KV_W, 2 * KV_W, 3 * NSA_HEADS, HG_W, HG_W, HG_W, HG_W)
    q, kvc, kvs, kvw, gn, hq, hf, hi, hg, gm = jnp.split(w, [int(v) for v in np.cumsum(sizes)], axis=1)
    pad = jnp.zeros((w.shape[0], Z_COLS - C_GN - 3 * NSA_HEADS), w.dtype)
    return jnp.concatenate([gm, q, hq, hf, hi, hg, kvc, kvs, kvw, gn, pad], axis=1).astype(BF16)


def _prep_compress(w1k, w2k, w1v, w2v, pos):
    eye = jnp.eye(KVH, dtype=F32)

    def big1(w1):
        t = jnp.einsum('rsdh,kq->rskdqh', w1, eye)
        return t.reshape(CMP_BLOCK // CMP_STRIDE, CMP_STRIDE * KV_W, KVH * CMP_HIDDEN)

    def big2(w2):
        return jnp.einsum('hd,kq->khqd', w2, eye).reshape(KVH * CMP_HIDDEN, KV_W)

    w1 = jnp.stack([big1(w1k), big1(w1v)]).astype(BF16)
    w2 = jnp.stack([big2(w2k), big2(w2v)]).astype(BF16)
    posb = jnp.broadcast_to(pos[:, :, None, :], pos.shape[:2] + (KVH, HD)).reshape(pos.shape[0], 1, -1)
    posb = jnp.broadcast_to(posb, (pos.shape[0], SUBLANE, posb.shape[-1])).astype(BF16)
    return w1, w2, posb


def _kv_rows_t_kernel(ck_ref, cv_ref, sk_ref, sv_ref, wk_ref, wv_ref, ct_ref, st_ref, wt_ref):
    rows = ck_ref.shape[0]
    for src, dst, part in ((ck_ref, ct_ref, 0), (cv_ref, ct_ref, 1), (sk_ref, st_ref, 0), (sv_ref, st_ref, 1),
                           (wk_ref, wt_ref, 0), (wv_ref, wt_ref, 1)):
        for j in range(rows // LANE):
            dst[part * KV_W:(part + 1) * KV_W, j * LANE:(j + 1) * LANE] = src[j * LANE:(j + 1) * LANE, :].T


def _kv_rows_t(z, b, t, rows):
    nt = t // rows
    col = lambda c0: pl.BlockSpec((rows, KV_W), lambda bi, j: (bi * nt + j, c0 // KV_W))
    full = pl.BlockSpec((None, 2 * KV_W, rows), lambda bi, j: (bi, 0, j))
    tail = pl.BlockSpec((None, 2 * KV_W, rows), lambda bi, j: (bi, 0, 0))
    return pl.pallas_call(
        _kv_rows_t_kernel,
        grid=(b, nt),
        in_specs=[col(C_KVC), col(C_KVC + KV_W), col(C_KVS), col(C_KVS + KV_W), col(C_KVW), col(C_KVW + KV_W)],
        out_specs=[full, full, tail],
        out_shape=[jax.ShapeDtypeStruct((b, 2 * KV_W, t), F32), jax.ShapeDtypeStruct((b, 2 * KV_W, t), F32),
                   jax.ShapeDtypeStruct((b, 2 * KV_W, rows), F32)],
        compiler_params=_cparams(("parallel", "arbitrary")),
        name="kv_rows_t",
    )(z, z, z, z, z, z)


def _prep_layer(i, lb_all, w_in, cmp_k_w1, cmp_k_w2, cmp_v_w1, cmp_v_w2, cmp_pos, hg_norm, w_branch_nsa,
                w_branch_hgrn, w_out, norm_pre_mix, norm_post_mix, norm_pre_mlp, norm_post_mlp, w_mlp_up,
                w_mlp_down, norm_ple, w_ple_gate, w_ple_proj):
    w1, w2, posb = _prep_compress(cmp_k_w1[i], cmp_k_w2[i], cmp_v_w1[i], cmp_v_w2[i], cmp_pos[i])
    row = lambda a: a[i].reshape(1, -1).astype(F32)
    return {
        'w_in': _prep_w_in(w_in[i]), 'cmp_w1': w1, 'cmp_w2': w2, 'cmp_pos': posb,
        'hg_lb': lb_all[i].reshape(1, HG_W), 'hg_norm': row(hg_norm),
        'w_bn': w_branch_nsa[i].astype(BF16), 'w_bh': w_branch_hgrn[i].astype(BF16), 'w_out': w_out[i].astype(BF16),
        'w_up': w_mlp_up[i].astype(BF16), 'w_down': w_mlp_down[i].astype(BF16),
        'w_gate': w_ple_gate[i].astype(BF16), 'w_proj': w_ple_proj[i].astype(BF16),
        'g_pre_mix': row(norm_pre_mix), 'g_post_mix': row(norm_post_mix), 'g_pre_mlp': row(norm_pre_mlp),
        'g_post_mlp': row(norm_post_mlp), 'g_ple': row(norm_ple),
    }


def _layer_prompt(x, p, lw, b, t):
    n = b * t
    z = _norm_matmul(x, lw['g_pre_mix'], lw['w_in'], tm=min(1024, n), tn=Z_COLS // 4)
    wb = min(WINDOW, t)
    kv_c, kv_s, kv_w = _kv_rows_t(z, b, t, wb)
    kc, vc = _compress_prompt(z, lw['cmp_w1'], lw['cmp_w2'], lw['cmp_pos'], b, t)
    vct = jnp.swapaxes(vc, 1, 2)
    o_nsa = _nsa_prompt256(z, kc, vct, b, t)
    o_hg, st = _hgrn_prompt(z, lw['hg_lb'], lw['hg_norm'], b, t, tc=min(256, t))
    h1 = _merge(o_nsa, o_hg, z, x, lw['w_bn'], lw['w_bh'], lw['w_out'], lw['g_post_mix'], tm=min(512, n))
    h2 = _mlp(h1, p, lw['w_up'], lw['w_down'], lw['w_gate'], lw['w_proj'],
              lw['g_pre_mlp'], lw['g_post_mlp'], lw['g_ple'], tm=min(1024, n))
    kv6 = lambda a: a.reshape(b, 2, KVH, HD, a.shape[-1]).transpose(0, 4, 1, 2, 3)
    return h2, kv6(kv_c), kv6(kv_s), kv6(kv_w), st


SEG_PITCH = 24


def _page_fetch(pt_ref, cache_ref, buf_ref, sem, nseq=1):
    b = pl.program_id(0)
    n_pages = pt_ref.shape[1]
    slot = b % 2

    def copy(step, sl, p):
        seq = step * nseq + p // n_pages
        return pltpu.make_async_copy(cache_ref.at[pt_ref[seq, p % n_pages]], buf_ref.at[sl, p], sem.at[sl])

    def start(step, sl):
        lax.fori_loop(0, nseq * n_pages, lambda p, c: (copy(step, sl, p).start(), c)[1], 0)

    @pl.when(b == 0)
    def _():
        start(0, 0)

    @pl.when(b + 1 < pl.num_programs(0))
    def _():
        start(b + 1, 1 - slot)

    def wait():
        lax.fori_loop(0, nseq * n_pages, lambda p, c: (copy(b, slot, p).wait(), c)[1], 0)

    return slot, wait


def _compress_sample_kernel(pt_ref, cache_ref, w1_ref, w2_ref, pos_ref, kc_ref, vc_ref, buf_ref, rows_ref, sem):
    n_pages, page = pt_ref.shape[1], buf_ref.shape[-1]
    n_seg = n_pages * page // CMP_STRIDE
    slot, wait = _page_fetch(pt_ref, cache_ref, buf_ref, sem)
    wait()
    segs_per_page = page // CMP_STRIDE
    for kv, out_ref in ((0, kc_ref), (1, vc_ref)):
        for p in range(n_pages):
            rows = buf_ref[slot, p, kv].T
            for g in range(segs_per_page):
                r0 = (p * segs_per_page + g) * SEG_PITCH
                rows_ref[kv, r0:r0 + CMP_STRIDE, :] = rows[g * CMP_STRIDE:(g + 1) * CMP_STRIDE, :]
        load = lambda s, kv=kv: rows_ref[kv, pl.ds(s, n_seg, stride=SEG_PITCH), :]
        out_ref[...] = _compress_math(load, n_seg, w1_ref, w2_ref, pos_ref, kv).astype(BF16)


def _compress_sample(page_table, cache_t, w1, w2, pos):
    bs, n_pages = page_table.shape
    page = cache_t.shape[-1]
    n_seg = n_pages * page // CMP_STRIDE
    const = lambda shape: pl.BlockSpec(shape, lambda i, pt: (0,) * len(shape), pipeline_mode=pl.Buffered(1))
    out = pl.BlockSpec((None, n_seg, KV_W), lambda i, pt: (i, 0, 0))
    return pl.pallas_call(
        _compress_sample_kernel,
        grid_spec=pltpu.PrefetchScalarGridSpec(
            num_scalar_prefetch=1, grid=(bs,),
            in_specs=[pl.BlockSpec(memory_space=pl.ANY), const(w1.shape), const(w2.shape), const(pos.shape)],
            out_specs=[out, out],
            scratch_shapes=[pltpu.VMEM((2, n_pages, 2, KV_W, page), F32),
                            pltpu.VMEM((2, n_seg * SEG_PITCH, KV_W), F32), pltpu.SemaphoreType.DMA((2,))]),
        out_shape=[jax.ShapeDtypeStruct((bs, n_seg, KV_W), BF16)] * 2,
        compiler_params=_cparams(("arbitrary",)),
        name="compress_sample",
    )(page_table, cache_t, w1, w2, pos)


_NT = (((1,), (1,)), ((), ()))


def _nsa_sample_kernel(pt_ref, qbd_ref, gl_ref, kc_ref, vc_ref, snew_ref, cwin_ref, wnew_ref, csel_ref,
                       slope_ref, tq_ref, tq8_ref, cover_ref, gsum_ref, gexp_ref, spread_ref, o_ref,
                       buf_ref, s_ref, sem):
    nseq = qbd_ref.shape[0]
    slot, wait = _page_fetch(pt_ref, csel_ref, buf_ref, sem, nseq)
    wait()
    chains = [_nsa_sample_one(pt_ref.shape[1], u, slot, qbd_ref.at[u], gl_ref.at[u], kc_ref.at[u], vc_ref.at[u],
                              snew_ref.at[u], cwin_ref.at[u], wnew_ref.at[u], slope_ref, tq_ref, tq8_ref,
                              cover_ref, gsum_ref, gexp_ref, spread_ref, o_ref.at[u], buf_ref, s_ref.at[u])
              for u in range(nseq)]
    for _ in zip(*chains):
        pass


def _nsa_sample_one(n_pages, u, slot, qbd_ref, gl_ref, kc_ref, vc_ref, snew_ref, cwin_ref, wnew_ref,
                    slope_ref, tq_ref, tq8_ref, cover_ref, gsum_ref, gexp_ref, spread_ref, o_ref, buf_ref, s_ref):
    page = buf_ref.shape[-1]
    past = n_pages * page
    n_seg = kc_ref.shape[0]
    npad = cover_ref.shape[1]
    n_sel = past // SEL_BLOCK + 1
    nr = qbd_ref.shape[0]
    wb = cwin_ref.shape[-1]

    qb = (qbd_ref[...] * SCALE).astype(BF16)
    slope = slope_ref[...]
    qpos = tq_ref[...] + float(past)
    zeros_pad = jnp.zeros((QB - snew_ref.shape[0], KV_W), F32)
    t_new = lax.broadcasted_iota(jnp.int32, (1, QB), 1).astype(F32)

    sc = lax.dot_general(qb, kc_ref[...], _NT, preferred_element_type=F32)
    n_i = lax.broadcasted_iota(jnp.int32, (1, n_seg), 1)
    dcmp = qpos - (n_i * CMP_STRIDE + (CMP_BLOCK - 1)).astype(F32)
    okc = dcmp >= 0.0
    sc = jnp.where(okc, sc - slope * dcmp, NEG)
    yield
    mc = jnp.max(sc, axis=-1, keepdims=True)
    pc = jnp.where(okc, jnp.exp(sc - mc), 0.0)
    lc = jnp.sum(pc, axis=-1, keepdims=True)
    pc = pc * (1.0 / jnp.where(lc > 0.0, lc, 1.0))
    yield
    o_cmp = jnp.dot(pc.astype(BF16), vc_ref[...], preferred_element_type=F32)

    psum = jnp.dot(gsum_ref[...], pc, precision=HI, preferred_element_type=F32)
    imp = jnp.dot(psum, cover_ref[...], precision=HI, preferred_element_type=F32)
    yield
    nq8 = gsum_ref.shape[0]
    jblk = lax.broadcasted_iota(jnp.int32, (nq8, npad), 1)
    qblk = (tq8_ref[...].astype(jnp.int32) + past) // SEL_BLOCK
    back = qblk - jblk
    visible = back >= 0
    forced = (jblk == 0) | (visible & (back < SEL_LOCAL))
    score = jnp.where(forced, 1e9, jnp.where(visible, imp, -1e9))
    rank = jnp.zeros((nq8, npad), F32)
    for jp in range(n_sel):
        col = score[:, jp:jp + 1]
        ge = jnp.where(col >= score, 1.0, 0.0)
        gt = jnp.where(col > score, 1.0, 0.0)
        rank = rank + jnp.where(jblk > jp, ge, gt)
        if jp % 16 == 15:
            yield
    sel8 = jnp.where(visible, jnp.where(rank < float(min(SEL_TOP, n_sel)), 1.0, 0.0), 0.0)
    negb = (jnp.dot(gexp_ref[...], sel8, precision=HI, preferred_element_type=F32) - 1.0) * (-NEG)
    yield

    def tile_bias(jt):
        lane = lax.broadcasted_iota(jnp.int32, (nr, QB), 1)
        return jnp.where(lane < SEL_BLOCK, negb[:, 2 * jt:2 * jt + 1], negb[:, 2 * jt + 1:2 * jt + 2])

    ppc = 4
    ck = ppc * page
    bpc = ck // SEL_BLOCK

    def chunk_t(c, kv):
        return jnp.concatenate([buf_ref[slot, u * n_pages + c * ppc + w, kv] for w in range(ppc)],
                               axis=1).astype(BF16)

    slope_ck = jnp.concatenate([jnp.broadcast_to(slope, (nr, LANE))] * (ck // LANE), axis=1)
    qpos_ck = jnp.concatenate([jnp.broadcast_to(qpos, (nr, LANE))] * (ck // LANE), axis=1)
    negb_bf = negb.astype(BF16)
    for c in range(n_pages // ppc):
        s = jnp.dot(qb, chunk_t(c, 0), preferred_element_type=F32)
        kpos = (lax.broadcasted_iota(jnp.int32, (1, ck), 1) + c * ck).astype(F32)
        b0 = c * bpc
        bias = jnp.dot(negb_bf[:, (b0 // LANE) * LANE:(b0 // LANE + 1) * LANE], spread_ref[(b0 % LANE) // bpc],
                       preferred_element_type=F32)
        s_ref[:, c * ck:(c + 1) * ck] = s - slope_ck * (qpos_ck - kpos) + bias
        if c % 4 == 3:
            yield
    knew =jnp.concatenate([snew_ref[:, 0:KV_W], zeros_pad], axis=0).astype(BF16)
    vnew = jnp.concatenate([snew_ref[:, KV_W:2 * KV_W], zeros_pad], axis=0).astype(BF16)
    dnew = tq_ref[...] - t_new
    s = lax.dot_general(qb, knew, _NT, preferred_element_type=F32)
    s_ref[:, past:past + QB] = jnp.where(dnew >= 0.0, s - slope * dnew + tile_bias(past // QB), NEG)
    s_all = s_ref[...]
    ms = jnp.max(s_all, axis=-1, keepdims=True)
    ps = jnp.exp(s_all - ms)
    ls = jnp.sum(ps, axis=-1, keepdims=True)
    yield
    psb = ps.astype(BF16)
    o_sel = jnp.dot(psb[:, past:past + QB], vnew, preferred_element_type=F32)
    for c in range(n_pages // ppc):
        o_sel = o_sel + lax.dot_general(psb[:, c * ck:(c + 1) * ck], chunk_t(c, 1), _NT,
                                        preferred_element_type=F32)
        if c % 4 == 3:
            yield
    o_sel = o_sel * (1.0 / ls)

    s1 = jnp.dot(qb, cwin_ref[0].astype(BF16), preferred_element_type=F32)
    d1 = float(wb) + tq_ref[...] - lax.broadcasted_iota(jnp.int32, (1, wb), 1).astype(F32)
    ok1 = d1 < float(WINDOW)
    s1 = jnp.where(ok1, s1 - slope * d1, NEG)
    yield
    wk =jnp.concatenate([wnew_ref[:, 0:KV_W], zeros_pad], axis=0).astype(BF16)
    wv = jnp.concatenate([wnew_ref[:, KV_W:2 * KV_W], zeros_pad], axis=0).astype(BF16)
    ok2 = dnew >= 0.0
    s2 = jnp.where(ok2, lax.dot_general(qb, wk, _NT, preferred_element_type=F32) - slope * dnew, NEG)
    mw = jnp.maximum(jnp.max(s1, axis=-1, keepdims=True), jnp.max(s2, axis=-1, keepdims=True))
    p1 = jnp.where(ok1, jnp.exp(s1 - mw), 0.0)
    p2 = jnp.where(ok2, jnp.exp(s2 - mw), 0.0)
    lw_ = jnp.sum(p1, axis=-1, keepdims=True) + jnp.sum(p2, axis=-1, keepdims=True)
    yield
    o_win =(lax.dot_general(p1.astype(BF16), cwin_ref[1].astype(BF16), _NT, preferred_element_type=F32)
             + jnp.dot(p2.astype(BF16), wv, preferred_element_type=F32)) * (1.0 / lw_)

    sig = _sigmoid(gl_ref[...])
    o_ref[...] = sig[:, 0:1] * o_cmp + sig[:, 1:2] * o_sel + sig[:, 2:3] * o_win
    yield


def _nsa_sample(page_table, qbd, gl, kc, vc, snew, cwin, wnew, csel, ts):
    bs, n_pages = page_table.shape
    page = csel.shape[-1]
    past = n_pages * page
    n_seg = kc.shape[1]
    n_sel = past // SEL_BLOCK + 1
    npad = -(-(n_sel + 1) // LANE) * LANE
    nr = KVH * GROUP * ts
    wb = cwin.shape[-1]
    r = np.arange(nr)
    slope = np.exp2(-8.0 * ((r // ts) + 1) / NSA_HEADS).astype(np.float32)[:, None]
    tq = (r % ts).astype(np.float32)[:, None]
    r8 = np.arange(KVH * ts)
    tq8 = (r8 % ts).astype(np.float32)[:, None]
    cover = np.zeros((n_seg, npad), np.float32)
    cover[:, :n_sel] = _cover_matrix(n_seg, n_sel)
    gsum = ((r[None, :] // (GROUP * ts) == r8[:, None] // ts) & (r[None, :] % ts == r8[:, None] % ts)).astype(np.float32)
    ck = 4 * page
    bpc = ck // SEL_BLOCK
    j_i = np.arange(LANE)[None, :, None]
    spread = (j_i == (np.arange(LANE // bpc)[:, None, None] * bpc + np.arange(ck)[None, None, :] // SEL_BLOCK))
    consts = [slope, tq, tq8, cover, gsum, gsum.T.copy(), jnp.asarray(spread, BF16)]
    const = lambda shape: pl.BlockSpec(shape, lambda i, pt: (0,) * len(shape), pipeline_mode=pl.Buffered(1))
    nseq = 2 if bs % 2 == 0 else 1
    per = lambda *s: pl.BlockSpec((nseq,) + s, lambda i, pt: (i,) + (0,) * len(s))
    return pl.pallas_call(
        _nsa_sample_kernel,
        grid_spec=pltpu.PrefetchScalarGridSpec(
            num_scalar_prefetch=1, grid=(bs // nseq,),
            in_specs=[per(nr, KV_W), per(nr, LANE), per(n_seg, KV_W), per(n_seg, KV_W), per(SUBLANE, 2 * KV_W),
                      per(2, KV_W, wb), per(SUBLANE, 2 * KV_W), pl.BlockSpec(memory_space=pl.ANY)]
                     + [const(c.shape) for c in consts],
            out_specs=per(nr, KV_W),
            scratch_shapes=[pltpu.VMEM((2, nseq * n_pages, 2, KV_W, page), F32),
                            pltpu.VMEM((nseq, nr, past + QB), F32), pltpu.SemaphoreType.DMA((2,))]),
        out_shape=jax.ShapeDtypeStruct((bs, nr, KV_W), F32),
        compiler_params=_cparams(("arbitrary",)),
        name="nsa_sample",
    )(page_table, qbd, gl, kc, vc, snew, cwin, wnew, csel, *[jnp.asarray(c) for c in consts])


def _hgrn_sample_kernel(ts, hq_ref, hf_ref, hi_ref, hg_ref, lb_ref, ng_ref, s0_ref, o_ref, s1_ref):
    rows = hq_ref.shape[0]
    q = hq_ref[...]
    v = hi_ref[...]
    log_f, k = _hgrn_gates(hf_ref[...], lb_ref[...])
    tloc = lax.broadcasted_iota(jnp.int32, (rows, HG_W), 0) % ts
    up = lambda a, d: pltpu.roll(a, d, axis=0)
    down = lambda a, d: pltpu.roll(a, rows - d, axis=0)

    bcum = log_f
    for d in range(1, ts):
        bcum = bcum + jnp.where(tloc >= d, up(log_f, d), 0.0)
    b_last = bcum
    for d in range(1, ts):
        b_last = jnp.where(tloc == ts - 1 - d, down(bcum, d), b_last)

    o_intra = [jnp.zeros((rows, HG_D), F32) for _ in range(HG_H)]
    for d in range(ts):
        kd, bd, vd = (k, bcum, v) if d == 0 else (up(k, d), up(bcum, d), up(v, d))
        w = jnp.where(tloc >= d, q * kd * jnp.exp(jnp.where(tloc >= d, bcum - bd, 0.0)), 0.0)
        for h in range(HG_H):
            sl = slice(h * HG_D, (h + 1) * HG_D)
            o_intra[h] = o_intra[h] + jnp.sum(w[:, sl], axis=-1, keepdims=True) * vd[:, sl]

    qb = q * jnp.exp(bcum)
    kh = k * jnp.exp(b_last - bcum)
    per_tile = SUBLANE // ts
    row8 = lax.broadcasted_iota(jnp.int32, (SUBLANE, HG_D), 0) // ts
    tiles = []
    for j in range(rows // SUBLANE):
        r8 = slice(j * SUBLANE, (j + 1) * SUBLANE)
        heads = []
        for h in range(HG_H):
            sl = slice(h * HG_D, (h + 1) * HG_D)
            o_inter = jnp.zeros((SUBLANE, HG_D), F32)
            for u in range(per_tile):
                seq = j * per_tile + u
                mine = row8 == u
                s0 = s0_ref[seq, h]
                o_inter = o_inter + jnp.dot(jnp.where(mine, qb[r8, sl], 0.0).astype(BF16), s0.astype(BF16),
                                            preferred_element_type=F32)
                upd = lax.dot_general(jnp.where(mine, kh[r8, sl], 0.0).astype(BF16), v[r8, sl].astype(BF16),
                                      (((0,), (0,)), ((), ())), preferred_element_type=F32)
                r_last = j * SUBLANE + u * ts + ts - 1
                dec = jnp.exp(bcum[r_last:r_last + 1, sl])
                s1_ref[seq, h] = jnp.broadcast_to(dec, (HG_D, HG_D)).T * s0 + upd
            heads.append(o_inter + o_intra[h][r8, :])
        tiles.append(jnp.concatenate(heads, axis=1))
    o = jnp.concatenate(tiles, axis=0)
    o_ref[...] = _hgrn_out(o, hg_ref[...], ng_ref[...]).astype(BF16)


def _hgrn_sample(z, lb, ng, s0, bs, ts, nb):
    rows = nb * ts
    col = lambda c0: pl.BlockSpec((rows, HG_W), lambda i: (i, c0 // HG_W))
    st = pl.BlockSpec((nb, HG_H, HG_D, HG_D), lambda i: (i, 0, 0, 0))
    return pl.pallas_call(
        functools.partial(_hgrn_sample_kernel, ts),
        grid=(bs // nb,),
        in_specs=[col(C_HQ), col(C_HF), col(C_HI), col(C_HG), _const_spec((1, HG_W)), _const_spec((1, HG_D)), st],
        out_specs=[pl.BlockSpec((rows, HG_W), lambda i: (i, 0)), st],
        out_shape=[jax.ShapeDtypeStruct((bs * ts, HG_W), BF16),
                   jax.ShapeDtypeStruct((bs, HG_H, HG_D, HG_D), F32)],
        compiler_params=_cparams(("parallel",)),
        name="hgrn_sample",
    )(z, z, z, z, lb, ng, s0)


def _layer_sample(x, p, cache_cmp, cache_sel, cache_win, state, page_table, lw, bs, ts):
    n = bs * ts
    assert SUBLANE % ts == 0 and n % SUBLANE == 0
    z = _norm_matmul(x, lw['g_pre_mix'], lw['w_in'], tm=min(512, n), tn=512)
    kv_c = z[:, C_KVC:C_KVC + 2 * KV_W]
    kv_s = z[:, C_KVS:C_KVS + 2 * KV_W]
    kv_w = z[:, C_KVW:C_KVW + 2 * KV_W]

    rows_last = lambda a: jnp.transpose(a, (0, 2, 3, 4, 1)).reshape(a.shape[0], 2, KV_W, a.shape[1])
    kc, vc = _compress_sample(page_table, rows_last(cache_cmp), lw['cmp_w1'], lw['cmp_w2'], lw['cmp_pos'])

    eye = jnp.eye(KVH, dtype=F32)
    q5 = z[:, C_Q:C_Q + NSA_W].reshape(bs, ts, KVH, GROUP, HD).transpose(0, 2, 3, 1, 4)
    qbd = jnp.einsum('bkgtd,kq->bkgtqd', q5, eye).reshape(bs, KVH * GROUP * ts, KV_W)
    g5 = z[:, C_GN:C_GN + 3 * NSA_HEADS].reshape(bs, ts, KVH, GROUP, 3).transpose(0, 2, 3, 1, 4)
    gl = jnp.pad(g5.reshape(bs, KVH * GROUP * ts, 3), ((0, 0), (0, 0), (0, LANE - 3)))
    pad_rows = lambda a: jnp.pad(a.reshape(bs, ts, 2 * KV_W), ((0, 0), (0, SUBLANE - ts), (0, 0)))
    o_rows = _nsa_sample(page_table, qbd, gl, kc, vc, pad_rows(kv_s), rows_last(cache_win), pad_rows(kv_w),
                         rows_last(cache_sel), ts)
    o6 = o_rows.reshape(bs, KVH, GROUP, ts, KVH, HD)
    o_nsa = jnp.stack([o6[:, kvh, :, :, kvh, :] for kvh in range(KVH)], axis=1)
    o_nsa = o_nsa.transpose(0, 3, 1, 2, 4).reshape(n, NSA_W).astype(BF16)

    o_hg, st = _hgrn_sample(z, lw['hg_lb'], lw['hg_norm'], state, bs, ts, nb=min(8, bs))
    h1 = _merge(o_nsa, o_hg, z, x, lw['w_bn'], lw['w_bh'], lw['w_out'], lw['g_post_mix'], tm=min(512, n))
    h2 = _mlp(h1, p, lw['w_up'], lw['w_down'], lw['w_gate'], lw['w_proj'],
              lw['g_pre_mlp'], lw['g_post_mlp'], lw['g_ple'], tm=min(512, n))
    kv6 = lambda a: a.reshape(bs, ts, 2, KVH, HD)
    win_buf = jnp.concatenate([cache_win, kv6(kv_w)], axis=1)[:, ts:]
    return h2, kv6(kv_c), kv6(kv_s), win_buf, st


def kernel(x_prompt, x_sample, cache_cmp_kv, cache_sel_kv, cache_win_kv, state_hgrn, page_table, p_prompt,
           p_sample, w_in, cmp_k_w1, cmp_k_w2, cmp_v_w1, cmp_v_w2, cmp_pos, hg_lb_logits, hg_norm, w_branch_nsa,
           w_branch_hgrn, w_out, norm_pre_mix, norm_post_mix, norm_pre_mlp, norm_post_mlp, w_mlp_up, w_mlp_down,
           norm_ple, w_ple_gate, w_ple_proj):
    depth = w_in.shape[0]
    b, t, d = x_prompt.shape
    bs, ts, _ = x_sample.shape
    lb_all = jnp.cumsum(jax.nn.softmax(hg_lb_logits.astype(F32), axis=0), axis=0)
    h_p = x_prompt.reshape(b * t, d)
    h_s = x_sample.reshape(bs * ts, d)
    outs = [[] for _ in range(8)]
    for i in range(depth):
        lw = _prep_layer(i, lb_all, w_in, cmp_k_w1, cmp_k_w2, cmp_v_w1, cmp_v_w2, cmp_pos, hg_norm, w_branch_nsa,
                         w_branch_hgrn, w_out, norm_pre_mix, norm_post_mix, norm_pre_mlp, norm_post_mlp, w_mlp_up,
                         w_mlp_down, norm_ple, w_ple_gate, w_ple_proj)
        h_p, *res_p = _layer_prompt(h_p, p_prompt[i].reshape(b * t, -1), lw, b, t)
        h_s, *res_s = _layer_sample(h_s, p_sample[i].reshape(bs * ts, -1), cache_cmp_kv[i], cache_sel_kv[i],
                                    cache_win_kv[i], state_hgrn[i], page_table, lw, bs, ts)
        for lst, v in zip(outs, res_p + res_s):
            lst.append(v)
    return (h_p.reshape(b, t, d), h_s.reshape(bs, ts, d)) + tuple(jnp.stack(lst, axis=0) for lst in outs)
```

```python
import functools
import itertools
import math

import numpy as np
import jax
import jax.numpy as jnp
from jax import lax
from jax.experimental import pallas as pl
from jax.experimental.pallas import tpu as pltpu

F32 = jnp.float32
BF16 = jnp.bfloat16

D_MODEL = 1024
NSA_HEADS = 8
KVH = 2
GROUP = NSA_HEADS // KVH
HD = 64
NSA_W = NSA_HEADS * HD
KV_W = KVH * HD
CMP_BLOCK = 32
CMP_STRIDE = 16
CMP_HIDDEN = 2 * HD
SEL_BLOCK = 64
SEL_TOP = 16
SEL_LOCAL = 2
WINDOW = 512
HG_W = 512
HG_H = 4
HG_D = 128
HG_CHUNK = 64
D_FF = 4 * D_MODEL
PLE_DIM = 256
RMS_EPS = 1e-6
NEG = -1e30
M_INIT = -1e20
SCALE = HD ** -0.5
HG_SAFE_EXP = 60.0

LANE = 128
SUBLANE = 8
VMEM_LIMIT = 48 * 1024 * 1024

C_GM = 0
C_Q = 2048
C_HQ = 2560
C_HF = 3072
C_HI = 3584
C_HG = 4096
C_KVC = 4608
C_KVS = 4864
C_KVW = 5120
C_GN = 5376
Z_COLS = 5632
QB = 128
HI = lax.Precision.HIGHEST


def _cparams(sem, vmem=VMEM_LIMIT):
    return pltpu.CompilerParams(dimension_semantics=sem, vmem_limit_bytes=vmem)


def _rms(x, g):
    return x * lax.rsqrt(jnp.mean(x * x, axis=-1, keepdims=True) + RMS_EPS) * g


def _sigmoid(x):
    return 1.0 / (1.0 + jnp.exp(-x))


def _gelu_tanh(x):
    return 0.5 * x * (1.0 + jnp.tanh(math.sqrt(2.0 / math.pi) * (x + 0.044715 * (x * x * x))))


def _const_spec(shape):
    nd = len(shape)
    return pl.BlockSpec(shape, lambda *_: (0,) * nd)


def _norm_matmul_kernel(x_ref, g_ref, w_ref, o_ref, xn_ref):
    @pl.when(pl.program_id(1) == 0)
    def _():
        xn_ref[...] = _rms(x_ref[...], g_ref[...]).astype(BF16)

    tn = o_ref.shape[1]
    col = pl.multiple_of(pl.program_id(1) * tn, LANE)
    o_ref[...] = jnp.dot(xn_ref[...], w_ref[:, pl.ds(col, tn)], preferred_element_type=F32)


def _norm_matmul(x, g, w, tm, tn):
    n, d = x.shape
    c = w.shape[1]
    return pl.pallas_call(
        _norm_matmul_kernel,
        grid=(n // tm, c // tn),
        in_specs=[pl.BlockSpec((tm, d), lambda i, j: (i, 0)),
                  pl.BlockSpec((1, d), lambda i, j: (0, 0)),
                  pl.BlockSpec((d, c), lambda i, j: (0, 0), pipeline_mode=pl.Buffered(1))],
        out_specs=pl.BlockSpec((tm, tn), lambda i, j: (i, j)),
        out_shape=jax.ShapeDtypeStruct((n, c), F32),
        scratch_shapes=[pltpu.VMEM((tm, d), BF16)],
        compiler_params=_cparams(("parallel", "arbitrary")),
        name="norm_in_proj",
    )(x, g, w)


def _compress_math(load_rows, n_seg, w1_ref, w2_ref, pos_ref, kv):
    x = jnp.concatenate([load_rows(s) for s in range(CMP_STRIDE)], axis=1).astype(BF16)
    h0 = jnp.dot(x, w1_ref[kv, 0], preferred_element_type=F32)
    h1 = jnp.dot(x, w1_ref[kv, 1], preferred_element_type=F32)
    posb = (jnp.dot(pos_ref[0], w1_ref[kv, 0], preferred_element_type=F32)
            + jnp.dot(pos_ref[1], w1_ref[kv, 1], preferred_element_type=F32))
    hid = h0 + pltpu.roll(h1, n_seg - 1, axis=0) + posb[0:1]
    return jnp.dot(_gelu_tanh(hid).astype(BF16), w2_ref[kv], preferred_element_type=F32)


def _compress_prompt_kernel(xk_ref, xv_ref, w1_ref, w2_ref, pos_ref, kc_ref, vc_ref):
    n_seg = xk_ref.shape[0] // CMP_STRIDE
    for kv, x_ref, out_ref in ((0, xk_ref, kc_ref), (1, xv_ref, vc_ref)):
        load = lambda s, x_ref=x_ref: x_ref[pl.ds(s, n_seg, stride=CMP_STRIDE), :]
        out_ref[...] = _compress_math(load, n_seg, w1_ref, w2_ref, pos_ref, kv).astype(BF16)


def _compress_prompt(z, w1, w2, pos, b, t):
    n_seg = t // CMP_STRIDE
    return pl.pallas_call(
        _compress_prompt_kernel,
        grid=(b,),
        in_specs=[pl.BlockSpec((t, KV_W), lambda i: (i, C_KVC // KV_W)),
                  pl.BlockSpec((t, KV_W), lambda i: (i, C_KVC // KV_W + 1)),
                  _const_spec(w1.shape), _const_spec(w2.shape), _const_spec(pos.shape)],
        out_specs=[pl.BlockSpec((None, n_seg, KV_W), lambda i: (i, 0, 0)),
                   pl.BlockSpec((None, n_seg, KV_W), lambda i: (i, 0, 0))],
        out_shape=[jax.ShapeDtypeStruct((b, n_seg, KV_W), BF16),
                   jax.ShapeDtypeStruct((b, n_seg, KV_W), BF16)],
        compiler_params=_cparams(("parallel",)),
        name="compress_prompt",
    )(z, z, w1, w2, pos)


PB = 256
N_AUG = 16
VROWS = HD + 16
LOG2E = math.log2(math.e)


def _split3(x):
    hi = x.astype(BF16).astype(F32)
    r = x - hi
    mid = r.astype(BF16).astype(F32)
    return hi, mid, r - mid


def _nsa_prompt256_kernel(q_ref, gn_ref, kc_ref, vct_ref, zks_ref, zvs_ref, zkw_ref, zvw_ref,
                          augc_ref, slopes_ref, covert_ref, o_ref,
                          qa_ref, m_ref, acc_ref, sel_ref, score_ref, ks_ref, vst_ref, kw_ref, vwt_ref,
                          sa_ref, sb_ref, sc_ref, sd_ref, mw_ref, accw_ref, ocmp_ref, imp_ref):
    i = pl.program_id(1)
    t0 = i * PB

    @pl.when(i == 0)
    def _():
        ones_rows = jnp.where(lax.broadcasted_iota(jnp.int32, (VROWS - HD, PB), 0) == 0, 1.0, 0.0)
        for zk, zv, k_dst, vt_dst in ((zks_ref, zvs_ref, ks_ref, vst_ref), (zkw_ref, zvw_ref, kw_ref, vwt_ref)):
            for j in range(k_dst.shape[0]):
                k_dst[j] = zk[j * PB:(j + 1) * PB, :].astype(BF16)
                vt = zv[j * PB:(j + 1) * PB, :].T
                vt_dst[j] = jnp.concatenate(
                    [piece for kvh in range(KVH) for piece in (vt[kvh * HD:(kvh + 1) * HD, :], ones_rows)],
                    axis=0).astype(BF16)
    n_seg = kc_ref.shape[0]
    n_sel = covert_ref.shape[0]
    nl = NSA_HEADS * PB
    half = GROUP * PB
    slopes2 = slopes_ref[...] * LOG2E
    tlane = (lax.broadcasted_iota(jnp.int32, (1, nl), 1) % PB).astype(F32)
    blocks_per_tile = PB // SEL_BLOCK

    q = q_ref[...] * (SCALE * LOG2E)
    zero = jnp.zeros((HD, PB), F32)
    cols = []
    for kvh in range(KVH):
        for p in range(GROUP // 2):
            c0 = (kvh * (GROUP // 2) + p) * LANE
            blk = q[:, c0:c0 + LANE].T
            for hh in range(2):
                piece = blk[hh * HD:(hh + 1) * HD, :]
                cols.append(jnp.concatenate([piece, zero] if kvh == 0 else [zero, piece], axis=0))
    qbdt = jnp.concatenate(cols, axis=1).astype(BF16)
    qa_ref[0:KV_W, :] = qbdt
    qa_ref[KV_W + N_AUG:2 * KV_W, :] = jnp.zeros((KV_W - N_AUG, nl), BF16)

    lane = lax.broadcasted_iota(jnp.int32, (PB, PB), 1)
    sub = lax.broadcasted_iota(jnp.int32, (PB, PB), 0)
    causal_bias = jnp.where(lane >= sub, 0.0, NEG)
    edge_bias = jnp.where(lane < sub, 0.0, NEG)

    def compressed(rows):
        sc = jnp.dot(kc_ref[0:rows, :], qbdt, preferred_element_type=F32)
        n_i = lax.broadcasted_iota(jnp.int32, (rows, PB), 0)
        t_i = lax.broadcasted_iota(jnp.int32, (rows, PB), 1)
        d1 = (t0 + t_i - (n_i * CMP_STRIDE + (CMP_BLOCK - 1))).astype(F32)
        dc = jnp.concatenate([d1] * NSA_HEADS, axis=1)
        okc = dc >= 0.0
        sc = jnp.where(okc, sc - slopes2 * dc, NEG)
        mc = jnp.max(sc, axis=0, keepdims=True)
        pc = jnp.where(okc, jnp.exp2(sc - mc), 0.0)
        lc = jnp.sum(pc, axis=0, keepdims=True)
        pc = pc * (1.0 / jnp.where(lc > 0.0, lc, 1.0))
        pcb = pc.astype(BF16)
        for kvh in range(KVH):
            ocmp_ref[kvh] = jnp.dot(vct_ref[kvh * HD:(kvh + 1) * HD, 0:rows], pcb[:, kvh * half:(kvh + 1) * half],
                                    preferred_element_type=F32)
            psum = pc[:, kvh * half:kvh * half + PB]
            for g in range(1, GROUP):
                psum = psum + pc[:, kvh * half + g * PB:kvh * half + (g + 1) * PB]
            imp_ref[kvh] = jnp.dot(covert_ref[:, 0:rows], psum, precision=HI, preferred_element_type=F32)

    half_rows = n_seg // 2
    if half_rows % LANE == 0:
        first_full = -(-(half_rows * CMP_STRIDE + CMP_BLOCK) // PB) - 1
        pl.when(i < first_full)(lambda: compressed(half_rows))
        pl.when(i >= first_full)(lambda: compressed(n_seg))
    else:
        compressed(n_seg)

    jblk = lax.broadcasted_iota(jnp.int32, (n_sel, PB), 0)
    qblk = (t0 + lax.broadcasted_iota(jnp.int32, (n_sel, PB), 1)) // SEL_BLOCK
    back = qblk - jblk
    visible = back >= 0
    forced = (jblk == 0) | (visible & (back < SEL_LOCAL))
    n_groups = jnp.minimum((t0 + PB - 1) // SEL_BLOCK // SUBLANE + 1, n_sel // SUBLANE)
    for kvh in range(KVH):
        score = jnp.where(forced, 1e9, jnp.where(visible, imp_ref[kvh], -1e9))
        score_ref[...] = score

        def rank_group(gi, rank):
            rows8 = score_ref[pl.ds(pl.multiple_of(gi * SUBLANE, SUBLANE), SUBLANE), :]
            for u in range(SUBLANE):
                row = rows8[u:u + 1, :]
                ge = jnp.where(row >= score, 1.0, 0.0)
                gt = jnp.where(row > score, 1.0, 0.0)
                rank = rank + jnp.where(jblk > gi * SUBLANE + u, ge, gt)
            return rank

        rank = lax.fori_loop(0, n_groups, rank_group, jnp.zeros((n_sel, PB), F32))
        sel_ref[kvh] = jnp.where(visible, jnp.where(rank < float(min(SEL_TOP, n_sel)), 0.0, NEG), NEG)

    sl3 = _split3(slopes2)

    def scores(k_ref, jt, use_sel, dst_ref):
        off = (t0 - jt * PB).astype(F32)
        c3 = _split3(-slopes2 * (tlane + off))
        rows = list(sl3) + list(c3)
        if use_sel:
            tiles_per_group = SUBLANE // blocks_per_tile
            base = pl.multiple_of((jt // tiles_per_group) * SUBLANE, SUBLANE)
            which = jt % tiles_per_group
            per_kvh = []
            for kvh in range(KVH):
                rows8 = sel_ref[kvh, pl.ds(base, SUBLANE), :]
                mine = rows8[0:blocks_per_tile, :]
                for w in range(1, tiles_per_group):
                    mine = jnp.where(which == w, rows8[w * blocks_per_tile:(w + 1) * blocks_per_tile, :], mine)
                per_kvh.append(mine)
            rows.append(jnp.concatenate([per_kvh[kvh] for kvh in range(KVH) for _ in range(GROUP)], axis=1))
        rows.append(jnp.zeros((N_AUG - sum(r.shape[0] for r in rows), nl), F32))
        qa_ref[KV_W:KV_W + N_AUG, :] = jnp.concatenate(rows, axis=0).astype(BF16)
        ka = jnp.concatenate([k_ref[jt], augc_ref[...]], axis=1)
        s = jnp.dot(ka, qa_ref[...], preferred_element_type=F32)
        dst_ref[0:PB, :] = s
        dst_ref[PB:PB + 1, :] = jnp.max(s, axis=0, keepdims=True)

    def softmax_pv(vt_ref, jt, src_ref, extra, state):
        m_st, acc_st = state
        s = src_ref[0:PB, :]
        if extra is not None:
            s = s + jnp.concatenate([extra] * NSA_HEADS, axis=1)
            s_max = jnp.max(s, axis=0, keepdims=True)
        else:
            s_max = src_ref[PB:PB + 1, :]
        m_old = m_st[...]
        m_new = jnp.maximum(m_old, s_max)
        alpha = jnp.exp2(m_old - m_new)
        p = jnp.exp2(s - m_new)
        m_st[...] = m_new
        pb = p.astype(BF16)
        vt = vt_ref[jt]
        for kvh in range(KVH):
            pv = jnp.dot(vt[kvh * VROWS:(kvh + 1) * VROWS, :], pb[:, kvh * half:(kvh + 1) * half],
                         preferred_element_type=F32)
            acc_st[kvh] = alpha[:, kvh * half:(kvh + 1) * half] * acc_st[kvh] + pv

    def reset(state):
        m_st, acc_st = state
        m_st[...] = jnp.full((1, nl), M_INIT, F32)
        acc_st[...] = jnp.zeros((KVH, VROWS, half), F32)

    def result(state):
        acc_st = state[1]
        return [acc_st[kvh, 0:HD, :] * (1.0 / acc_st[kvh, HD:HD + 1, :]) for kvh in range(KVH)]

    sel_state = (m_ref, acc_ref)
    win_state = (mw_ref, accw_ref)

    reset(sel_state)
    n_pairs = (i + 2) // 2
    last = pl.num_programs(1) - 1
    odd_i = (i % 2) == 1
    scores(ks_ref, 0, True, sa_ref)

    def pair(u, carry):
        scores(ks_ref, 2 * u + 1, True, sb_ref)
        softmax_pv(vst_ref, 2 * u, sa_ref, None, sel_state)
        scores(ks_ref, 2 * u + 2, True, sa_ref)
        softmax_pv(vst_ref, 2 * u + 1, sb_ref, None, sel_state)
        return carry

    lax.fori_loop(0, n_pairs - 1, pair, 0)
    j0 = 2 * n_pairs - 2
    j1 = jnp.minimum(j0 + 1, last)

    def sel_tail():
        scores(ks_ref, j1, True, sb_ref)
        yield
        softmax_pv(vst_ref, j0, sa_ref, jnp.where(odd_i, 0.0, causal_bias), sel_state)
        yield
        softmax_pv(vst_ref, j1, sb_ref, jnp.where(odd_i, causal_bias, NEG), sel_state)
        yield

    w0 = jnp.maximum(i - 2, 0)
    w1 = jnp.maximum(i - 1, 0)

    def window():
        reset(win_state)
        scores(kw_ref, w0, False, sc_ref)
        yield
        scores(kw_ref, w1, False, sd_ref)
        softmax_pv(vwt_ref, w0, sc_ref, jnp.where(i >= 2, edge_bias, NEG), win_state)
        yield
        scores(kw_ref, i, False, sc_ref)
        softmax_pv(vwt_ref, w1, sd_ref, jnp.where(i >= 1, 0.0, jnp.full((PB, PB), NEG, F32)), win_state)
        yield
        softmax_pv(vwt_ref, i, sc_ref, causal_bias, win_state)
        yield

    for _ in itertools.zip_longest(sel_tail(), window()):
        pass
    osel = result(sel_state)
    owin = result(win_state)

    gt_ = _sigmoid(gn_ref[...]).T
    for kvh in range(KVH):
        for p in range(GROUP // 2):
            pieces = []
            for hh in range(2):
                g = 2 * p + hh
                h = kvh * GROUP + g
                cs = slice(g * PB, (g + 1) * PB)
                pieces.append(gt_[3 * h:3 * h + 1, :] * ocmp_ref[kvh, :, cs]
                              + gt_[3 * h + 1:3 * h + 2, :] * osel[kvh][:, cs]
                              + gt_[3 * h + 2:3 * h + 3, :] * owin[kvh][:, cs])
            c0 = (kvh * (GROUP // 2) + p) * LANE
            o_ref[:, c0:c0 + LANE] = jnp.concatenate(pieces, axis=0).T.astype(BF16)


def _aug_key_columns():
    a = np.zeros((PB, KV_W), np.float32)
    s = np.arange(PB)
    a[:, 0:3] = s[:, None]
    a[:, 3:6] = 1.0
    for r in range(PB // SEL_BLOCK):
        a[:, 6 + r] = (s // SEL_BLOCK == r)
    return a


def _nsa_prompt256(z, kc, vct, b, t):
    nq = t // PB
    n_seg = t // CMP_STRIDE
    n_sel = t // SEL_BLOCK
    nl = NSA_HEADS * PB
    assert 6 + PB // SEL_BLOCK <= N_AUG and n_sel % SUBLANE == 0 and WINDOW == 2 * PB
    slopes = _alibi_slopes_lanes(PB)
    covert = _cover_matrix(n_seg, n_sel).T.copy()
    kv_col = lambda c0: pl.BlockSpec((t, KV_W), lambda bi, i: (bi, c0 // KV_W))
    return pl.pallas_call(
        _nsa_prompt256_kernel,
        grid=(b, nq),
        in_specs=[pl.BlockSpec((PB, NSA_W), lambda bi, i: (bi * nq + i, C_Q // NSA_W)),
                  pl.BlockSpec((PB, LANE), lambda bi, i: (bi * nq + i, C_GN // LANE)),
                  pl.BlockSpec((None, n_seg, KV_W), lambda bi, i: (bi, 0, 0)),
                  pl.BlockSpec((None, KV_W, n_seg), lambda bi, i: (bi, 0, 0)),
                  kv_col(C_KVS), kv_col(C_KVS + KV_W), kv_col(C_KVW), kv_col(C_KVW + KV_W),
                  _const_spec((PB, KV_W)), _const_spec((1, nl)), _const_spec((n_sel, n_seg))],
        out_specs=pl.BlockSpec((PB, NSA_W), lambda bi, i: (bi * nq + i, 0)),
        out_shape=jax.ShapeDtypeStruct((b * t, NSA_W), BF16),
        scratch_shapes=[pltpu.VMEM((2 * KV_W, nl), BF16), pltpu.VMEM((1, nl), F32),
                        pltpu.VMEM((KVH, VROWS, GROUP * PB), F32), pltpu.VMEM((KVH, n_sel, PB), F32),
                        pltpu.VMEM((n_sel, PB), F32),
                        pltpu.VMEM((nq, PB, KV_W), BF16), pltpu.VMEM((nq, KVH * VROWS, PB), BF16),
                        pltpu.VMEM((nq, PB, KV_W), BF16), pltpu.VMEM((nq, KVH * VROWS, PB), BF16),
                        pltpu.VMEM((PB + SUBLANE, nl), F32), pltpu.VMEM((PB + SUBLANE, nl), F32),
                        pltpu.VMEM((PB + SUBLANE, nl), F32), pltpu.VMEM((PB + SUBLANE, nl), F32),
                        pltpu.VMEM((1, nl), F32), pltpu.VMEM((KVH, VROWS, GROUP * PB), F32),
                        pltpu.VMEM((KVH, HD, GROUP * PB), F32), pltpu.VMEM((KVH, n_sel, PB), F32)],
        compiler_params=_cparams(("parallel", "arbitrary")),
        name="nsa_prompt",
    )(z, z, kc, vct, z, z, z, z, jnp.asarray(_aug_key_columns(), BF16), jnp.asarray(slopes),
      jnp.asarray(covert))


def _alibi_slopes_lanes(width):
    h = np.arange(1, NSA_HEADS + 1, dtype=np.float32)
    return np.repeat(np.exp2(-8.0 * h / NSA_HEADS), width)[None, :].astype(np.float32)


def _cover_matrix(n_cmp_rows, n_sel_rows):
    n = np.arange(n_cmp_rows)[:, None] * CMP_STRIDE
    j = np.arange(n_sel_rows)[None, :] * SEL_BLOCK
    return ((n < j + SEL_BLOCK) & (n + CMP_BLOCK > j)).astype(np.float32)


def _hgrn_gates(pre, lb):
    e = jnp.exp(-pre)
    r = 1.0 / (1.0 + e)
    log_f = jnp.log(lb + (1.0 - lb) * r)
    k = (1.0 - lb) * jnp.where(pre > 0.0, e * r, 1.0 - r)
    return log_f, k


def _hgrn_out(o, gate, ng):
    outs = []
    for h in range(HG_H):
        sl = slice(h * HG_D, (h + 1) * HG_D)
        g = gate[:, sl]
        outs.append(_rms(o[:, sl], ng) * (g * _sigmoid(g)))
    return jnp.concatenate(outs, axis=1)


def _hgrn_prompt_kernel(hq_ref, hf_ref, hi_ref, hg_ref, lb_ref, ng_ref, tri_ref, o_ref, st_ref,
                        s_ref, oraw_ref):
    ci = pl.program_id(1)
    tc = hq_ref.shape[0]
    c = HG_CHUNK

    @pl.when(ci == 0)
    def _():
        s_ref[...] = jnp.zeros(s_ref.shape, F32)

    lb = lb_ref[...]
    tril = (lax.broadcasted_iota(jnp.int32, (c, c), 0) >= lax.broadcasted_iota(jnp.int32, (c, c), 1))
    row8 = lax.broadcasted_iota(jnp.int32, (SUBLANE, HG_D), 0)

    n_chunks = tc // c
    q_all = hq_ref[...]
    v_all = hi_ref[...]
    log_f, k_all = _hgrn_gates(hf_ref[...], lb)
    chunks = []
    worst = None
    for g in range(n_chunks):
        rows = slice(g * c, (g + 1) * c)
        bcum = jnp.dot(tri_ref[...], log_f[rows], precision=HI, preferred_element_type=F32)
        e = bcum - bcum[c // 2 - 1:c // 2, :]
        chunks.append((rows, bcum, e))
        worst = jnp.abs(e) if worst is None else jnp.maximum(worst, jnp.abs(e))
    safe = jnp.max(worst) < HG_SAFE_EXP

    @pl.when(safe)
    def _():
        for rows, bcum, e in chunks:
            q, k, v = q_all[rows], k_all[rows], v_all[rows]
            b_last = bcum[c - 1:c, :]
            qt = (q * jnp.exp(e)).astype(BF16)
            kt = (k * jnp.exp(-e)).astype(BF16)
            qb = (q * jnp.exp(bcum)).astype(BF16)
            kh = (k * jnp.exp(b_last - bcum)).astype(BF16)
            dec = jnp.exp(b_last)
            vb = v.astype(BF16)
            def head(h, rows=rows, qt=qt, kt=kt, qb=qb, kh=kh, dec=dec, vb=vb):
                sl = slice(h * HG_D, (h + 1) * HG_D)
                a = lax.dot_general(qt[:, sl], kt[:, sl], (((1,), (1,)), ((), ())), preferred_element_type=F32)
                st = s_ref[h]
                o = lax.dot_general(qb[:, sl], st.astype(BF16), (((1,), (1,)), ((), ())),
                                    preferred_element_type=F32)
                upd = lax.dot_general(vb[:, sl], kh[:, sl], (((0,), (0,)), ((), ())), preferred_element_type=F32)
                yield
                a = jnp.where(tril, a, 0.0).astype(BF16)
                s_ref[h] = st * dec[:, sl] + upd
                yield
                oraw_ref[rows, sl] = o + jnp.dot(a, vb[:, sl], preferred_element_type=F32)
                yield

            for _ in zip(*[head(h) for h in range(HG_H)]):
                pass

    @pl.when(jnp.logical_not(safe))
    def _():
        for h in range(HG_H):
            sl = slice(h * HG_D, (h + 1) * HG_D)

            def tile(ti, carry, sl=sl, h=h):
                r = pl.multiple_of(ti * SUBLANE, SUBLANE)
                q8 = hq_ref[pl.ds(r, SUBLANE), sl]
                v8 = hi_ref[pl.ds(r, SUBLANE), sl]
                lf8, k8 = _hgrn_gates(hf_ref[pl.ds(r, SUBLANE), sl], lb[:, sl])
                f8 = jnp.exp(lf8)
                st = s_ref[h]
                rows_out = []
                for u in range(SUBLANE):
                    vu = jnp.where(row8 == 0, jnp.broadcast_to(v8[u:u + 1, :], (SUBLANE, HG_D)), 0.0)
                    ku = jnp.broadcast_to(k8[u:u + 1, :], (SUBLANE, HG_D))
                    qu = jnp.broadcast_to(q8[u:u + 1, :], (SUBLANE, HG_D))
                    st = st * f8[u:u + 1, :] + lax.dot_general(
                        vu, ku, (((0,), (0,)), ((), ())), precision=HI, preferred_element_type=F32)
                    ou = lax.dot_general(qu, st, (((1,), (1,)), ((), ())), precision=HI,
                                         preferred_element_type=F32)
                    rows_out.append(ou[0:1, :])
                s_ref[h] = st
                oraw_ref[pl.ds(r, SUBLANE), sl] = jnp.concatenate(rows_out, axis=0)
                return carry

            lax.fori_loop(0, tc // SUBLANE, tile, 0)

    o_ref[...] = _hgrn_out(oraw_ref[...], hg_ref[...], ng_ref[...]).astype(BF16)

    @pl.when(ci == pl.num_programs(1) - 1)
    def _():
        for h in range(HG_H):
            st_ref[h] = s_ref[h].T


def _hgrn_prompt(z, lb, ng, b, t, tc):
    nc = t // tc
    col = lambda c0: pl.BlockSpec((tc, HG_W), lambda bi, ci: (bi * nc + ci, c0 // HG_W))
    tri = np.tril(np.ones((HG_CHUNK, HG_CHUNK), np.float32))
    return pl.pallas_call(
        _hgrn_prompt_kernel,
        grid=(b, nc),
        in_specs=[col(C_HQ), col(C_HF), col(C_HI), col(C_HG),
                  _const_spec((1, HG_W)), _const_spec((1, HG_D)), _const_spec((HG_CHUNK, HG_CHUNK))],
        out_specs=[pl.BlockSpec((tc, HG_W), lambda bi, ci: (bi * nc + ci, 0)),
                   pl.BlockSpec((None, HG_H, HG_D, HG_D), lambda bi, ci: (bi, 0, 0, 0))],
        out_shape=[jax.ShapeDtypeStruct((b * t, HG_W), BF16),
                   jax.ShapeDtypeStruct((b, HG_H, HG_D, HG_D), F32)],
        scratch_shapes=[pltpu.VMEM((HG_H, HG_D, HG_D), F32), pltpu.VMEM((tc, HG_W), F32)],
        compiler_params=_cparams(("parallel", "arbitrary")),
        name="hgrn_prompt",
    )(z, z, z, z, lb, ng, jnp.asarray(tri))


def _merge_kernel(on_ref, oh_ref, ga_ref, gb_ref, x_ref, wn_ref, wh_ref, wo_ref, g_ref, o_ref):
    y = (_sigmoid(ga_ref[...]) * jnp.dot(on_ref[...], wn_ref[...], preferred_element_type=F32)
         + _sigmoid(gb_ref[...]) * jnp.dot(oh_ref[...], wh_ref[...], preferred_element_type=F32))
    mix = jnp.dot(y.astype(BF16), wo_ref[...], preferred_element_type=F32)
    o_ref[...] = x_ref[...] + _rms(mix, g_ref[...])


def _resident(shape):
    nd = len(shape)
    return pl.BlockSpec(shape, lambda *_: (0,) * nd, pipeline_mode=pl.Buffered(1))


def _merge(o_nsa, o_hg, z, x, wn, wh, wo, g, tm):
    n = x.shape[0]
    row = lambda w, cb: pl.BlockSpec((tm, w), lambda i: (i, cb))
    return pl.pallas_call(
        _merge_kernel,
        grid=(n // tm,),
        in_specs=[row(NSA_W, 0), row(HG_W, 0), row(D_MODEL, C_GM // D_MODEL), row(D_MODEL, C_GM // D_MODEL + 1),
                  row(D_MODEL, 0), _resident(wn.shape), _resident(wh.shape), _resident(wo.shape),
                  _resident((1, D_MODEL))],
        out_specs=row(D_MODEL, 0),
        out_shape=jax.ShapeDtypeStruct((n, D_MODEL), F32),
        compiler_params=_cparams(("parallel",)),
        name="merge_out_proj",
    )(o_nsa, o_hg, z, z, x, wn, wh, wo, g)


def _mlp_kernel(h_ref, p_ref, wu_ref, wd_ref, wg_ref, wp_ref, g1_ref, g2_ref, g3_ref, o_ref):
    h = h_ref[...]
    xn = _rms(h, g1_ref[...]).astype(BF16)
    ffn = jnp.zeros(h.shape, F32)
    step = D_MODEL // 2
    for c0 in range(0, D_FF, step):
        up = jnp.dot(xn, wu_ref[:, c0:c0 + step], preferred_element_type=F32)
        act = jnp.square(jnp.maximum(up, 0.0)).astype(BF16)
        ffn = ffn + jnp.dot(act, wd_ref[c0:c0 + step, :], preferred_element_type=F32)
    h = h + _rms(ffn, g2_ref[...])
    gate = _sigmoid(jnp.dot(_rms(h, g3_ref[...]).astype(BF16), wg_ref[...], preferred_element_type=F32))
    o_ref[...] = h + gate * jnp.dot(p_ref[...].astype(BF16), wp_ref[...], preferred_element_type=F32)


def _mlp(h, p, wu, wd, wg, wp, g1, g2, g3, tm):
    n = h.shape[0]
    row = lambda w: pl.BlockSpec((tm, w), lambda i: (i, 0))
    gain = _resident((1, D_MODEL))
    return pl.pallas_call(
        _mlp_kernel,
        grid=(n // tm,),
        in_specs=[row(D_MODEL), row(PLE_DIM), _resident(wu.shape), _resident(wd.shape), _resident(wg.shape),
                  _resident(wp.shape), gain, gain, gain],
        out_specs=row(D_MODEL),
        out_shape=jax.ShapeDtypeStruct((n, D_MODEL), F32),
        compiler_params=_cparams(("parallel",)),
        name="mlp_ple",
    )(h, p, wu, wd, wg, wp, g1, g2, g3)


def _prep_w_in(w):
    sizes = (NSA_W, 2 * KV_W, 2 * KV_W, 2 * KV_W, 3 * NSA_HEADS, HG_W, HG_W, HG_W, HG_W)
    q, kvc, kvs, kvw, gn, hq, hf, hi, hg, gm = jnp.split(w, [int(v) for v in np.cumsum(sizes)], axis=1)
    pad = jnp.zeros((w.shape[0], Z_COLS - C_GN - 3 * NSA_HEADS), w.dtype)
    return jnp.concatenate([gm, q, hq, hf, hi, hg, kvc, kvs, kvw, gn, pad], axis=1).astype(BF16)


def _prep_compress(w1k, w2k, w1v, w2v, pos):
    eye = jnp.eye(KVH, dtype=F32)

    def big1(w1):
        t = jnp.einsum('rsdh,kq->rskdqh', w1, eye)
        return t.reshape(CMP_BLOCK // CMP_STRIDE, CMP_STRIDE * KV_W, KVH * CMP_HIDDEN)

    def big2(w2):
        return jnp.einsum('hd,kq->khqd', w2, eye).reshape(KVH * CMP_HIDDEN, KV_W)

    w1 = jnp.stack([big1(w1k), big1(w1v)]).astype(BF16)
    w2 = jnp.stack([big2(w2k), big2(w2v)]).astype(BF16)
    posb = jnp.broadcast_to(pos[:, :, None, :], pos.shape[:2] + (KVH, HD)).reshape(pos.shape[0], 1, -1)
    posb = jnp.broadcast_to(posb, (pos.shape[0], SUBLANE, posb.shape[-1])).astype(BF16)
    return w1, w2, posb


def _kv_rows_t_kernel(ck_ref, cv_ref, sk_ref, sv_ref, wk_ref, wv_ref, ct_ref, st_ref, wt_ref):
    rows = ck_ref.shape[0]
    for src, dst, part in ((ck_ref, ct_ref, 0), (cv_ref, ct_ref, 1), (sk_ref, st_ref, 0), (sv_ref, st_ref, 1),
                           (wk_ref, wt_ref, 0), (wv_ref, wt_ref, 1)):
        for j in range(rows // LANE):
            dst[part * KV_W:(part + 1) * KV_W, j * LANE:(j + 1) * LANE] = src[j * LANE:(j + 1) * LANE, :].T


def _kv_rows_t(z, b, t, rows):
    nt = t // rows
    col = lambda c0: pl.BlockSpec((rows, KV_W), lambda bi, j: (bi * nt + j, c0 // KV_W))
    full = pl.BlockSpec((None, 2 * KV_W, rows), lambda bi, j: (bi, 0, j))
    tail = pl.BlockSpec((None, 2 * KV_W, rows), lambda bi, j: (bi, 0, 0))
    return pl.pallas_call(
        _kv_rows_t_kernel,
        grid=(b, nt),
        in_specs=[col(C_KVC), col(C_KVC + KV_W), col(C_KVS), col(C_KVS + KV_W), col(C_KVW), col(C_KVW + KV_W)],
        out_specs=[full, full, tail],
        out_shape=[jax.ShapeDtypeStruct((b, 2 * KV_W, t), F32), jax.ShapeDtypeStruct((b, 2 * KV_W, t), F32),
                   jax.ShapeDtypeStruct((b, 2 * KV_W, rows), F32)],
        compiler_params=_cparams(("parallel", "arbitrary")),
        name="kv_rows_t",
    )(z, z, z, z, z, z)


def _prep_layer(i, lb_all, w_in, cmp_k_w1, cmp_k_w2, cmp_v_w1, cmp_v_w2, cmp_pos, hg_norm, w_branch_nsa,
                w_branch_hgrn, w_out, norm_pre_mix, norm_post_mix, norm_pre_mlp, norm_post_mlp, w_mlp_up,
                w_mlp_down, norm_ple, w_ple_gate, w_ple_proj):
    w1, w2, posb = _prep_compress(cmp_k_w1[i], cmp_k_w2[i], cmp_v_w1[i], cmp_v_w2[i], cmp_pos[i])
    row = lambda a: a[i].reshape(1, -1).astype(F32)
    return {
        'w_in': _prep_w_in(w_in[i]), 'cmp_w1': w1, 'cmp_w2': w2, 'cmp_pos': posb,
        'hg_lb': lb_all[i].reshape(1, HG_W), 'hg_norm': row(hg_norm),
        'w_bn': w_branch_nsa[i].astype(BF16), 'w_bh': w_branch_hgrn[i].astype(BF16), 'w_out': w_out[i].astype(BF16),
        'w_up': w_mlp_up[i].astype(BF16), 'w_down': w_mlp_down[i].astype(BF16),
        'w_gate': w_ple_gate[i].astype(BF16), 'w_proj': w_ple_proj[i].astype(BF16),
        'g_pre_mix': row(norm_pre_mix), 'g_post_mix': row(norm_post_mix), 'g_pre_mlp': row(norm_pre_mlp),
        'g_post_mlp': row(norm_post_mlp), 'g_ple': row(norm_ple),
    }


def _layer_prompt(x, p, lw, b, t):
    n = b * t
    z = _norm_matmul(x, lw['g_pre_mix'], lw['w_in'], tm=min(1024, n), tn=Z_COLS // 4)
    wb = min(WINDOW, t)
    kv_c, kv_s, kv_w = _kv_rows_t(z, b, t, wb)
    kc, vc = _compress_prompt(z, lw['cmp_w1'], lw['cmp_w2'], lw['cmp_pos'], b, t)
    vct = jnp.swapaxes(vc, 1, 2)
    o_nsa = _nsa_prompt256(z, kc, vct, b, t)
    o_hg, st = _hgrn_prompt(z, lw['hg_lb'], lw['hg_norm'], b, t, tc=min(256, t))
    h1 = _merge(o_nsa, o_hg, z, x, lw['w_bn'], lw['w_bh'], lw['w_out'], lw['g_post_mix'], tm=min(512, n))
    h2 = _mlp(h1, p, lw['w_up'], lw['w_down'], lw['w_gate'], lw['w_proj'],
              lw['g_pre_mlp'], lw['g_post_mlp'], lw['g_ple'], tm=min(1024, n))
    kv6 = lambda a: a.reshape(b, 2, KVH, HD, a.shape[-1]).transpose(0, 4, 1, 2, 3)
    return h2, kv6(kv_c), kv6(kv_s), kv6(kv_w), st


SEG_PITCH = 24


def _page_fetch(pt_ref, cache_ref, buf_ref, sem, nseq=1):
    b = pl.program_id(0)
    n_pages = pt_ref.shape[1]
    slot = b % 2

    def copy(step, sl, p):
        seq = step * nseq + p // n_pages
        return pltpu.make_async_copy(cache_ref.at[pt_ref[seq, p % n_pages]], buf_ref.at[sl, p], sem.at[sl])

    def start(step, sl):
        lax.fori_loop(0, nseq * n_pages, lambda p, c: (copy(step, sl, p).start(), c)[1], 0)

    @pl.when(b == 0)
    def _():
        start(0, 0)

    @pl.when(b + 1 < pl.num_programs(0))
    def _():
        start(b + 1, 1 - slot)

    def wait():
        lax.fori_loop(0, nseq * n_pages, lambda p, c: (copy(b, slot, p).wait(), c)[1], 0)

    return slot, wait


def _compress_sample_kernel(pt_ref, cache_ref, w1_ref, w2_ref, pos_ref, kc_ref, vc_ref, buf_ref, rows_ref, sem):
    n_pages, page = pt_ref.shape[1], buf_ref.shape[-1]
    n_seg = n_pages * page // CMP_STRIDE
    slot, wait = _page_fetch(pt_ref, cache_ref, buf_ref, sem)
    wait()
    segs_per_page = page // CMP_STRIDE
    for kv, out_ref in ((0, kc_ref), (1, vc_ref)):
        for p in range(n_pages):
            rows = buf_ref[slot, p, kv].T
            for g in range(segs_per_page):
                r0 = (p * segs_per_page + g) * SEG_PITCH
                rows_ref[kv, r0:r0 + CMP_STRIDE, :] = rows[g * CMP_STRIDE:(g + 1) * CMP_STRIDE, :]
        load = lambda s, kv=kv: rows_ref[kv, pl.ds(s, n_seg, stride=SEG_PITCH), :]
        out_ref[...] = _compress_math(load, n_seg, w1_ref, w2_ref, pos_ref, kv).astype(BF16)


def _compress_sample(page_table, cache_t, w1, w2, pos):
    bs, n_pages = page_table.shape
    page = cache_t.shape[-1]
    n_seg = n_pages * page // CMP_STRIDE
    const = lambda shape: pl.BlockSpec(shape, lambda i, pt: (0,) * len(shape), pipeline_mode=pl.Buffered(1))
    out = pl.BlockSpec((None, n_seg, KV_W), lambda i, pt: (i, 0, 0))
    return pl.pallas_call(
        _compress_sample_kernel,
        grid_spec=pltpu.PrefetchScalarGridSpec(
            num_scalar_prefetch=1, grid=(bs,),
            in_specs=[pl.BlockSpec(memory_space=pl.ANY), const(w1.shape), const(w2.shape), const(pos.shape)],
            out_specs=[out, out],
            scratch_shapes=[pltpu.VMEM((2, n_pages, 2, KV_W, page), F32),
                            pltpu.VMEM((2, n_seg * SEG_PITCH, KV_W), F32), pltpu.SemaphoreType.DMA((2,))]),
        out_shape=[jax.ShapeDtypeStruct((bs, n_seg, KV_W), BF16)] * 2,
        compiler_params=_cparams(("arbitrary",)),
        name="compress_sample",
    )(page_table, cache_t, w1, w2, pos)


_NT = (((1,), (1,)), ((), ()))


def _nsa_sample_kernel(pt_ref, qbd_ref, gl_ref, kc_ref, vc_ref, snew_ref, cwin_ref, wnew_ref, csel_ref,
                       slope_ref, tq_ref, tq8_ref, cover_ref, gsum_ref, gexp_ref, spread_ref, o_ref,
                       buf_ref, s_ref, sem):
    nseq = qbd_ref.shape[0]
    slot, wait = _page_fetch(pt_ref, csel_ref, buf_ref, sem, nseq)
    wait()
    chains = [_nsa_sample_one(pt_ref.shape[1], u, slot, qbd_ref.at[u], gl_ref.at[u], kc_ref.at[u], vc_ref.at[u],
                              snew_ref.at[u], cwin_ref.at[u], wnew_ref.at[u], slope_ref, tq_ref, tq8_ref,
                              cover_ref, gsum_ref, gexp_ref, spread_ref, o_ref.at[u], buf_ref, s_ref.at[u])
              for u in range(nseq)]
    for _ in zip(*chains):
        pass


def _nsa_sample_one(n_pages, u, slot, qbd_ref, gl_ref, kc_ref, vc_ref, snew_ref, cwin_ref, wnew_ref,
                    slope_ref, tq_ref, tq8_ref, cover_ref, gsum_ref, gexp_ref, spread_ref, o_ref, buf_ref, s_ref):
    page = buf_ref.shape[-1]
    past = n_pages * page
    n_seg = kc_ref.shape[0]
    npad = cover_ref.shape[1]
    n_sel = past // SEL_BLOCK + 1
    nr = qbd_ref.shape[0]
    wb = cwin_ref.shape[-1]

    qb = (qbd_ref[...] * SCALE).astype(BF16)
    slope = slope_ref[...]
    qpos = tq_ref[...] + float(past)
    zeros_pad = jnp.zeros((QB - snew_ref.shape[0], KV_W), F32)
    t_new = lax.broadcasted_iota(jnp.int32, (1, QB), 1).astype(F32)

    sc = lax.dot_general(qb, kc_ref[...], _NT, preferred_element_type=F32)
    n_i = lax.broadcasted_iota(jnp.int32, (1, n_seg), 1)
    dcmp = qpos - (n_i * CMP_STRIDE + (CMP_BLOCK - 1)).astype(F32)
    okc = dcmp >= 0.0
    sc = jnp.where(okc, sc - slope * dcmp, NEG)
    yield
    mc = jnp.max(sc, axis=-1, keepdims=True)
    pc = jnp.where(okc, jnp.exp(sc - mc), 0.0)
    lc = jnp.sum(pc, axis=-1, keepdims=True)
    pc = pc * (1.0 / jnp.where(lc > 0.0, lc, 1.0))
    yield
    o_cmp = jnp.dot(pc.astype(BF16), vc_ref[...], preferred_element_type=F32)

    psum = jnp.dot(gsum_ref[...], pc, precision=HI, preferred_element_type=F32)
    imp = jnp.dot(psum, cover_ref[...], precision=HI, preferred_element_type=F32)
    yield
    nq8 = gsum_ref.shape[0]
    jblk = lax.broadcasted_iota(jnp.int32, (nq8, npad), 1)
    qblk = (tq8_ref[...].astype(jnp.int32) + past) // SEL_BLOCK
    back = qblk - jblk
    visible = back >= 0
    forced = (jblk == 0) | (visible & (back < SEL_LOCAL))
    score = jnp.where(forced, 1e9, jnp.where(visible, imp, -1e9))
    rank = jnp.zeros((nq8, npad), F32)
    for jp in range(n_sel):
        col = score[:, jp:jp + 1]
        ge = jnp.where(col >= score, 1.0, 0.0)
        gt = jnp.where(col > score, 1.0, 0.0)
        rank = rank + jnp.where(jblk > jp, ge, gt)
        if jp % 16 == 15:
            yield
    sel8 = jnp.where(visible, jnp.where(rank < float(min(SEL_TOP, n_sel)), 1.0, 0.0), 0.0)
    negb = (jnp.dot(gexp_ref[...], sel8, precision=HI, preferred_element_type=F32) - 1.0) * (-NEG)
    yield

    def tile_bias(jt):
        lane = lax.broadcasted_iota(jnp.int32, (nr, QB), 1)
        return jnp.where(lane < SEL_BLOCK, negb[:, 2 * jt:2 * jt + 1], negb[:, 2 * jt + 1:2 * jt + 2])

    ppc = 4
    ck = ppc * page
    bpc = ck // SEL_BLOCK

    def chunk_t(c, kv):
        return jnp.concatenate([buf_ref[slot, u * n_pages + c * ppc + w, kv] for w in range(ppc)],
                               axis=1).astype(BF16)

    slope_ck = jnp.concatenate([jnp.broadcast_to(slope, (nr, LANE))] * (ck // LANE), axis=1)
    qpos_ck = jnp.concatenate([jnp.broadcast_to(qpos, (nr, LANE))] * (ck // LANE), axis=1)
    negb_bf = negb.astype(BF16)
    for c in range(n_pages // ppc):
        s = jnp.dot(qb, chunk_t(c, 0), preferred_element_type=F32)
        kpos = (lax.broadcasted_iota(jnp.int32, (1, ck), 1) + c * ck).astype(F32)
        b0 = c * bpc
        bias = jnp.dot(negb_bf[:, (b0 // LANE) * LANE:(b0 // LANE + 1) * LANE], spread_ref[(b0 % LANE) // bpc],
                       preferred_element_type=F32)
        s_ref[:, c * ck:(c + 1) * ck] = s - slope_ck * (qpos_ck - kpos) + bias
        if c % 4 == 3:
            yield
    knew =jnp.concatenate([snew_ref[:, 0:KV_W], zeros_pad], axis=0).astype(BF16)
    vnew = jnp.concatenate([snew_ref[:, KV_W:2 * KV_W], zeros_pad], axis=0).astype(BF16)
    dnew = tq_ref[...] - t_new
    s = lax.dot_general(qb, knew, _NT, preferred_element_type=F32)
    s_ref[:, past:past + QB] = jnp.where(dnew >= 0.0, s - slope * dnew + tile_bias(past // QB), NEG)
    s_all = s_ref[...]
    ms = jnp.max(s_all, axis=-1, keepdims=True)
    ps = jnp.exp(s_all - ms)
    ls = jnp.sum(ps, axis=-1, keepdims=True)
    yield
    psb = ps.astype(BF16)
    o_sel = jnp.dot(psb[:, past:past + QB], vnew, preferred_element_type=F32)
    for c in range(n_pages // ppc):
        o_sel = o_sel + lax.dot_general(psb[:, c * ck:(c + 1) * ck], chunk_t(c, 1), _NT,
                                        preferred_element_type=F32)
        if c % 4 == 3:
            yield
    o_sel = o_sel * (1.0 / ls)

    s1 = jnp.dot(qb, cwin_ref[0].astype(BF16), preferred_element_type=F32)
    d1 = float(wb) + tq_ref[...] - lax.broadcasted_iota(jnp.int32, (1, wb), 1).astype(F32)
    ok1 = d1 < float(WINDOW)
    s1 = jnp.where(ok1, s1 - slope * d1, NEG)
    yield
    wk =jnp.concatenate([wnew_ref[:, 0:KV_W], zeros_pad], axis=0).astype(BF16)
    wv = jnp.concatenate([wnew_ref[:, KV_W:2 * KV_W], zeros_pad], axis=0).astype(BF16)
    ok2 = dnew >= 0.0
    s2 = jnp.where(ok2, lax.dot_general(qb, wk, _NT, preferred_element_type=F32) - slope * dnew, NEG)
    mw = jnp.maximum(jnp.max(s1, axis=-1, keepdims=True), jnp.max(s2, axis=-1, keepdims=True))
    p1 = jnp.where(ok1, jnp.exp(s1 - mw), 0.0)
    p2 = jnp.where(ok2, jnp.exp(s2 - mw), 0.0)
    lw_ = jnp.sum(p1, axis=-1, keepdims=True) + jnp.sum(p2, axis=-1, keepdims=True)
    yield
    o_win =(lax.dot_general(p1.astype(BF16), cwin_ref[1].astype(BF16), _NT, preferred_element_type=F32)
             + jnp.dot(p2.astype(BF16), wv, preferred_element_type=F32)) * (1.0 / lw_)

    sig = _sigmoid(gl_ref[...])
    o_ref[...] = sig[:, 0:1] * o_cmp + sig[:, 1:2] * o_sel + sig[:, 2:3] * o_win
    yield


def _nsa_sample(page_table, qbd, gl, kc, vc, snew, cwin, wnew, csel, ts):
    bs, n_pages = page_table.shape
    page = csel.shape[-1]
    past = n_pages * page
    n_seg = kc.shape[1]
    n_sel = past // SEL_BLOCK + 1
    npad = -(-(n_sel + 1) // LANE) * LANE
    nr = KVH * GROUP * ts
    wb = cwin.shape[-1]
    r = np.arange(nr)
    slope = np.exp2(-8.0 * ((r // ts) + 1) / NSA_HEADS).astype(np.float32)[:, None]
    tq = (r % ts).astype(np.float32)[:, None]
    r8 = np.arange(KVH * ts)
    tq8 = (r8 % ts).astype(np.float32)[:, None]
    cover = np.zeros((n_seg, npad), np.float32)
    cover[:, :n_sel] = _cover_matrix(n_seg, n_sel)
    gsum = ((r[None, :] // (GROUP * ts) == r8[:, None] // ts) & (r[None, :] % ts == r8[:, None] % ts)).astype(np.float32)
    ck = 4 * page
    bpc = ck // SEL_BLOCK
    j_i = np.arange(LANE)[None, :, None]
    spread = (j_i == (np.arange(LANE // bpc)[:, None, None] * bpc + np.arange(ck)[None, None, :] // SEL_BLOCK))
    consts = [slope, tq, tq8, cover, gsum, gsum.T.copy(), jnp.asarray(spread, BF16)]
    const = lambda shape: pl.BlockSpec(shape, lambda i, pt: (0,) * len(shape), pipeline_mode=pl.Buffered(1))
    nseq = 2 if bs % 2 == 0 else 1
    per = lambda *s: pl.BlockSpec((nseq,) + s, lambda i, pt: (i,) + (0,) * len(s))
    return pl.pallas_call(
        _nsa_sample_kernel,
        grid_spec=pltpu.PrefetchScalarGridSpec(
            num_scalar_prefetch=1, grid=(bs // nseq,),
            in_specs=[per(nr, KV_W), per(nr, LANE), per(n_seg, KV_W), per(n_seg, KV_W), per(SUBLANE, 2 * KV_W),
                      per(2, KV_W, wb), per(SUBLANE, 2 * KV_W), pl.BlockSpec(memory_space=pl.ANY)]
                     + [const(c.shape) for c in consts],
            out_specs=per(nr, KV_W),
            scratch_shapes=[pltpu.VMEM((2, nseq * n_pages, 2, KV_W, page), F32),
                            pltpu.VMEM((nseq, nr, past + QB), F32), pltpu.SemaphoreType.DMA((2,))]),
        out_shape=jax.ShapeDtypeStruct((bs, nr, KV_W), F32),
        compiler_params=_cparams(("arbitrary",)),
        name="nsa_sample",
    )(page_table, qbd, gl, kc, vc, snew, cwin, wnew, csel, *[jnp.asarray(c) for c in consts])


def _hgrn_sample_kernel(ts, hq_ref, hf_ref, hi_ref, hg_ref, lb_ref, ng_ref, s0_ref, o_ref, s1_ref):
    rows = hq_ref.shape[0]
    q = hq_ref[...]
    v = hi_ref[...]
    log_f, k = _hgrn_gates(hf_ref[...], lb_ref[...])
    tloc = lax.broadcasted_iota(jnp.int32, (rows, HG_W), 0) % ts
    up = lambda a, d: pltpu.roll(a, d, axis=0)
    down = lambda a, d: pltpu.roll(a, rows - d, axis=0)

    bcum = log_f
    for d in range(1, ts):
        bcum = bcum + jnp.where(tloc >= d, up(log_f, d), 0.0)
    b_last = bcum
    for d in range(1, ts):
        b_last = jnp.where(tloc == ts - 1 - d, down(bcum, d), b_last)

    o_intra = [jnp.zeros((rows, HG_D), F32) for _ in range(HG_H)]
    for d in range(ts):
        kd, bd, vd = (k, bcum, v) if d == 0 else (up(k, d), up(bcum, d), up(v, d))
        w = jnp.where(tloc >= d, q * kd * jnp.exp(jnp.where(tloc >= d, bcum - bd, 0.0)), 0.0)
        for h in range(HG_H):
            sl = slice(h * HG_D, (h + 1) * HG_D)
            o_intra[h] = o_intra[h] + jnp.sum(w[:, sl], axis=-1, keepdims=True) * vd[:, sl]

    qb = q * jnp.exp(bcum)
    kh = k * jnp.exp(b_last - bcum)
    per_tile = SUBLANE // ts
    row8 = lax.broadcasted_iota(jnp.int32, (SUBLANE, HG_D), 0) // ts
    tiles = []
    for j in range(rows // SUBLANE):
        r8 = slice(j * SUBLANE, (j + 1) * SUBLANE)
        heads = []
        for h in range(HG_H):
            sl = slice(h * HG_D, (h + 1) * HG_D)
            o_inter = jnp.zeros((SUBLANE, HG_D), F32)
            for u in range(per_tile):
                seq = j * per_tile + u
                mine = row8 == u
                s0 = s0_ref[seq, h]
                o_inter = o_inter + jnp.dot(jnp.where(mine, qb[r8, sl], 0.0).astype(BF16), s0.astype(BF16),
                                            preferred_element_type=F32)
                upd = lax.dot_general(jnp.where(mine, kh[r8, sl], 0.0).astype(BF16), v[r8, sl].astype(BF16),
                                      (((0,), (0,)), ((), ())), preferred_element_type=F32)
                r_last = j * SUBLANE + u * ts + ts - 1
                dec = jnp.exp(bcum[r_last:r_last + 1, sl])
                s1_ref[seq, h] = jnp.broadcast_to(dec, (HG_D, HG_D)).T * s0 + upd
            heads.append(o_inter + o_intra[h][r8, :])
        tiles.append(jnp.concatenate(heads, axis=1))
    o = jnp.concatenate(tiles, axis=0)
    o_ref[...] = _hgrn_out(o, hg_ref[...], ng_ref[...]).astype(BF16)


def _hgrn_sample(z, lb, ng, s0, bs, ts, nb):
    rows = nb * ts
    col = lambda c0: pl.BlockSpec((rows, HG_W), lambda i: (i, c0 // HG_W))
    st = pl.BlockSpec((nb, HG_H, HG_D, HG_D), lambda i: (i, 0, 0, 0))
    return pl.pallas_call(
        functools.partial(_hgrn_sample_kernel, ts),
        grid=(bs // nb,),
        in_specs=[col(C_HQ), col(C_HF), col(C_HI), col(C_HG), _const_spec((1, HG_W)), _const_spec((1, HG_D)), st],
        out_specs=[pl.BlockSpec((rows, HG_W), lambda i: (i, 0)), st],
        out_shape=[jax.ShapeDtypeStruct((bs * ts, HG_W), BF16),
                   jax.ShapeDtypeStruct((bs, HG_H, HG_D, HG_D), F32)],
        compiler_params=_cparams(("parallel",)),
        name="hgrn_sample",
    )(z, z, z, z, lb, ng, s0)


def _layer_sample(x, p, cache_cmp, cache_sel, cache_win, state, page_table, lw, bs, ts):
    n = bs * ts
    assert SUBLANE % ts == 0 and n % SUBLANE == 0
    z = _norm_matmul(x, lw['g_pre_mix'], lw['w_in'], tm=min(512, n), tn=512)
    kv_c = z[:, C_KVC:C_KVC + 2 * KV_W]
    kv_s = z[:, C_KVS:C_KVS + 2 * KV_W]
    kv_w = z[:, C_KVW:C_KVW + 2 * KV_W]

    rows_last = lambda a: jnp.transpose(a, (0, 2, 3, 4, 1)).reshape(a.shape[0], 2, KV_W, a.shape[1])
    kc, vc = _compress_sample(page_table, rows_last(cache_cmp), lw['cmp_w1'], lw['cmp_w2'], lw['cmp_pos'])

    eye = jnp.eye(KVH, dtype=F32)
    q5 = z[:, C_Q:C_Q + NSA_W].reshape(bs, ts, KVH, GROUP, HD).transpose(0, 2, 3, 1, 4)
    qbd = jnp.einsum('bkgtd,kq->bkgtqd', q5, eye).reshape(bs, KVH * GROUP * ts, KV_W)
    g5 = z[:, C_GN:C_GN + 3 * NSA_HEADS].reshape(bs, ts, KVH, GROUP, 3).transpose(0, 2, 3, 1, 4)
    gl = jnp.pad(g5.reshape(bs, KVH * GROUP * ts, 3), ((0, 0), (0, 0), (0, LANE - 3)))
    pad_rows = lambda a: jnp.pad(a.reshape(bs, ts, 2 * KV_W), ((0, 0), (0, SUBLANE - ts), (0, 0)))
    o_rows = _nsa_sample(page_table, qbd, gl, kc, vc, pad_rows(kv_s), rows_last(cache_win), pad_rows(kv_w),
                         rows_last(cache_sel), ts)
    o6 = o_rows.reshape(bs, KVH, GROUP, ts, KVH, HD)
    o_nsa = jnp.stack([o6[:, kvh, :, :, kvh, :] for kvh in range(KVH)], axis=1)
    o_nsa = o_nsa.transpose(0, 3, 1, 2, 4).reshape(n, NSA_W).astype(BF16)

    o_hg, st = _hgrn_sample(z, lw['hg_lb'], lw['hg_norm'], state, bs, ts, nb=min(8, bs))
    h1 = _merge(o_nsa, o_hg, z, x, lw['w_bn'], lw['w_bh'], lw['w_out'], lw['g_post_mix'], tm=min(512, n))
    h2 = _mlp(h1, p, lw['w_up'], lw['w_down'], lw['w_gate'], lw['w_proj'],
              lw['g_pre_mlp'], lw['g_post_mlp'], lw['g_ple'], tm=min(512, n))
    kv6 = lambda a: a.reshape(bs, ts, 2, KVH, HD)
    win_buf = jnp.concatenate([cache_win, kv6(kv_w)], axis=1)[:, ts:]
    return h2, kv6(kv_c), kv6(kv_s), win_buf, st


def kernel(x_prompt, x_sample, cache_cmp_kv, cache_sel_kv, cache_win_kv, state_hgrn, page_table, p_prompt,
           p_sample, w_in, cmp_k_w1, cmp_k_w2, cmp_v_w1, cmp_v_w2, cmp_pos, hg_lb_logits, hg_norm, w_branch_nsa,
           w_branch_hgrn, w_out, norm_pre_mix, norm_post_mix, norm_pre_mlp, norm_post_mlp, w_mlp_up, w_mlp_down,
           norm_ple, w_ple_gate, w_ple_proj):
    depth = w_in.shape[0]
    b, t, d = x_prompt.shape
    bs, ts, _ = x_sample.shape
    lb_all = jnp.cumsum(jax.nn.softmax(hg_lb_logits.astype(F32), axis=0), axis=0)
    h_p = x_prompt.reshape(b * t, d)
    h_s = x_sample.reshape(bs * ts, d)
    outs = [[] for _ in range(8)]
    for i in range(depth):
        lw = _prep_layer(i, lb_all, w_in, cmp_k_w1, cmp_k_w2, cmp_v_w1, cmp_v_w2, cmp_pos, hg_norm, w_branch_nsa,
                         w_branch_hgrn, w_out, norm_pre_mix, norm_post_mix, norm_pre_mlp, norm_post_mlp, w_mlp_up,
                         w_mlp_down, norm_ple, w_ple_gate, w_ple_proj)
        h_p, *res_p = _layer_prompt(h_p, p_prompt[i].reshape(b * t, -1), lw, b, t)
        h_s, *res_s = _layer_sample(h_s, p_sample[i].reshape(bs * ts, -1), cache_cmp_kv[i], cache_sel_kv[i],
                                    cache_win_kv[i], state_hgrn[i], page_table, lw, bs, ts)
        for lst, v in zip(outs, res_p + res_s):
            lst.append(v)
    return (h_p.reshape(b, t, d), h_s.reshape(bs, ts, d)) + tuple(jnp.stack(lst, axis=0) for lst in outs)
```

```python
import functools
import itertools
import math

import numpy as np
import jax
import jax.numpy as jnp
from jax import lax
from jax.experimental import pallas as pl
from jax.experimental.pallas import tpu as pltpu

F32 = jnp.float32
BF16 = jnp.bfloat16

D_MODEL = 1024
NSA_HEADS = 8
KVH = 2
GROUP = NSA_HEADS // KVH
HD = 64
NSA_W = NSA_HEADS * HD
KV_W = KVH * HD
CMP_BLOCK = 32
CMP_STRIDE = 16
CMP_HIDDEN = 2 * HD
SEL_BLOCK = 64
SEL_TOP = 16
SEL_LOCAL = 2
WINDOW = 512
HG_W = 512
HG_H = 4
HG_D = 128
HG_CHUNK = 64
D_FF = 4 * D_MODEL
PLE_DIM = 256
RMS_EPS = 1e-6
NEG = -1e30
M_INIT = -1e20
SCALE = HD ** -0.5
HG_SAFE_EXP = 60.0

LANE = 128
SUBLANE = 8
VMEM_LIMIT = 48 * 1024 * 1024

C_GM = 0
C_Q = 2048
C_HQ = 2560
C_HF = 3072
C_HI = 3584
C_HG = 4096
C_KVC = 4608
C_KVS = 4864
C_KVW = 5120
C_GN = 5376
Z_COLS = 5632
QB = 128
HI = lax.Precision.HIGHEST


def _cparams(sem, vmem=VMEM_LIMIT):
    return pltpu.CompilerParams(dimension_semantics=sem, vmem_limit_bytes=vmem)


def _rms(x, g):
    return x * lax.rsqrt(jnp.mean(x * x, axis=-1, keepdims=True) + RMS_EPS) * g


def _sigmoid(x):
    return 1.0 / (1.0 + jnp.exp(-x))


def _gelu_tanh(x):
    return 0.5 * x * (1.0 + jnp.tanh(math.sqrt(2.0 / math.pi) * (x + 0.044715 * (x * x * x))))


def _const_spec(shape):
    nd = len(shape)
    return pl.BlockSpec(shape, lambda *_: (0,) * nd)


def _norm_matmul_kernel(x_ref, g_ref, w_ref, o_ref, xn_ref):
    @pl.when(pl.program_id(1) == 0)
    def _():
        xn_ref[...] = _rms(x_ref[...], g_ref[...]).astype(BF16)

    tn = o_ref.shape[1]
    col = pl.multiple_of(pl.program_id(1) * tn, LANE)
    o_ref[...] = jnp.dot(xn_ref[...], w_ref[:, pl.ds(col, tn)], preferred_element_type=F32)


def _norm_matmul(x, g, w, tm, tn):
    n, d = x.shape
    c = w.shape[1]
    return pl.pallas_call(
        _norm_matmul_kernel,
        grid=(n // tm, c // tn),
        in_specs=[pl.BlockSpec((tm, d), lambda i, j: (i, 0)),
                  pl.BlockSpec((1, d), lambda i, j: (0, 0)),
                  pl.BlockSpec((d, c), lambda i, j: (0, 0), pipeline_mode=pl.Buffered(1))],
        out_specs=pl.BlockSpec((tm, tn), lambda i, j: (i, j)),
        out_shape=jax.ShapeDtypeStruct((n, c), F32),
        scratch_shapes=[pltpu.VMEM((tm, d), BF16)],
        compiler_params=_cparams(("parallel", "arbitrary")),
        name="norm_in_proj",
    )(x, g, w)


def _compress_math(load_rows, n_seg, w1_ref, w2_ref, pos_ref, kv):
    x = jnp.concatenate([load_rows(s) for s in range(CMP_STRIDE)], axis=1).astype(BF16)
    h0 = jnp.dot(x, w1_ref[kv, 0], preferred_element_type=F32)
    h1 = jnp.dot(x, w1_ref[kv, 1], preferred_element_type=F32)
    posb = (jnp.dot(pos_ref[0], w1_ref[kv, 0], preferred_element_type=F32)
            + jnp.dot(pos_ref[1], w1_ref[kv, 1], preferred_element_type=F32))
    hid = h0 + pltpu.roll(h1, n_seg - 1, axis=0) + posb[0:1]
    return jnp.dot(_gelu_tanh(hid).astype(BF16), w2_ref[kv], preferred_element_type=F32)


def _compress_prompt_kernel(xk_ref, xv_ref, w1_ref, w2_ref, pos_ref, kc_ref, vc_ref):
    n_seg = xk_ref.shape[0] // CMP_STRIDE
    for kv, x_ref, out_ref in ((0, xk_ref, kc_ref), (1, xv_ref, vc_ref)):
        load = lambda s, x_ref=x_ref: x_ref[pl.ds(s, n_seg, stride=CMP_STRIDE), :]
        out_ref[...] = _compress_math(load, n_seg, w1_ref, w2_ref, pos_ref, kv).astype(BF16)


def _compress_prompt(z, w1, w2, pos, b, t):
    n_seg = t // CMP_STRIDE
    return pl.pallas_call(
        _compress_prompt_kernel,
        grid=(b,),
        in_specs=[pl.BlockSpec((t, KV_W), lambda i: (i, C_KVC // KV_W)),
                  pl.BlockSpec((t, KV_W), lambda i: (i, C_KVC // KV_W + 1)),
                  _const_spec(w1.shape), _const_spec(w2.shape), _const_spec(pos.shape)],
        out_specs=[pl.BlockSpec((None, n_seg, KV_W), lambda i: (i, 0, 0)),
                   pl.BlockSpec((None, n_seg, KV_W), lambda i: (i, 0, 0))],
        out_shape=[jax.ShapeDtypeStruct((b, n_seg, KV_W), BF16),
                   jax.ShapeDtypeStruct((b, n_seg, KV_W), BF16)],
        compiler_params=_cparams(("parallel",)),
        name="compress_prompt",
    )(z, z, w1, w2, pos)


PB = 256
N_AUG = 16
VROWS = HD + 16
LOG2E = math.log2(math.e)


def _split3(x):
    hi = x.astype(BF16).astype(F32)
    r = x - hi
    mid = r.astype(BF16).astype(F32)
    return hi, mid, r - mid


def _nsa_prompt256_kernel(q_ref, gn_ref, kc_ref, vct_ref, zks_ref, zvs_ref, zkw_ref, zvw_ref,
                          augc_ref, slopes_ref, covert_ref, o_ref,
                          qa_ref, m_ref, acc_ref, sel_ref, score_ref, ks_ref, vst_ref, kw_ref, vwt_ref,
                          sa_ref, sb_ref, sc_ref, sd_ref, mw_ref, accw_ref, ocmp_ref, imp_ref):
    i = pl.program_id(1)
    t0 = i * PB

    @pl.when(i == 0)
    def _():
        ones_rows = jnp.where(lax.broadcasted_iota(jnp.int32, (VROWS - HD, PB), 0) == 0, 1.0, 0.0)
        for zk, zv, k_dst, vt_dst in ((zks_ref, zvs_ref, ks_ref, vst_ref), (zkw_ref, zvw_ref, kw_ref, vwt_ref)):
            for j in range(k_dst.shape[0]):
                k_dst[j] = zk[j * PB:(j + 1) * PB, :].astype(BF16)
                vt = zv[j * PB:(j + 1) * PB, :].T
                vt_dst[j] = jnp.concatenate(
                    [piece for kvh in range(KVH) for piece in (vt[kvh * HD:(kvh + 1) * HD, :], ones_rows)],
                    axis=0).astype(BF16)
    n_seg = kc_ref.shape[0]
    n_sel = covert_ref.shape[0]
    nl = NSA_HEADS * PB
    half = GROUP * PB
    slopes2 = slopes_ref[...] * LOG2E
    tlane = (lax.broadcasted_iota(jnp.int32, (1, nl), 1) % PB).astype(F32)
    blocks_per_tile = PB // SEL_BLOCK

    q = q_ref[...] * (SCALE * LOG2E)
    zero = jnp.zeros((HD, PB), F32)
    cols = []
    for kvh in range(KVH):
        for p in range(GROUP // 2):
            c0 = (kvh * (GROUP // 2) + p) * LANE
            blk = q[:, c0:c0 + LANE].T
            for hh in range(2):
                piece = blk[hh * HD:(hh + 1) * HD, :]
                cols.append(jnp.concatenate([piece, zero] if kvh == 0 else [zero, piece], axis=0))
    qbdt = jnp.concatenate(cols, axis=1).astype(BF16)
    qa_ref[0:KV_W, :] = qbdt
    qa_ref[KV_W + N_AUG:2 * KV_W, :] = jnp.zeros((KV_W - N_AUG, nl), BF16)

    lane = lax.broadcasted_iota(jnp.int32, (PB, PB), 1)
    sub = lax.broadcasted_iota(jnp.int32, (PB, PB), 0)
    causal_bias = jnp.where(lane >= sub, 0.0, NEG)
    edge_bias = jnp.where(lane < sub, 0.0, NEG)

    def compressed(rows):
        sc = jnp.dot(kc_ref[0:rows, :], qbdt, preferred_element_type=F32)
        n_i = lax.broadcasted_iota(jnp.int32, (rows, PB), 0)
        t_i = lax.broadcasted_iota(jnp.int32, (rows, PB), 1)
        d1 = (t0 + t_i - (n_i * CMP_STRIDE + (CMP_BLOCK - 1))).astype(F32)
        dc = jnp.concatenate([d1] * NSA_HEADS, axis=1)
        okc = dc >= 0.0
        sc = jnp.where(okc, sc - slopes2 * dc, NEG)
        mc = jnp.max(sc, axis=0, keepdims=True)
        pc = jnp.where(okc, jnp.exp2(sc - mc), 0.0)
        lc = jnp.sum(pc, axis=0, keepdims=True)
        pc = pc * (1.0 / jnp.where(lc > 0.0, lc, 1.0))
        pcb = pc.astype(BF16)
        for kvh in range(KVH):
            ocmp_ref[kvh] = jnp.dot(vct_ref[kvh * HD:(kvh + 1) * HD, 0:rows], pcb[:, kvh * half:(kvh + 1) * half],
                                    preferred_element_type=F32)
            psum = pc[:, kvh * half:kvh * half + PB]
            for g in range(1, GROUP):
                psum = psum + pc[:, kvh * half + g * PB:kvh * half + (g + 1) * PB]
            imp_ref[kvh] = jnp.dot(covert_ref[:, 0:rows], psum, precision=HI, preferred_element_type=F32)

    half_rows = n_seg // 2
    if half_rows % LANE == 0:
        first_full = -(-(half_rows * CMP_STRIDE + CMP_BLOCK) // PB) - 1
        pl.when(i < first_full)(lambda: compressed(half_rows))
        pl.when(i >= first_full)(lambda: compressed(n_seg))
    else:
        compressed(n_seg)

    jblk = lax.broadcasted_iota(jnp.int32, (n_sel, PB), 0)
    qblk = (t0 + lax.broadcasted_iota(jnp.int32, (n_sel, PB), 1)) // SEL_BLOCK
    back = qblk - jblk
    visible = back >= 0
    forced = (jblk == 0) | (visible & (back < SEL_LOCAL))
    n_groups = jnp.minimum((t0 + PB - 1) // SEL_BLOCK // SUBLANE + 1, n_sel // SUBLANE)
    for kvh in range(KVH):
        score = jnp.where(forced, 1e9, jnp.where(visible, imp_ref[kvh], -1e9))
        score_ref[...] = score

        def rank_group(gi, rank):
            rows8 = score_ref[pl.ds(pl.multiple_of(gi * SUBLANE, SUBLANE), SUBLANE), :]
            for u in range(SUBLANE):
                row = rows8[u:u + 1, :]
                ge = jnp.where(row >= score, 1.0, 0.0)
                gt = jnp.where(row > score, 1.0, 0.0)
                rank = rank + jnp.where(jblk > gi * SUBLANE + u, ge, gt)
            return rank

        rank = lax.fori_loop(0, n_groups, rank_group, jnp.zeros((n_sel, PB), F32))
        sel_ref[kvh] = jnp.where(visible, jnp.where(rank < float(min(SEL_TOP, n_sel)), 0.0, NEG), NEG)

    sl3 = _split3(slopes2)

    def scores(k_ref, jt, use_sel, dst_ref):
        off = (t0 - jt * PB).astype(F32)
        c3 = _split3(-slopes2 * (tlane + off))
        rows = list(sl3) + list(c3)
        if use_sel:
            tiles_per_group = SUBLANE // blocks_per_tile
            base = pl.multiple_of((jt // tiles_per_group) * SUBLANE, SUBLANE)
            which = jt % tiles_per_group
            per_kvh = []
            for kvh in range(KVH):
                rows8 = sel_ref[kvh, pl.ds(base, SUBLANE), :]
                mine = rows8[0:blocks_per_tile, :]
                for w in range(1, tiles_per_group):
                    mine = jnp.where(which == w, rows8[w * blocks_per_tile:(w + 1) * blocks_per_tile, :], mine)
                per_kvh.append(mine)
            rows.append(jnp.concatenate([per_kvh[kvh] for kvh in range(KVH) for _ in range(GROUP)], axis=1))
        rows.append(jnp.zeros((N_AUG - sum(r.shape[0] for r in rows), nl), F32))
        qa_ref[KV_W:KV_W + N_AUG, :] = jnp.concatenate(rows, axis=0).astype(BF16)
        ka = jnp.concatenate([k_ref[jt], augc_ref[...]], axis=1)
        s = jnp.dot(ka, qa_ref[...], preferred_element_type=F32)
        dst_ref[0:PB, :] = s
        dst_ref[PB:PB + 1, :] = jnp.max(s, axis=0, keepdims=True)

    def softmax_pv(vt_ref, jt, src_ref, extra, state):
        m_st, acc_st = state
        s = src_ref[0:PB, :]
        if extra is not None:
            s = s + jnp.concatenate([extra] * NSA_HEADS, axis=1)
            s_max = jnp.max(s, axis=0, keepdims=True)
        else:
            s_max = src_ref[PB:PB + 1, :]
        m_old = m_st[...]
        m_new = jnp.maximum(m_old, s_max)
        alpha = jnp.exp2(m_old - m_new)
        p = jnp.exp2(s - m_new)
        m_st[...] = m_new
        pb = p.astype(BF16)
        vt = vt_ref[jt]
        for kvh in range(KVH):
            pv = jnp.dot(vt[kvh * VROWS:(kvh + 1) * VROWS, :], pb[:, kvh * half:(kvh + 1) * half],
                         preferred_element_type=F32)
            acc_st[kvh] = alpha[:, kvh * half:(kvh + 1) * half] * acc_st[kvh] + pv

    def reset(state):
        m_st, acc_st = state
        m_st[...] = jnp.full((1, nl), M_INIT, F32)
        acc_st[...] = jnp.zeros((KVH, VROWS, half), F32)

    def result(state):
        acc_st = state[1]
        return [acc_st[kvh, 0:HD, :] * (1.0 / acc_st[kvh, HD:HD + 1, :]) for kvh in range(KVH)]

    sel_state = (m_ref, acc_ref)
    win_state = (mw_ref, accw_ref)

    reset(sel_state)
    n_pairs = (i + 2) // 2
    odd_i = (i % 2) == 1
    scores(ks_ref, 0, True, sa_ref)

    def pair(u, carry):
        scores(ks_ref, 2 * u + 1, True, sb_ref)
        softmax_pv(vst_ref, 2 * u, sa_ref, None, sel_state)
        scores(ks_ref, 2 * u + 2, True, sa_ref)
        softmax_pv(vst_ref, 2 * u + 1, sb_ref, None, sel_state)
        return carry

    lax.fori_loop(0, n_pairs - 1, pair, 0)
    def sel_tail(two_tiles):
        if two_tiles:
            scores(ks_ref, i, True, sb_ref)
            yield
            softmax_pv(vst_ref, i - 1, sa_ref, None, sel_state)
            yield
            softmax_pv(vst_ref, i, sb_ref, causal_bias, sel_state)
        else:
            softmax_pv(vst_ref, i, sa_ref, causal_bias, sel_state)
        yield

    w0 = jnp.maximum(i - 2, 0)
    w1 = jnp.maximum(i - 1, 0)

    def window():
        reset(win_state)
        scores(kw_ref, w0, False, sc_ref)
        yield
        scores(kw_ref, w1, False, sd_ref)
        softmax_pv(vwt_ref, w0, sc_ref, jnp.where(i >= 2, edge_bias, NEG), win_state)
        yield
        scores(kw_ref, i, False, sc_ref)
        softmax_pv(vwt_ref, w1, sd_ref, jnp.where(i >= 1, 0.0, jnp.full((PB, PB), NEG, F32)), win_state)
        yield
        softmax_pv(vwt_ref, i, sc_ref, causal_bias, win_state)
        yield

    def finish(two_tiles):
        for _ in itertools.zip_longest(sel_tail(two_tiles), window()):
            pass
        osel = result(sel_state)
        owin = result(win_state)

        gt_ = _sigmoid(gn_ref[...]).T
        for kvh in range(KVH):
            for p in range(GROUP // 2):
                pieces = []
                for hh in range(2):
                    g = 2 * p + hh
                    h = kvh * GROUP + g
                    cs = slice(g * PB, (g + 1) * PB)
                    pieces.append(gt_[3 * h:3 * h + 1, :] * ocmp_ref[kvh, :, cs]
                                  + gt_[3 * h + 1:3 * h + 2, :] * osel[kvh][:, cs]
                                  + gt_[3 * h + 2:3 * h + 3, :] * owin[kvh][:, cs])
                c0 = (kvh * (GROUP // 2) + p) * LANE
                o_ref[:, c0:c0 + LANE] = jnp.concatenate(pieces, axis=0).T.astype(BF16)

    pl.when(odd_i)(lambda: finish(True))
    pl.when(jnp.logical_not(odd_i))(lambda: finish(False))


def _aug_key_columns():
    a = np.zeros((PB, KV_W), np.float32)
    s = np.arange(PB)
    a[:, 0:3] = s[:, None]
    a[:, 3:6] = 1.0
    for r in range(PB // SEL_BLOCK):
        a[:, 6 + r] = (s // SEL_BLOCK == r)
    return a


def _nsa_prompt256(z, kc, vct, b, t):
    nq = t // PB
    n_seg = t // CMP_STRIDE
    n_sel = t // SEL_BLOCK
    nl = NSA_HEADS * PB
    assert 6 + PB // SEL_BLOCK <= N_AUG and n_sel % SUBLANE == 0 and WINDOW == 2 * PB
    slopes = _alibi_slopes_lanes(PB)
    covert = _cover_matrix(n_seg, n_sel).T.copy()
    kv_col = lambda c0: pl.BlockSpec((t, KV_W), lambda bi, i: (bi, c0 // KV_W))
    return pl.pallas_call(
        _nsa_prompt256_kernel,
        grid=(b, nq),
        in_specs=[pl.BlockSpec((PB, NSA_W), lambda bi, i: (bi * nq + i, C_Q // NSA_W)),
                  pl.BlockSpec((PB, LANE), lambda bi, i: (bi * nq + i, C_GN // LANE)),
                  pl.BlockSpec((None, n_seg, KV_W), lambda bi, i: (bi, 0, 0)),
                  pl.BlockSpec((None, KV_W, n_seg), lambda bi, i: (bi, 0, 0)),
                  kv_col(C_KVS), kv_col(C_KVS + KV_W), kv_col(C_KVW), kv_col(C_KVW + KV_W),
                  _const_spec((PB, KV_W)), _const_spec((1, nl)), _const_spec((n_sel, n_seg))],
        out_specs=pl.BlockSpec((PB, NSA_W), lambda bi, i: (bi * nq + i, 0)),
        out_shape=jax.ShapeDtypeStruct((b * t, NSA_W), BF16),
        scratch_shapes=[pltpu.VMEM((2 * KV_W, nl), BF16), pltpu.VMEM((1, nl), F32),
                        pltpu.VMEM((KVH, VROWS, GROUP * PB), F32), pltpu.VMEM((KVH, n_sel, PB), F32),
                        pltpu.VMEM((n_sel, PB), F32),
                        pltpu.VMEM((nq, PB, KV_W), BF16), pltpu.VMEM((nq, KVH * VROWS, PB), BF16),
                        pltpu.VMEM((nq, PB, KV_W), BF16), pltpu.VMEM((nq, KVH * VROWS, PB), BF16),
                        pltpu.VMEM((PB + SUBLANE, nl), F32), pltpu.VMEM((PB + SUBLANE, nl), F32),
                        pltpu.VMEM((PB + SUBLANE, nl), F32), pltpu.VMEM((PB + SUBLANE, nl), F32),
                        pltpu.VMEM((1, nl), F32), pltpu.VMEM((KVH, VROWS, GROUP * PB), F32),
                        pltpu.VMEM((KVH, HD, GROUP * PB), F32), pltpu.VMEM((KVH, n_sel, PB), F32)],
        compiler_params=_cparams(("parallel", "arbitrary")),
        name="nsa_prompt",
    )(z, z, kc, vct, z, z, z, z, jnp.asarray(_aug_key_columns(), BF16), jnp.asarray(slopes),
      jnp.asarray(covert))


def _alibi_slopes_lanes(width):
    h = np.arange(1, NSA_HEADS + 1, dtype=np.float32)
    return np.repeat(np.exp2(-8.0 * h / NSA_HEADS), width)[None, :].astype(np.float32)


def _cover_matrix(n_cmp_rows, n_sel_rows):
    n = np.arange(n_cmp_rows)[:, None] * CMP_STRIDE
    j = np.arange(n_sel_rows)[None, :] * SEL_BLOCK
    return ((n < j + SEL_BLOCK) & (n + CMP_BLOCK > j)).astype(np.float32)


def _hgrn_gates(pre, lb):
    e = jnp.exp(-pre)
    r = 1.0 / (1.0 + e)
    log_f = jnp.log(lb + (1.0 - lb) * r)
    k = (1.0 - lb) * jnp.where(pre > 0.0, e * r, 1.0 - r)
    return log_f, k


def _hgrn_out(o, gate, ng):
    outs = []
    for h in range(HG_H):
        sl = slice(h * HG_D, (h + 1) * HG_D)
        g = gate[:, sl]
        outs.append(_rms(o[:, sl], ng) * (g * _sigmoid(g)))
    return jnp.concatenate(outs, axis=1)


def _hgrn_prompt_kernel(hq_ref, hf_ref, hi_ref, hg_ref, lb_ref, ng_ref, tri_ref, o_ref, st_ref,
                        s_ref, oraw_ref):
    ci = pl.program_id(1)
    tc = hq_ref.shape[0]
    c = HG_CHUNK

    @pl.when(ci == 0)
    def _():
        s_ref[...] = jnp.zeros(s_ref.shape, F32)

    lb = lb_ref[...]
    tril = (lax.broadcasted_iota(jnp.int32, (c, c), 0) >= lax.broadcasted_iota(jnp.int32, (c, c), 1))
    row8 = lax.broadcasted_iota(jnp.int32, (SUBLANE, HG_D), 0)

    n_chunks = tc // c
    q_all = hq_ref[...]
    v_all = hi_ref[...]
    log_f, k_all = _hgrn_gates(hf_ref[...], lb)
    chunks = []
    worst = None
    for g in range(n_chunks):
        rows = slice(g * c, (g + 1) * c)
        bcum = jnp.dot(tri_ref[...], log_f[rows], precision=HI, preferred_element_type=F32)
        e = bcum - bcum[c // 2 - 1:c // 2, :]
        chunks.append((rows, bcum, e))
        worst = jnp.abs(e) if worst is None else jnp.maximum(worst, jnp.abs(e))
    safe = jnp.max(worst) < HG_SAFE_EXP

    @pl.when(safe)
    def _():
        for rows, bcum, e in chunks:
            q, k, v = q_all[rows], k_all[rows], v_all[rows]
            b_last = bcum[c - 1:c, :]
            qt = (q * jnp.exp(e)).astype(BF16)
            kt = (k * jnp.exp(-e)).astype(BF16)
            qb = (q * jnp.exp(bcum)).astype(BF16)
            kh = (k * jnp.exp(b_last - bcum)).astype(BF16)
            dec = jnp.exp(b_last)
            vb = v.astype(BF16)
            def head(h, rows=rows, qt=qt, kt=kt, qb=qb, kh=kh, dec=dec, vb=vb):
                sl = slice(h * HG_D, (h + 1) * HG_D)
                a = lax.dot_general(qt[:, sl], kt[:, sl], (((1,), (1,)), ((), ())), preferred_element_type=F32)
                st = s_ref[h]
                o = lax.dot_general(qb[:, sl], st.astype(BF16), (((1,), (1,)), ((), ())),
                                    preferred_element_type=F32)
                upd = lax.dot_general(vb[:, sl], kh[:, sl], (((0,), (0,)), ((), ())), preferred_element_type=F32)
                yield
                a = jnp.where(tril, a, 0.0).astype(BF16)
                s_ref[h] = st * dec[:, sl] + upd
                yield
                oraw_ref[rows, sl] = o + jnp.dot(a, vb[:, sl], preferred_element_type=F32)
                yield

            for _ in zip(*[head(h) for h in range(HG_H)]):
                pass

    @pl.when(jnp.logical_not(safe))
    def _():
        for h in range(HG_H):
            sl = slice(h * HG_D, (h + 1) * HG_D)

            def tile(ti, carry, sl=sl, h=h):
                r = pl.multiple_of(ti * SUBLANE, SUBLANE)
                q8 = hq_ref[pl.ds(r, SUBLANE), sl]
                v8 = hi_ref[pl.ds(r, SUBLANE), sl]
                lf8, k8 = _hgrn_gates(hf_ref[pl.ds(r, SUBLANE), sl], lb[:, sl])
                f8 = jnp.exp(lf8)
                st = s_ref[h]
                rows_out = []
                for u in range(SUBLANE):
                    vu = jnp.where(row8 == 0, jnp.broadcast_to(v8[u:u + 1, :], (SUBLANE, HG_D)), 0.0)
                    ku = jnp.broadcast_to(k8[u:u + 1, :], (SUBLANE, HG_D))
                    qu = jnp.broadcast_to(q8[u:u + 1, :], (SUBLANE, HG_D))
                    st = st * f8[u:u + 1, :] + lax.dot_general(
                        vu, ku, (((0,), (0,)), ((), ())), precision=HI, preferred_element_type=F32)
                    ou = lax.dot_general(qu, st, (((1,), (1,)), ((), ())), precision=HI,
                                         preferred_element_type=F32)
                    rows_out.append(ou[0:1, :])
                s_ref[h] = st
                oraw_ref[pl.ds(r, SUBLANE), sl] = jnp.concatenate(rows_out, axis=0)
                return carry

            lax.fori_loop(0, tc // SUBLANE, tile, 0)

    o_ref[...] = _hgrn_out(oraw_ref[...], hg_ref[...], ng_ref[...]).astype(BF16)

    @pl.when(ci == pl.num_programs(1) - 1)
    def _():
        for h in range(HG_H):
            st_ref[h] = s_ref[h].T


def _hgrn_prompt(z, lb, ng, b, t, tc):
    nc = t // tc
    col = lambda c0: pl.BlockSpec((tc, HG_W), lambda bi, ci: (bi * nc + ci, c0 // HG_W))
    tri = np.tril(np.ones((HG_CHUNK, HG_CHUNK), np.float32))
    return pl.pallas_call(
        _hgrn_prompt_kernel,
        grid=(b, nc),
        in_specs=[col(C_HQ), col(C_HF), col(C_HI), col(C_HG),
                  _const_spec((1, HG_W)), _const_spec((1, HG_D)), _const_spec((HG_CHUNK, HG_CHUNK))],
        out_specs=[pl.BlockSpec((tc, HG_W), lambda bi, ci: (bi * nc + ci, 0)),
                   pl.BlockSpec((None, HG_H, HG_D, HG_D), lambda bi, ci: (bi, 0, 0, 0))],
        out_shape=[jax.ShapeDtypeStruct((b * t, HG_W), BF16),
                   jax.ShapeDtypeStruct((b, HG_H, HG_D, HG_D), F32)],
        scratch_shapes=[pltpu.VMEM((HG_H, HG_D, HG_D), F32), pltpu.VMEM((tc, HG_W), F32)],
        compiler_params=_cparams(("parallel", "arbitrary")),
        name="hgrn_prompt",
    )(z, z, z, z, lb, ng, jnp.asarray(tri))


def _merge_kernel(on_ref, oh_ref, ga_ref, gb_ref, x_ref, wn_ref, wh_ref, wo_ref, g_ref, o_ref):
    y = (_sigmoid(ga_ref[...]) * jnp.dot(on_ref[...], wn_ref[...], preferred_element_type=F32)
         + _sigmoid(gb_ref[...]) * jnp.dot(oh_ref[...], wh_ref[...], preferred_element_type=F32))
    mix = jnp.dot(y.astype(BF16), wo_ref[...], preferred_element_type=F32)
    o_ref[...] = x_ref[...] + _rms(mix, g_ref[...])


def _resident(shape):
    nd = len(shape)
    return pl.BlockSpec(shape, lambda *_: (0,) * nd, pipeline_mode=pl.Buffered(1))


def _merge(o_nsa, o_hg, z, x, wn, wh, wo, g, tm):
    n = x.shape[0]
    row = lambda w, cb: pl.BlockSpec((tm, w), lambda i: (i, cb))
    return pl.pallas_call(
        _merge_kernel,
        grid=(n // tm,),
        in_specs=[row(NSA_W, 0), row(HG_W, 0), row(D_MODEL, C_GM // D_MODEL), row(D_MODEL, C_GM // D_MODEL + 1),
                  row(D_MODEL, 0), _resident(wn.shape), _resident(wh.shape), _resident(wo.shape),
                  _resident((1, D_MODEL))],
        out_specs=row(D_MODEL, 0),
        out_shape=jax.ShapeDtypeStruct((n, D_MODEL), F32),
        compiler_params=_cparams(("parallel",)),
        name="merge_out_proj",
    )(o_nsa, o_hg, z, z, x, wn, wh, wo, g)


def _mlp_kernel(h_ref, p_ref, wu_ref, wd_ref, wg_ref, wp_ref, g1_ref, g2_ref, g3_ref, o_ref):
    h = h_ref[...]
    xn = _rms(h, g1_ref[...]).astype(BF16)
    ffn = jnp.zeros(h.shape, F32)
    step = D_MODEL // 2
    for c0 in range(0, D_FF, step):
        up = jnp.dot(xn, wu_ref[:, c0:c0 + step], preferred_element_type=F32)
        act = jnp.square(jnp.maximum(up, 0.0)).astype(BF16)
        ffn = ffn + jnp.dot(act, wd_ref[c0:c0 + step, :], preferred_element_type=F32)
    h = h + _rms(ffn, g2_ref[...])
    gate = _sigmoid(jnp.dot(_rms(h, g3_ref[...]).astype(BF16), wg_ref[...], preferred_element_type=F32))
    o_ref[...] = h + gate * jnp.dot(p_ref[...].astype(BF16), wp_ref[...], preferred_element_type=F32)


def _mlp(h, p, wu, wd, wg, wp, g1, g2, g3, tm):
    n = h.shape[0]
    row = lambda w: pl.BlockSpec((tm, w), lambda i: (i, 0))
    gain = _resident((1, D_MODEL))
    return pl.pallas_call(
        _mlp_kernel,
        grid=(n // tm,),
        in_specs=[row(D_MODEL), row(PLE_DIM), _resident(wu.shape), _resident(wd.shape), _resident(wg.shape),
                  _resident(wp.shape), gain, gain, gain],
        out_specs=row(D_MODEL),
        out_shape=jax.ShapeDtypeStruct((n, D_MODEL), F32),
        compiler_params=_cparams(("parallel",)),
        name="mlp_ple",
    )(h, p, wu, wd, wg, wp, g1, g2, g3)


def _prep_w_in(w):
    sizes = (NSA_W, 2 * KV_W, 2 * KV_W, 2 * KV_W, 3 * NSA_HEADS, HG_W, HG_W, HG_W, HG_W)
    q, kvc, kvs, kvw, gn, hq, hf, hi, hg, gm = jnp.split(w, [int(v) for v in np.cumsum(sizes)], axis=1)
    pad = jnp.zeros((w.shape[0], Z_COLS - C_GN - 3 * NSA_HEADS), w.dtype)
    return jnp.concatenate([gm, q, hq, hf, hi, hg, kvc, kvs, kvw, gn, pad], axis=1).astype(BF16)


def _prep_compress(w1k, w2k, w1v, w2v, pos):
    eye = jnp.eye(KVH, dtype=F32)

    def big1(w1):
        t = jnp.einsum('rsdh,kq->rskdqh', w1, eye)
        return t.reshape(CMP_BLOCK // CMP_STRIDE, CMP_STRIDE * KV_W, KVH * CMP_HIDDEN)

    def big2(w2):
        return jnp.einsum('hd,kq->khqd', w2, eye).reshape(KVH * CMP_HIDDEN, KV_W)

    w1 = jnp.stack([big1(w1k), big1(w1v)]).astype(BF16)
    w2 = jnp.stack([big2(w2k), big2(w2v)]).astype(BF16)
    posb = jnp.broadcast_to(pos[:, :, None, :], pos.shape[:2] + (KVH, HD)).reshape(pos.shape[0], 1, -1)
    posb = jnp.broadcast_to(posb, (pos.shape[0], SUBLANE, posb.shape[-1])).astype(BF16)
    return w1, w2, posb


def _kv_rows_t_kernel(ck_ref, cv_ref, sk_ref, sv_ref, wk_ref, wv_ref, ct_ref, st_ref, wt_ref):
    rows = ck_ref.shape[0]
    for src, dst, part in ((ck_ref, ct_ref, 0), (cv_ref, ct_ref, 1), (sk_ref, st_ref, 0), (sv_ref, st_ref, 1),
                           (wk_ref, wt_ref, 0), (wv_ref, wt_ref, 1)):
        for j in range(rows // LANE):
            dst[part * KV_W:(part + 1) * KV_W, j * LANE:(j + 1) * LANE] = src[j * LANE:(j + 1) * LANE, :].T


def _kv_rows_t(z, b, t, rows):
    nt = t // rows
    col = lambda c0: pl.BlockSpec((rows, KV_W), lambda bi, j: (bi * nt + j, c0 // KV_W))
    full = pl.BlockSpec((None, 2 * KV_W, rows), lambda bi, j: (bi, 0, j))
    tail = pl.BlockSpec((None, 2 * KV_W, rows), lambda bi, j: (bi, 0, 0))
    return pl.pallas_call(
        _kv_rows_t_kernel,
        grid=(b, nt),
        in_specs=[col(C_KVC), col(C_KVC + KV_W), col(C_KVS), col(C_KVS + KV_W), col(C_KVW), col(C_KVW + KV_W)],
        out_specs=[full, full, tail],
        out_shape=[jax.ShapeDtypeStruct((b, 2 * KV_W, t), F32), jax.ShapeDtypeStruct((b, 2 * KV_W, t), F32),
                   jax.ShapeDtypeStruct((b, 2 * KV_W, rows), F32)],
        compiler_params=_cparams(("parallel", "arbitrary")),
        name="kv_rows_t",
    )(z, z, z, z, z, z)


def _prep_layer(i, lb_all, w_in, cmp_k_w1, cmp_k_w2, cmp_v_w1, cmp_v_w2, cmp_pos, hg_norm, w_branch_nsa,
                w_branch_hgrn, w_out, norm_pre_mix, norm_post_mix, norm_pre_mlp, norm_post_mlp, w_mlp_up,
                w_mlp_down, norm_ple, w_ple_gate, w_ple_proj):
    w1, w2, posb = _prep_compress(cmp_k_w1[i], cmp_k_w2[i], cmp_v_w1[i], cmp_v_w2[i], cmp_pos[i])
    row = lambda a: a[i].reshape(1, -1).astype(F32)
    return {
        'w_in': _prep_w_in(w_in[i]), 'cmp_w1': w1, 'cmp_w2': w2, 'cmp_pos': posb,
        'hg_lb': lb_all[i].reshape(1, HG_W), 'hg_norm': row(hg_norm),
        'w_bn': w_branch_nsa[i].astype(BF16), 'w_bh': w_branch_hgrn[i].astype(BF16), 'w_out': w_out[i].astype(BF16),
        'w_up': w_mlp_up[i].astype(BF16), 'w_down': w_mlp_down[i].astype(BF16),
        'w_gate': w_ple_gate[i].astype(BF16), 'w_proj': w_ple_proj[i].astype(BF16),
        'g_pre_mix': row(norm_pre_mix), 'g_post_mix': row(norm_post_mix), 'g_pre_mlp': row(norm_pre_mlp),
        'g_post_mlp': row(norm_post_mlp), 'g_ple': row(norm_ple),
    }


def _layer_prompt(x, p, lw, b, t):
    n = b * t
    z = _norm_matmul(x, lw['g_pre_mix'], lw['w_in'], tm=min(1024, n), tn=Z_COLS // 4)
    wb = min(WINDOW, t)
    kv_c, kv_s, kv_w = _kv_rows_t(z, b, t, wb)
    kc, vc = _compress_prompt(z, lw['cmp_w1'], lw['cmp_w2'], lw['cmp_pos'], b, t)
    vct = jnp.swapaxes(vc, 1, 2)
    o_nsa = _nsa_prompt256(z, kc, vct, b, t)
    o_hg, st = _hgrn_prompt(z, lw['hg_lb'], lw['hg_norm'], b, t, tc=min(256, t))
    h1 = _merge(o_nsa, o_hg, z, x, lw['w_bn'], lw['w_bh'], lw['w_out'], lw['g_post_mix'], tm=min(512, n))
    h2 = _mlp(h1, p, lw['w_up'], lw['w_down'], lw['w_gate'], lw['w_proj'],
              lw['g_pre_mlp'], lw['g_post_mlp'], lw['g_ple'], tm=min(1024, n))
    kv6 = lambda a: a.reshape(b, 2, KVH, HD, a.shape[-1]).transpose(0, 4, 1, 2, 3)
    return h2, kv6(kv_c), kv6(kv_s), kv6(kv_w), st


SEG_PITCH = 24


def _page_fetch(pt_ref, cache_ref, buf_ref, sem, nseq=1):
    b = pl.program_id(0)
    n_pages = pt_ref.shape[1]
    slot = b % 2

    def copy(step, sl, p):
        seq = step * nseq + p // n_pages
        return pltpu.make_async_copy(cache_ref.at[pt_ref[seq, p % n_pages]], buf_ref.at[sl, p], sem.at[sl])

    def start(step, sl):
        lax.fori_loop(0, nseq * n_pages, lambda p, c: (copy(step, sl, p).start(), c)[1], 0)

    @pl.when(b == 0)
    def _():
        start(0, 0)

    @pl.when(b + 1 < pl.num_programs(0))
    def _():
        start(b + 1, 1 - slot)

    def wait():
        lax.fori_loop(0, nseq * n_pages, lambda p, c: (copy(b, slot, p).wait(), c)[1], 0)

    return slot, wait


def _compress_sample_kernel(pt_ref, cache_ref, w1_ref, w2_ref, pos_ref, kc_ref, vc_ref, buf_ref, rows_ref, sem):
    n_pages, page = pt_ref.shape[1], buf_ref.shape[-1]
    n_seg = n_pages * page // CMP_STRIDE
    slot, wait = _page_fetch(pt_ref, cache_ref, buf_ref, sem)
    wait()
    segs_per_page = page // CMP_STRIDE
    for kv, out_ref in ((0, kc_ref), (1, vc_ref)):
        for p in range(n_pages):
            rows = buf_ref[slot, p, kv].T
            for g in range(segs_per_page):
                r0 = (p * segs_per_page + g) * SEG_PITCH
                rows_ref[kv, r0:r0 + CMP_STRIDE, :] = rows[g * CMP_STRIDE:(g + 1) * CMP_STRIDE, :]
        load = lambda s, kv=kv: rows_ref[kv, pl.ds(s, n_seg, stride=SEG_PITCH), :]
        out_ref[...] = _compress_math(load, n_seg, w1_ref, w2_ref, pos_ref, kv).astype(BF16)


def _compress_sample(page_table, cache_t, w1, w2, pos):
    bs, n_pages = page_table.shape
    page = cache_t.shape[-1]
    n_seg = n_pages * page // CMP_STRIDE
    const = lambda shape: pl.BlockSpec(shape, lambda i, pt: (0,) * len(shape), pipeline_mode=pl.Buffered(1))
    out = pl.BlockSpec((None, n_seg, KV_W), lambda i, pt: (i, 0, 0))
    return pl.pallas_call(
        _compress_sample_kernel,
        grid_spec=pltpu.PrefetchScalarGridSpec(
            num_scalar_prefetch=1, grid=(bs,),
            in_specs=[pl.BlockSpec(memory_space=pl.ANY), const(w1.shape), const(w2.shape), const(pos.shape)],
            out_specs=[out, out],
            scratch_shapes=[pltpu.VMEM((2, n_pages, 2, KV_W, page), F32),
                            pltpu.VMEM((2, n_seg * SEG_PITCH, KV_W), F32), pltpu.SemaphoreType.DMA((2,))]),
        out_shape=[jax.ShapeDtypeStruct((bs, n_seg, KV_W), BF16)] * 2,
        compiler_params=_cparams(("arbitrary",)),
        name="compress_sample",
    )(page_table, cache_t, w1, w2, pos)


_NT = (((1,), (1,)), ((), ()))


def _nsa_sample_kernel(pt_ref, qbd_ref, gl_ref, kc_ref, vc_ref, snew_ref, cwin_ref, wnew_ref, csel_ref,
                       slope_ref, tq_ref, tq8_ref, cover_ref, gsum_ref, gexp_ref, spread_ref, o_ref,
                       buf_ref, s_ref, sem):
    nseq = qbd_ref.shape[0]
    slot, wait = _page_fetch(pt_ref, csel_ref, buf_ref, sem, nseq)
    wait()
    chains = [_nsa_sample_one(pt_ref.shape[1], u, slot, qbd_ref.at[u], gl_ref.at[u], kc_ref.at[u], vc_ref.at[u],
                              snew_ref.at[u], cwin_ref.at[u], wnew_ref.at[u], slope_ref, tq_ref, tq8_ref,
                              cover_ref, gsum_ref, gexp_ref, spread_ref, o_ref.at[u], buf_ref, s_ref.at[u])
              for u in range(nseq)]
    for _ in zip(*chains):
        pass


def _nsa_sample_one(n_pages, u, slot, qbd_ref, gl_ref, kc_ref, vc_ref, snew_ref, cwin_ref, wnew_ref,
                    slope_ref, tq_ref, tq8_ref, cover_ref, gsum_ref, gexp_ref, spread_ref, o_ref, buf_ref, s_ref):
    page = buf_ref.shape[-1]
    past = n_pages * page
    n_seg = kc_ref.shape[0]
    npad = cover_ref.shape[1]
    n_sel = past // SEL_BLOCK + 1
    nr = qbd_ref.shape[0]
    wb = cwin_ref.shape[-1]

    qb = (qbd_ref[...] * SCALE).astype(BF16)
    slope = slope_ref[...]
    qpos = tq_ref[...] + float(past)
    zeros_pad = jnp.zeros((QB - snew_ref.shape[0], KV_W), F32)
    t_new = lax.broadcasted_iota(jnp.int32, (1, QB), 1).astype(F32)

    sc = lax.dot_general(qb, kc_ref[...], _NT, preferred_element_type=F32)
    n_i = lax.broadcasted_iota(jnp.int32, (1, n_seg), 1)
    dcmp = qpos - (n_i * CMP_STRIDE + (CMP_BLOCK - 1)).astype(F32)
    okc = dcmp >= 0.0
    sc = jnp.where(okc, sc - slope * dcmp, NEG)
    yield
    mc = jnp.max(sc, axis=-1, keepdims=True)
    pc = jnp.where(okc, jnp.exp(sc - mc), 0.0)
    lc = jnp.sum(pc, axis=-1, keepdims=True)
    pc = pc * (1.0 / jnp.where(lc > 0.0, lc, 1.0))
    yield
    o_cmp = jnp.dot(pc.astype(BF16), vc_ref[...], preferred_element_type=F32)

    psum = jnp.dot(gsum_ref[...], pc, precision=HI, preferred_element_type=F32)
    imp = jnp.dot(psum, cover_ref[...], precision=HI, preferred_element_type=F32)
    yield
    nq8 = gsum_ref.shape[0]
    jblk = lax.broadcasted_iota(jnp.int32, (nq8, npad), 1)
    qblk = (tq8_ref[...].astype(jnp.int32) + past) // SEL_BLOCK
    back = qblk - jblk
    visible = back >= 0
    forced = (jblk == 0) | (visible & (back < SEL_LOCAL))
    score = jnp.where(forced, 1e9, jnp.where(visible, imp, -1e9))
    rank = jnp.zeros((nq8, npad), F32)
    for jp in range(n_sel):
        col = score[:, jp:jp + 1]
        ge = jnp.where(col >= score, 1.0, 0.0)
        gt = jnp.where(col > score, 1.0, 0.0)
        rank = rank + jnp.where(jblk > jp, ge, gt)
        if jp % 16 == 15:
            yield
    sel8 = jnp.where(visible, jnp.where(rank < float(min(SEL_TOP, n_sel)), 1.0, 0.0), 0.0)
    negb = (jnp.dot(gexp_ref[...], sel8, precision=HI, preferred_element_type=F32) - 1.0) * (-NEG)
    yield

    def tile_bias(jt):
        lane = lax.broadcasted_iota(jnp.int32, (nr, QB), 1)
        return jnp.where(lane < SEL_BLOCK, negb[:, 2 * jt:2 * jt + 1], negb[:, 2 * jt + 1:2 * jt + 2])

    ppc = 4
    ck = ppc * page
    bpc = ck // SEL_BLOCK

    def chunk_t(c, kv):
        return jnp.concatenate([buf_ref[slot, u * n_pages + c * ppc + w, kv] for w in range(ppc)],
                               axis=1).astype(BF16)

    slope_ck = jnp.concatenate([jnp.broadcast_to(slope, (nr, LANE))] * (ck // LANE), axis=1)
    qpos_ck = jnp.concatenate([jnp.broadcast_to(qpos, (nr, LANE))] * (ck // LANE), axis=1)
    negb_bf = negb.astype(BF16)
    for c in range(n_pages // ppc):
        s = jnp.dot(qb, chunk_t(c, 0), preferred_element_type=F32)
        kpos = (lax.broadcasted_iota(jnp.int32, (1, ck), 1) + c * ck).astype(F32)
        b0 = c * bpc
        bias = jnp.dot(negb_bf[:, (b0 // LANE) * LANE:(b0 // LANE + 1) * LANE], spread_ref[(b0 % LANE) // bpc],
                       preferred_element_type=F32)
        s_ref[:, c * ck:(c + 1) * ck] = s - slope_ck * (qpos_ck - kpos) + bias
        if c % 4 == 3:
            yield
    knew =jnp.concatenate([snew_ref[:, 0:KV_W], zeros_pad], axis=0).astype(BF16)
    vnew = jnp.concatenate([snew_ref[:, KV_W:2 * KV_W], zeros_pad], axis=0).astype(BF16)
    dnew = tq_ref[...] - t_new
    s = lax.dot_general(qb, knew, _NT, preferred_element_type=F32)
    s_ref[:, past:past + QB] = jnp.where(dnew >= 0.0, s - slope * dnew + tile_bias(past // QB), NEG)
    s_all = s_ref[...]
    ms = jnp.max(s_all, axis=-1, keepdims=True)
    ps = jnp.exp(s_all - ms)
    ls = jnp.sum(ps, axis=-1, keepdims=True)
    yield
    psb = ps.astype(BF16)
    o_sel = jnp.dot(psb[:, past:past + QB], vnew, preferred_element_type=F32)
    for c in range(n_pages // ppc):
        o_sel = o_sel + lax.dot_general(psb[:, c * ck:(c + 1) * ck], chunk_t(c, 1), _NT,
                                        preferred_element_type=F32)
        if c % 4 == 3:
            yield
    o_sel = o_sel * (1.0 / ls)

    s1 = jnp.dot(qb, cwin_ref[0].astype(BF16), preferred_element_type=F32)
    d1 = float(wb) + tq_ref[...] - lax.broadcasted_iota(jnp.int32, (1, wb), 1).astype(F32)
    ok1 = d1 < float(WINDOW)
    s1 = jnp.where(ok1, s1 - slope * d1, NEG)
    yield
    wk =jnp.concatenate([wnew_ref[:, 0:KV_W], zeros_pad], axis=0).astype(BF16)
    wv = jnp.concatenate([wnew_ref[:, KV_W:2 * KV_W], zeros_pad], axis=0).astype(BF16)
    ok2 = dnew >= 0.0
    s2 = jnp.where(ok2, lax.dot_general(qb, wk, _NT, preferred_element_type=F32) - slope * dnew, NEG)
    mw = jnp.maximum(jnp.max(s1, axis=-1, keepdims=True), jnp.max(s2, axis=-1, keepdims=True))
    p1 = jnp.where(ok1, jnp.exp(s1 - mw), 0.0)
    p2 = jnp.where(ok2, jnp.exp(s2 - mw), 0.0)
    lw_ = jnp.sum(p1, axis=-1, keepdims=True) + jnp.sum(p2, axis=-1, keepdims=True)
    yield
    o_win =(lax.dot_general(p1.astype(BF16), cwin_ref[1].astype(BF16), _NT, preferred_element_type=F32)
             + jnp.dot(p2.astype(BF16), wv, preferred_element_type=F32)) * (1.0 / lw_)

    sig = _sigmoid(gl_ref[...])
    o_ref[...] = sig[:, 0:1] * o_cmp + sig[:, 1:2] * o_sel + sig[:, 2:3] * o_win
    yield


def _nsa_sample(page_table, qbd, gl, kc, vc, snew, cwin, wnew, csel, ts):
    bs, n_pages = page_table.shape
    page = csel.shape[-1]
    past = n_pages * page
    n_seg = kc.shape[1]
    n_sel = past // SEL_BLOCK + 1
    npad = -(-(n_sel + 1) // LANE) * LANE
    nr = KVH * GROUP * ts
    wb = cwin.shape[-1]
    r = np.arange(nr)
    slope = np.exp2(-8.0 * ((r // ts) + 1) / NSA_HEADS).astype(np.float32)[:, None]
    tq = (r % ts).astype(np.float32)[:, None]
    r8 = np.arange(KVH * ts)
    tq8 = (r8 % ts).astype(np.float32)[:, None]
    cover = np.zeros((n_seg, npad), np.float32)
    cover[:, :n_sel] = _cover_matrix(n_seg, n_sel)
    gsum = ((r[None, :] // (GROUP * ts) == r8[:, None] // ts) & (r[None, :] % ts == r8[:, None] % ts)).astype(np.float32)
    ck = 4 * page
    bpc = ck // SEL_BLOCK
    j_i = np.arange(LANE)[None, :, None]
    spread = (j_i == (np.arange(LANE // bpc)[:, None, None] * bpc + np.arange(ck)[None, None, :] // SEL_BLOCK))
    consts = [slope, tq, tq8, cover, gsum, gsum.T.copy(), jnp.asarray(spread, BF16)]
    const = lambda shape: pl.BlockSpec(shape, lambda i, pt: (0,) * len(shape), pipeline_mode=pl.Buffered(1))
    nseq = 2 if bs % 2 == 0 else 1
    per = lambda *s: pl.BlockSpec((nseq,) + s, lambda i, pt: (i,) + (0,) * len(s))
    return pl.pallas_call(
        _nsa_sample_kernel,
        grid_spec=pltpu.PrefetchScalarGridSpec(
            num_scalar_prefetch=1, grid=(bs // nseq,),
            in_specs=[per(nr, KV_W), per(nr, LANE), per(n_seg, KV_W), per(n_seg, KV_W), per(SUBLANE, 2 * KV_W),
                      per(2, KV_W, wb), per(SUBLANE, 2 * KV_W), pl.BlockSpec(memory_space=pl.ANY)]
                     + [const(c.shape) for c in consts],
            out_specs=per(nr, KV_W),
            scratch_shapes=[pltpu.VMEM((2, nseq * n_pages, 2, KV_W, page), F32),
                            pltpu.VMEM((nseq, nr, past + QB), F32), pltpu.SemaphoreType.DMA((2,))]),
        out_shape=jax.ShapeDtypeStruct((bs, nr, KV_W), F32),
        compiler_params=_cparams(("arbitrary",)),
        name="nsa_sample",
    )(page_table, qbd, gl, kc, vc, snew, cwin, wnew, csel, *[jnp.asarray(c) for c in consts])


def _hgrn_sample_kernel(ts, hq_ref, hf_ref, hi_ref, hg_ref, lb_ref, ng_ref, s0_ref, o_ref, s1_ref):
    rows = hq_ref.shape[0]
    q = hq_ref[...]
    v = hi_ref[...]
    log_f, k = _hgrn_gates(hf_ref[...], lb_ref[...])
    tloc = lax.broadcasted_iota(jnp.int32, (rows, HG_W), 0) % ts
    up = lambda a, d: pltpu.roll(a, d, axis=0)
    down = lambda a, d: pltpu.roll(a, rows - d, axis=0)

    bcum = log_f
    for d in range(1, ts):
        bcum = bcum + jnp.where(tloc >= d, up(log_f, d), 0.0)
    b_last = bcum
    for d in range(1, ts):
        b_last = jnp.where(tloc == ts - 1 - d, down(bcum, d), b_last)

    o_intra = [jnp.zeros((rows, HG_D), F32) for _ in range(HG_H)]
    for d in range(ts):
        kd, bd, vd = (k, bcum, v) if d == 0 else (up(k, d), up(bcum, d), up(v, d))
        w = jnp.where(tloc >= d, q * kd * jnp.exp(jnp.where(tloc >= d, bcum - bd, 0.0)), 0.0)
        for h in range(HG_H):
            sl = slice(h * HG_D, (h + 1) * HG_D)
            o_intra[h] = o_intra[h] + jnp.sum(w[:, sl], axis=-1, keepdims=True) * vd[:, sl]

    qb = q * jnp.exp(bcum)
    kh = k * jnp.exp(b_last - bcum)
    per_tile = SUBLANE // ts
    row8 = lax.broadcasted_iota(jnp.int32, (SUBLANE, HG_D), 0) // ts
    tiles = []
    for j in range(rows // SUBLANE):
        r8 = slice(j * SUBLANE, (j + 1) * SUBLANE)
        heads = []
        for h in range(HG_H):
            sl = slice(h * HG_D, (h + 1) * HG_D)
            o_inter = jnp.zeros((SUBLANE, HG_D), F32)
            for u in range(per_tile):
                seq = j * per_tile + u
                mine = row8 == u
                s0 = s0_ref[seq, h]
                o_inter = o_inter + jnp.dot(jnp.where(mine, qb[r8, sl], 0.0).astype(BF16), s0.astype(BF16),
                                            preferred_element_type=F32)
                upd = lax.dot_general(jnp.where(mine, kh[r8, sl], 0.0).astype(BF16), v[r8, sl].astype(BF16),
                                      (((0,), (0,)), ((), ())), preferred_element_type=F32)
                r_last = j * SUBLANE + u * ts + ts - 1
                dec = jnp.exp(bcum[r_last:r_last + 1, sl])
                s1_ref[seq, h] = jnp.broadcast_to(dec, (HG_D, HG_D)).T * s0 + upd
            heads.append(o_inter + o_intra[h][r8, :])
        tiles.append(jnp.concatenate(heads, axis=1))
    o = jnp.concatenate(tiles, axis=0)
    o_ref[...] = _hgrn_out(o, hg_ref[...], ng_ref[...]).astype(BF16)


def _hgrn_sample(z, lb, ng, s0, bs, ts, nb):
    rows = nb * ts
    col = lambda c0: pl.BlockSpec((rows, HG_W), lambda i: (i, c0 // HG_W))
    st = pl.BlockSpec((nb, HG_H, HG_D, HG_D), lambda i: (i, 0, 0, 0))
    return pl.pallas_call(
        functools.partial(_hgrn_sample_kernel, ts),
        grid=(bs // nb,),
        in_specs=[col(C_HQ), col(C_HF), col(C_HI), col(C_HG), _const_spec((1, HG_W)), _const_spec((1, HG_D)), st],
        out_specs=[pl.BlockSpec((rows, HG_W), lambda i: (i, 0)), st],
        out_shape=[jax.ShapeDtypeStruct((bs * ts, HG_W), BF16),
                   jax.ShapeDtypeStruct((bs, HG_H, HG_D, HG_D), F32)],
        compiler_params=_cparams(("parallel",)),
        name="hgrn_sample",
    )(z, z, z, z, lb, ng, s0)


def _layer_sample(x, p, cache_cmp, cache_sel, cache_win, state, page_table, lw, bs, ts):
    n = bs * ts
    assert SUBLANE % ts == 0 and n % SUBLANE == 0
    z = _norm_matmul(x, lw['g_pre_mix'], lw['w_in'], tm=min(512, n), tn=512)
    kv_c = z[:, C_KVC:C_KVC + 2 * KV_W]
    kv_s = z[:, C_KVS:C_KVS + 2 * KV_W]
    kv_w = z[:, C_KVW:C_KVW + 2 * KV_W]

    rows_last = lambda a: jnp.transpose(a, (0, 2, 3, 4, 1)).reshape(a.shape[0], 2, KV_W, a.shape[1])
    kc, vc = _compress_sample(page_table, rows_last(cache_cmp), lw['cmp_w1'], lw['cmp_w2'], lw['cmp_pos'])

    eye = jnp.eye(KVH, dtype=F32)
    q5 = z[:, C_Q:C_Q + NSA_W].reshape(bs, ts, KVH, GROUP, HD).transpose(0, 2, 3, 1, 4)
    qbd = jnp.einsum('bkgtd,kq->bkgtqd', q5, eye).reshape(bs, KVH * GROUP * ts, KV_W)
    g5 = z[:, C_GN:C_GN + 3 * NSA_HEADS].reshape(bs, ts, KVH, GROUP, 3).transpose(0, 2, 3, 1, 4)
    gl = jnp.pad(g5.reshape(bs, KVH * GROUP * ts, 3), ((0, 0), (0, 0), (0, LANE - 3)))
    pad_rows = lambda a: jnp.pad(a.reshape(bs, ts, 2 * KV_W), ((0, 0), (0, SUBLANE - ts), (0, 0)))
    o_rows = _nsa_sample(page_table, qbd, gl, kc, vc, pad_rows(kv_s), rows_last(cache_win), pad_rows(kv_w),
                         rows_last(cache_sel), ts)
    o6 = o_rows.reshape(bs, KVH, GROUP, ts, KVH, HD)
    o_nsa = jnp.stack([o6[:, kvh, :, :, kvh, :] for kvh in range(KVH)], axis=1)
    o_nsa = o_nsa.transpose(0, 3, 1, 2, 4).reshape(n, NSA_W).astype(BF16)

    o_hg, st = _hgrn_sample(z, lw['hg_lb'], lw['hg_norm'], state, bs, ts, nb=min(8, bs))
    h1 = _merge(o_nsa, o_hg, z, x, lw['w_bn'], lw['w_bh'], lw['w_out'], lw['g_post_mix'], tm=min(512, n))
    h2 = _mlp(h1, p, lw['w_up'], lw['w_down'], lw['w_gate'], lw['w_proj'],
              lw['g_pre_mlp'], lw['g_post_mlp'], lw['g_ple'], tm=min(512, n))
    kv6 = lambda a: a.reshape(bs, ts, 2, KVH, HD)
    win_buf = jnp.concatenate([cache_win, kv6(kv_w)], axis=1)[:, ts:]
    return h2, kv6(kv_c), kv6(kv_s), win_buf, st


def kernel(x_prompt, x_sample, cache_cmp_kv, cache_sel_kv, cache_win_kv, state_hgrn, page_table, p_prompt,
           p_sample, w_in, cmp_k_w1, cmp_k_w2, cmp_v_w1, cmp_v_w2, cmp_pos, hg_lb_logits, hg_norm, w_branch_nsa,
           w_branch_hgrn, w_out, norm_pre_mix, norm_post_mix, norm_pre_mlp, norm_post_mlp, w_mlp_up, w_mlp_down,
           norm_ple, w_ple_gate, w_ple_proj):
    depth = w_in.shape[0]
    b, t, d = x_prompt.shape
    bs, ts, _ = x_sample.shape
    lb_all = jnp.cumsum(jax.nn.softmax(hg_lb_logits.astype(F32), axis=0), axis=0)
    h_p = x_prompt.reshape(b * t, d)
    h_s = x_sample.reshape(bs * ts, d)
    outs = [[] for _ in range(8)]
    for i in range(depth):
        lw = _prep_layer(i, lb_all, w_in, cmp_k_w1, cmp_k_w2, cmp_v_w1, cmp_v_w2, cmp_pos, hg_norm, w_branch_nsa,
                         w_branch_hgrn, w_out, norm_pre_mix, norm_post_mix, norm_pre_mlp, norm_post_mlp, w_mlp_up,
                         w_mlp_down, norm_ple, w_ple_gate, w_ple_proj)
        h_p, *res_p = _layer_prompt(h_p, p_prompt[i].reshape(b * t, -1), lw, b, t)
        h_s, *res_s = _layer_sample(h_s, p_sample[i].reshape(bs * ts, -1), cache_cmp_kv[i], cache_sel_kv[i],
                                    cache_win_kv[i], state_hgrn[i], page_table, lw, bs, ts)
        for lst, v in zip(outs, res_p + res_s):
            lst.append(v)
    return (h_p.reshape(b, t, d), h_s.reshape(bs, ts, d)) + tuple(jnp.stack(lst, axis=0) for lst in outs)
```
